```python
import jax, jax.numpy as jnp
from jax import lax
import numpy as np

D_MODEL = 1024
BATCH = 8
SEQ = 4096
DEPTH = 4

F_WIDTH = D_MODEL
F_GROUP_DIM = 128
F_GROUPS = F_WIDTH // F_GROUP_DIM
RET_HEADS = 4
RET_DK = D_MODEL // RET_HEADS
RET_DV = D_MODEL // RET_HEADS
RET_QK = RET_HEADS * RET_DK
RET_V = RET_HEADS * RET_DV
CHUNK = 128
ROPE_BASE = 10000.0
DECAY_OFFSET_FWD = 0.0
DECAY_OFFSET_BWD = 0.5
D_FF = 2816
CONV_WIDTH = 3
N_MOD = 6
EPS = 1e-6
IN_SIZES = (F_WIDTH, RET_QK, RET_QK, RET_V, RET_V, D_MODEL, D_MODEL)
D_IN = F_WIDTH + 2 * RET_QK + 2 * RET_V + 2 * D_MODEL

kernel_name = 'hybrid_fourier_retention_convffn_encoder'


def rmsnorm(x, g):
    xf = x.astype(jnp.float32)
    y = xf * lax.rsqrt(jnp.mean(xf * xf, axis=-1, keepdims=True) + EPS)
    return (y * g.astype(jnp.float32)).astype(x.dtype)


def decay_logs(offset):
    expo = -(5.0 + offset) - jnp.arange(RET_HEADS, dtype=jnp.float32)
    return jnp.log1p(-jnp.exp2(expo))


def fourier_mix(u):
    B, S, _ = u.shape
    ug = u.astype(jnp.float32).reshape(B, S, F_GROUPS, F_GROUP_DIM)
    y = jnp.fft.fft2(ug, axes=(1, 3), norm='ortho').real
    return y.reshape(B, S, F_WIDTH).astype(u.dtype)


def rotary(t):
    S = t.shape[1]
    half = t.shape[-1] // 2
    inv_freq = ROPE_BASE ** (-jnp.arange(half, dtype=jnp.float32) / half)
    ang = jnp.arange(S, dtype=jnp.float32)[:, None] * inv_freq[None, :]
    cos = jnp.cos(ang)[None, :, None, :]
    sin = jnp.sin(ang)[None, :, None, :]
    t1, t2 = t[..., :half], t[..., half:]
    return jnp.concatenate([t1 * cos - t2 * sin, t1 * sin + t2 * cos], axis=-1)


def chunk_retention(q, k, v, log_gamma, strict):
    B, H, S, dk = q.shape
    dv = v.shape[-1]
    n_chunks = S // CHUNK
    qc = q.reshape(B, H, n_chunks, CHUNK, dk)
    kc = k.reshape(B, H, n_chunks, CHUNK, dk)
    vc = v.reshape(B, H, n_chunks, CHUNK, dv)
    j = jnp.arange(CHUNK, dtype=jnp.float32)
    diff = j[:, None] - j[None, :]
    keep = (diff > 0) if strict else (diff >= 0)
    dmask = jnp.where(keep[None], jnp.exp(log_gamma[:, None, None] * jnp.maximum(diff, 0.0)[None]), 0.0)
    scores = jnp.einsum('bhncd,bhnld->bhncl', qc, kc) * dmask[None, :, None]
    intra = jnp.einsum('bhncl,bhnle->bhnce', scores, vc)
    q_decay = jnp.exp(log_gamma[:, None] * (j + 1.0))[None, :, :, None]
    k_decay = jnp.exp(log_gamma[:, None] * (CHUNK - 1.0 - j))[None, :, :, None]
    chunk_decay = jnp.exp(log_gamma * CHUNK)[None, :, None, None]

    def step(state, xs):
        qi, ki, vi = xs
        cross = jnp.einsum('bhcd,bhde->bhce', qi, state) * q_decay
        state = state * chunk_decay + jnp.einsum('bhcd,bhce->bhde', ki * k_decay, vi)
        return state, cross

    state0 = jnp.zeros((B, H, dk, dv), jnp.float32)
    xs = (jnp.moveaxis(qc, 2, 0), jnp.moveaxis(kc, 2, 0), jnp.moveaxis(vc, 2, 0))
    _, cross = lax.scan(step, state0, xs)
    out = intra + jnp.moveaxis(cross, 0, 2)
    return out.reshape(B, H, S, dv)


def retention_branch(q, k, v, g):
    B, S, _ = q.shape
    dt = q.dtype
    qh = rotary(q.astype(jnp.float32).reshape(B, S, RET_HEADS, RET_DK)) * (RET_DK ** -0.5)
    kh = rotary(k.astype(jnp.float32).reshape(B, S, RET_HEADS, RET_DK))
    vh = v.astype(jnp.float32).reshape(B, S, RET_HEADS, RET_DV)
    qh = jnp.transpose(qh, (0, 2, 1, 3))
    kh = jnp.transpose(kh, (0, 2, 1, 3))
    vh = jnp.transpose(vh, (0, 2, 1, 3))
    fwd = chunk_retention(qh, kh, vh, decay_logs(DECAY_OFFSET_FWD), strict=False)
    bwd = jnp.flip(chunk_retention(jnp.flip(qh, axis=2), jnp.flip(kh, axis=2), jnp.flip(vh, axis=2),
                                   decay_logs(DECAY_OFFSET_BWD), strict=True), axis=2)
    y = jnp.transpose(fwd + bwd, (0, 2, 1, 3))
    mu = jnp.mean(y, axis=-1, keepdims=True)
    var = jnp.mean(jnp.square(y - mu), axis=-1, keepdims=True)
    y = (y - mu) * lax.rsqrt(var + EPS)
    y = y.reshape(B, S, RET_V) * jax.nn.silu(g.astype(jnp.float32))
    return y.astype(dt)


def token_mix(h, w_in, w_fourier, w_ret, w_out):
    proj = h @ w_in
    offsets = np.cumsum(IN_SIZES)[:-1].tolist()
    u_f, q, k, v, g, a_f, a_r = jnp.split(proj, offsets, axis=-1)
    y_f = fourier_mix(u_f) @ w_fourier
    y_r = retention_branch(q, k, v, g) @ w_ret
    merged = jax.nn.sigmoid(a_f) * y_f + jax.nn.sigmoid(a_r) * y_r
    return merged @ w_out


def conv_ffn(h, w_up, conv_w, conv_b, w_down):
    S = h.shape[1]
    u = h @ w_up
    pad = CONV_WIDTH // 2
    up = jnp.pad(u, ((0, 0), (pad, pad), (0, 0)))
    u = sum(up[:, i:i + S] * conv_w[i] for i in range(CONV_WIDTH)) + conv_b
    a, b = jnp.split(u, 2, axis=-1)
    return (jax.nn.gelu(a, approximate=False) * b) @ w_down


def setup_inputs(seed: int = 0) -> dict:
    key = jax.random.key(seed)
    ks = jax.random.split(key, 15)

    def nrm(k, shape, scale):
        return jax.random.normal(k, shape, jnp.float32) * scale

    return {
        'x': nrm(ks[0], (BATCH, SEQ, D_MODEL), 1.0),
        'c': nrm(ks[1], (BATCH, D_MODEL), 1.0),
        'norm1_g': 1.0 + nrm(ks[2], (DEPTH, D_MODEL), 0.02),
        'norm2_g': 1.0 + nrm(ks[3], (DEPTH, D_MODEL), 0.02),
        'ada_w': nrm(ks[4], (DEPTH, D_MODEL, N_MOD * D_MODEL), 0.5 * D_MODEL ** -0.5),
        'ada_b': nrm(ks[5], (DEPTH, N_MOD * D_MODEL), 0.02),
        'w_in': nrm(ks[6], (DEPTH, D_MODEL, D_IN), D_MODEL ** -0.5),
        'w_fourier': nrm(ks[7], (DEPTH, F_WIDTH, D_MODEL), F_WIDTH ** -0.5),
        'w_ret': nrm(ks[8], (DEPTH, RET_V, D_MODEL), RET_V ** -0.5),
        'w_out': nrm(ks[9], (DEPTH, D_MODEL, D_MODEL), D_MODEL ** -0.5),
        'ffn_up': nrm(ks[10], (DEPTH, D_MODEL, 2 * D_FF), D_MODEL ** -0.5),
        'conv_w': nrm(ks[11], (DEPTH, CONV_WIDTH, 2 * D_FF), CONV_WIDTH ** -0.5),
        'conv_b': nrm(ks[12], (DEPTH, 2 * D_FF), 0.02),
        'ffn_down': nrm(ks[13], (DEPTH, D_FF, D_MODEL), D_FF ** -0.5),
        'final_g': 1.0 + nrm(ks[14], (D_MODEL,), 0.02),
    }


def reference(x, c, norm1_g, norm2_g, ada_w, ada_b, w_in, w_fourier, w_ret, w_out,
              ffn_up, conv_w, conv_b, ffn_down, final_g):
    c_act = jax.nn.silu(c)
    for l in range(DEPTH):
        mod = (c_act @ ada_w[l] + ada_b[l])[:, None, :]
        sh1, sc1, g1, sh2, sc2, g2 = jnp.split(mod, N_MOD, axis=-1)
        h = rmsnorm(x, norm1_g[l]) * (1.0 + sc1) + sh1
        x = x + g1 * token_mix(h, w_in[l], w_fourier[l], w_ret[l], w_out[l])
        h = rmsnorm(x, norm2_g[l]) * (1.0 + sc2) + sh2
        x = x + g2 * conv_ffn(h, ffn_up[l], conv_w[l], conv_b[l], ffn_down[l])
    return rmsnorm(x, final_g)
```

```python
import functools

import numpy as np
import jax
import jax.numpy as jnp
from jax import lax
from jax.experimental import pallas as pl
from jax.experimental.pallas import tpu as pltpu

f32 = jnp.float32
bf16 = jnp.bfloat16

D = 1024
BATCH = 8
SEQ = 4096
DEPTH = 4
GROUP = 128
HEADS = 4
DK = D // HEADS
CHUNK = 128
ROPE_BASE = 10000.0
D_FF = 2816
N_MOD = 6
EPS = 1e-6
D_REST = 6 * D

R = 64
TM = 512
NT = SEQ // TM
RET_ROWS = 256
FFT_SLABS = 16
FF_TILE = 256
HALO = 8
VMEM_LIMIT = 56 * 1024 * 1024

_NT_DIMS = (((1,), (1,)), ((), ()))
_TN_DIMS = (((0,), (0,)), ((), ()))


def _resident(shape):
    nd = len(shape)
    return pl.BlockSpec(shape, lambda *_: (0,) * nd, pipeline_mode=pl.Buffered(1))


def _params(*sem):
    return pltpu.CompilerParams(dimension_semantics=sem, vmem_limit_bytes=VMEM_LIMIT)


def _group_dft_tables():
    n = np.arange(GROUP)
    ang = 2.0 * np.pi * np.outer(n, n) / GROUP
    s = 1.0 / np.sqrt(GROUP)
    return (np.cos(ang) * s).astype(np.float32), (-np.sin(ang) * s).astype(np.float32)


def _stage_tables():
    k = np.arange(R)
    ang = 2.0 * np.pi * np.outer(k, k) / R
    cr = np.cos(ang) / 8.0
    ci = -np.sin(ang) / 8.0
    stage1 = np.block([[cr, -ci], [ci, cr]])
    stage2 = np.concatenate([cr, -ci], axis=1)
    n2 = np.arange(R)[:, None, None]
    k1 = np.arange(R)[None, :, None]
    tw = 2.0 * np.pi * (n2 * k1) / SEQ * np.ones((1, 1, 128))
    return (stage1.astype(np.float32), stage2.astype(np.float32),
            np.cos(tw).astype(np.float32), (-np.sin(tw)).astype(np.float32))


def _retention_tables():
    j = np.arange(CHUNK, dtype=np.float64)
    diff = j[:, None] - j[None, :]
    dmask = np.zeros((2, HEADS, CHUNK, CHUNK))
    qdec = np.zeros((2, HEADS, CHUNK, DK))
    kdec = np.zeros((2, HEADS, CHUNK, DK))
    cdec = np.zeros((2, HEADS))
    for d, offset in enumerate((0.0, 0.5)):
        for h in range(HEADS):
            lg = np.log1p(-np.exp2(-(5.0 + offset) - h))
            if d == 0:
                dmask[d, h] = np.where(diff >= 0, np.exp(lg * np.maximum(diff, 0.0)), 0.0)
                qdec[d, h] = np.exp(lg * (j + 1.0))[:, None]
                kdec[d, h] = np.exp(lg * (CHUNK - 1.0 - j))[:, None]
            else:
                dmask[d, h] = np.where(diff < 0, np.exp(lg * np.maximum(-diff, 0.0)), 0.0)
                qdec[d, h] = np.exp(lg * (CHUNK - j))[:, None]
                kdec[d, h] = np.exp(lg * j)[:, None]
            cdec[d, h] = np.exp(lg * CHUNK)
    return (dmask.astype(np.float32), qdec.astype(np.float32),
            kdec.astype(np.float32), cdec)


_CG, _SG_NEG = _group_dft_tables()
_STAGE1, _STAGE2, _TW_RE, _TW_IM = _stage_tables()
_DMASK, _QDEC, _KDEC, _CDEC = _retention_tables()


def _rotary_tables():
    half = DK // 2
    inv_freq = ROPE_BASE ** (-jnp.arange(half, dtype=f32) / half)
    ang = jnp.arange(SEQ, dtype=f32)[:, None] * inv_freq[None, :]
    return jnp.cos(ang), jnp.sin(ang)


def _ada_kernel(c_ref, w_ref, b_ref, o_ref):
    c = c_ref[...]
    act = (c * jax.nn.sigmoid(c)).astype(bf16)
    o_ref[...] = jnp.dot(act, w_ref[...].astype(bf16), preferred_element_type=f32) + b_ref[...]


def _ada_call(c, ada_w, ada_b):
    return pl.pallas_call(
        _ada_kernel,
        out_shape=jax.ShapeDtypeStruct((DEPTH, N_MOD, BATCH, D), f32),
        grid=(DEPTH, N_MOD),
        in_specs=[
            pl.BlockSpec((BATCH, D), lambda l, j: (0, 0)),
            pl.BlockSpec((None, D, D), lambda l, j: (l, 0, j)),
            pl.BlockSpec((None, None, 1, D), lambda l, j: (l, j, 0, 0)),
        ],
        out_specs=pl.BlockSpec((None, None, BATCH, D), lambda l, j: (l, j, 0, 0)),
        compiler_params=_params("arbitrary", "arbitrary"),
        name="ada_mod",
    )(c, ada_w, ada_b.reshape(DEPTH, N_MOD, 1, D))


def _fold_kernel(w_ref, cg_ref, sg_ref, wr_ref, wi_ref):
    w = w_ref[...]
    wr_ref[...] = jnp.dot(w, cg_ref[...], preferred_element_type=f32,
                          precision=lax.Precision.HIGHEST).astype(bf16)
    wi_ref[...] = jnp.dot(w, sg_ref[...], preferred_element_type=f32,
                          precision=lax.Precision.HIGHEST).astype(bf16)


def _fold_call(w_in):
    n_groups = D // GROUP
    out = jax.ShapeDtypeStruct((DEPTH, D, D), bf16)
    blk = pl.BlockSpec((None, D, GROUP), lambda l, g: (l, 0, g))
    tab = pl.BlockSpec((GROUP, GROUP), lambda l, g: (0, 0))
    return pl.pallas_call(
        _fold_kernel,
        out_shape=(out, out),
        grid=(DEPTH, n_groups),
        in_specs=[blk, tab, tab],
        out_specs=(blk, blk),
        compiler_params=_params("arbitrary", "arbitrary"),
        name="fold_group_dft",
    )(w_in, jnp.asarray(_CG), jnp.asarray(_SG_NEG))


def _modulated_norm(x, gain, shift, scale):
    ms = jnp.mean(x * x, axis=-1, keepdims=True)
    return (x * lax.rsqrt(ms + EPS) * gain) * (1.0 + scale) + shift


def _inproj_kernel(x_ref, mod_ref, g_ref, cos_ref, sin_ref, wr_ref, wi_ref, w_ref,
                   zr_ref, zi_ref, q_ref, k_ref, v_ref, gs_ref, af_ref, ar_ref):
    hb = _modulated_norm(x_ref[...], g_ref[...], mod_ref[0], mod_ref[1]).astype(bf16)

    def slab_major(w):
        p = jnp.dot(hb, w[...], preferred_element_type=f32)
        return pltpu.einshape("abc->bac", p.reshape(TM // R, R, D))

    zr_ref[...] = slab_major(wr_ref)
    zi_ref[...] = slab_major(wi_ref)

    cos = cos_ref[...]
    sin = sin_ref[...]
    half = DK // 2

    def rotary(sec, o_ref, scale):
        p = jnp.dot(hb, w_ref[:, sec * D:(sec + 1) * D], preferred_element_type=f32)
        for h in range(HEADS):
            t1 = p[:, h * DK:h * DK + half]
            t2 = p[:, h * DK + half:(h + 1) * DK]
            o_ref[:, h * DK:h * DK + half] = ((t1 * cos - t2 * sin) * scale).astype(bf16)
            o_ref[:, h * DK + half:(h + 1) * DK] = ((t1 * sin + t2 * cos) * scale).astype(bf16)

    rotary(0, q_ref, DK ** -0.5)
    rotary(1, k_ref, 1.0)
    v_ref[...] = jnp.dot(hb, w_ref[:, 2 * D:3 * D], preferred_element_type=f32).astype(bf16)
    g = jnp.dot(hb, w_ref[:, 3 * D:4 * D], preferred_element_type=f32)
    gs_ref[...] = (g * jax.nn.sigmoid(g)).astype(bf16)
    af = jnp.dot(hb, w_ref[:, 4 * D:5 * D], preferred_element_type=f32)
    af_ref[...] = jax.nn.sigmoid(af).astype(bf16)
    ar = jnp.dot(hb, w_ref[:, 5 * D:6 * D], preferred_element_type=f32)
    ar_ref[...] = jax.nn.sigmoid(ar).astype(bf16)


def _inproj_call(x, mod, gain, cos, sin, wr, wi, w_rest):
    row = pl.BlockSpec((None, TM, D), lambda b, i: (b, i, 0))
    rot = pl.BlockSpec((TM, DK // 2), lambda b, i: (i, 0))
    slab = pl.BlockSpec((None, R, TM // R, D), lambda b, i: (b, 0, i, 0))
    act = jax.ShapeDtypeStruct((BATCH, SEQ, D), bf16)
    zt = jax.ShapeDtypeStruct((BATCH, R, R, D), f32)
    return pl.pallas_call(
        _inproj_kernel,
        out_shape=(zt, zt, act, act, act, act, act, act),
        grid=(BATCH, NT),
        in_specs=[
            row,
            pl.BlockSpec((N_MOD, None, 1, D), lambda b, i: (0, b, 0, 0)),
            pl.BlockSpec((1, D), lambda b, i: (0, 0)),
            rot, rot,
            _resident((D, D)), _resident((D, D)), _resident((D, D_REST)),
        ],
        out_specs=(slab, slab, row, row, row, row, row, row),
        compiler_params=_params("arbitrary", "arbitrary"),
        name="in_projection",
    )(x, mod, gain, cos, sin, wr, wi, w_rest)


def _dft1_kernel(zr_ref, zi_ref, m_ref, twr_ref, twi_ref, tr_ref, ti_ref, sr_ref, si_ref):
    m = m_ref[...]
    for s in range(FFT_SLABS):
        z = jnp.concatenate([zr_ref[s].astype(bf16), zi_ref[s].astype(bf16)], axis=0)
        t = jnp.dot(m, z, preferred_element_type=f32)
        t_re, t_im = t[:R], t[R:]
        w_re = jnp.tile(twr_ref[s], (1, D // 128))
        w_im = jnp.tile(twi_ref[s], (1, D // 128))
        sr_ref[s] = t_re * w_re - t_im * w_im
        si_ref[s] = t_re * w_im + t_im * w_re
    tr_ref[...] = pltpu.einshape("abc->bac", sr_ref[...]).astype(bf16)
    ti_ref[...] = pltpu.einshape("abc->bac", si_ref[...]).astype(bf16)


def _dft1_call(zr, zi):
    nsteps = R // FFT_SLABS
    zin = pl.BlockSpec((None, FFT_SLABS, R, D), lambda b, j: (b, j, 0, 0))
    tw = pl.BlockSpec((FFT_SLABS, R, 128), lambda b, j: (j, 0, 0))
    tout = pl.BlockSpec((None, R, FFT_SLABS, D), lambda b, j: (b, 0, j, 0))
    t2 = jax.ShapeDtypeStruct((BATCH, R, R, D), bf16)
    return pl.pallas_call(
        _dft1_kernel,
        out_shape=(t2, t2),
        grid=(BATCH, nsteps),
        in_specs=[zin, zin, _resident((2 * R, 2 * R)), tw, tw],
        out_specs=(tout, tout),
        scratch_shapes=[pltpu.VMEM((FFT_SLABS, R, D), f32), pltpu.VMEM((FFT_SLABS, R, D), f32)],
        compiler_params=_params("arbitrary", "arbitrary"),
        name="dft_stage1",
    )(zr, zi, jnp.asarray(_STAGE1, dtype=bf16), jnp.asarray(_TW_RE), jnp.asarray(_TW_IM))


def _dft2_kernel(tr_ref, ti_ref, m_ref, y_ref, s_ref):
    m = m_ref[...]
    for s in range(FFT_SLABS):
        t = jnp.concatenate([tr_ref[s], ti_ref[s]], axis=0)
        s_ref[s] = jnp.dot(m, t, preferred_element_type=f32)
    y_ref[...] = pltpu.einshape("abc->bac", s_ref[...]).astype(bf16)


def _dft2_call(tr, ti):
    nsteps = R // FFT_SLABS
    tin = pl.BlockSpec((None, FFT_SLABS, R, D), lambda b, j: (b, j, 0, 0))
    yout = pl.BlockSpec((None, R, FFT_SLABS, D), lambda b, j: (b, 0, j, 0))
    return pl.pallas_call(
        _dft2_kernel,
        out_shape=jax.ShapeDtypeStruct((BATCH, R, R, D), bf16),
        grid=(BATCH, nsteps),
        in_specs=[tin, tin, _resident((R, 2 * R))],
        out_specs=yout,
        scratch_shapes=[pltpu.VMEM((FFT_SLABS, R, D), f32)],
        compiler_params=_params("arbitrary", "arbitrary"),
        name="dft_stage2",
    )(tr, ti, jnp.asarray(_STAGE2, dtype=bf16))


def _ret_chunk(d, rows, q_ref, k_ref, v_ref, dm_ref, qd_ref, kd_ref, state_ref):
    outs = []
    for h in range(HEADS):
        cols = slice(h * DK, (h + 1) * DK)
        q = q_ref[rows, cols]
        k = k_ref[rows, cols]
        v = v_ref[rows, cols]
        scores = lax.dot_general(q, k, _NT_DIMS, preferred_element_type=f32) * dm_ref[d, h]
        intra = jnp.dot(scores.astype(bf16), v, preferred_element_type=f32)
        state = state_ref[h]
        cross = jnp.dot(q, state.astype(bf16), preferred_element_type=f32) * qd_ref[d, h]
        k_dec = (k.astype(f32) * kd_ref[d, h]).astype(bf16)
        state_ref[h] = state * float(_CDEC[d, h]) + lax.dot_general(
            k_dec, v, _TN_DIMS, preferred_element_type=f32)
        outs.append(intra + cross)
    return outs


def _ret_kernel(q_ref, k_ref, v_ref, gs_ref, dm_ref, qd_ref, kd_ref, o_ref, state_ref, bwd_ref):
    p = pl.program_id(1)
    j = pl.program_id(2)
    n_blocks = pl.num_programs(2)
    n_chunks = RET_ROWS // CHUNK

    @pl.when(j == 0)
    def _():
        state_ref[...] = jnp.zeros_like(state_ref)

    @pl.when(p == 0)
    def _():
        base = (n_blocks - 1 - j) * RET_ROWS
        for c in reversed(range(n_chunks)):
            rows = slice(c * CHUNK, (c + 1) * CHUNK)
            outs = _ret_chunk(1, rows, q_ref, k_ref, v_ref, dm_ref, qd_ref, kd_ref, state_ref)
            for h in range(HEADS):
                bwd_ref[pl.ds(pl.multiple_of(base + c * CHUNK, CHUNK), CHUNK),
                        h * DK:(h + 1) * DK] = outs[h]

    @pl.when(p == 1)
    def _():
        base = j * RET_ROWS
        for c in range(n_chunks):
            rows = slice(c * CHUNK, (c + 1) * CHUNK)
            outs = _ret_chunk(0, rows, q_ref, k_ref, v_ref, dm_ref, qd_ref, kd_ref, state_ref)
            for h in range(HEADS):
                cols = slice(h * DK, (h + 1) * DK)
                y = outs[h] + bwd_ref[pl.ds(pl.multiple_of(base + c * CHUNK, CHUNK), CHUNK), cols]
                mu = jnp.mean(y, axis=-1, keepdims=True)
                yc = y - mu
                var = jnp.mean(yc * yc, axis=-1, keepdims=True)
                yn = yc * lax.rsqrt(var + EPS)
                o_ref[rows, cols] = (yn * gs_ref[rows, cols].astype(f32)).astype(bf16)


def _ret_call(q, k, v, gs):
    nb = SEQ // RET_ROWS

    def scan_idx(b, p, j):
        return (b, j + (1 - p) * (nb - 1 - 2 * j), 0)

    def finish_idx(b, p, j):
        return (b, p * j, 0)

    scan = pl.BlockSpec((None, RET_ROWS, D), scan_idx)
    fin = pl.BlockSpec((None, RET_ROWS, D), finish_idx)
    return pl.pallas_call(
        _ret_kernel,
        out_shape=jax.ShapeDtypeStruct((BATCH, SEQ, D), bf16),
        grid=(BATCH, 2, nb),
        in_specs=[scan, scan, scan, fin,
                  _resident((2, HEADS, CHUNK, CHUNK)),
                  _resident((2, HEADS, CHUNK, DK)),
                  _resident((2, HEADS, CHUNK, DK))],
        out_specs=fin,
        scratch_shapes=[pltpu.VMEM((HEADS, DK, DK), f32), pltpu.VMEM((SEQ, D), f32)],
        compiler_params=_params("arbitrary", "arbitrary", "arbitrary"),
        name="retention",
    )(q, k, v, gs, jnp.asarray(_DMASK), jnp.asarray(_QDEC), jnp.asarray(_KDEC))


def _merge_kernel(x_ref, mod_ref, yf_ref, yr_ref, af_ref, ar_ref, wf_ref, wr_ref, wo_ref, o_ref):
    a = jnp.dot(yf_ref[...], wf_ref[...], preferred_element_type=f32)
    b = jnp.dot(yr_ref[...], wr_ref[...], preferred_element_type=f32)
    merged = af_ref[...].astype(f32) * a + ar_ref[...].astype(f32) * b
    o = jnp.dot(merged.astype(bf16), wo_ref[...], preferred_element_type=f32)
    o_ref[...] = x_ref[...] + mod_ref[2] * o


def _merge_call(x, mod, yf, yr, af, ar, wf, wr, wo):
    row = pl.BlockSpec((None, TM, D), lambda b, i: (b, i, 0))
    return pl.pallas_call(
        _merge_kernel,
        out_shape=jax.ShapeDtypeStruct((BATCH, SEQ, D), f32),
        grid=(BATCH, NT),
        in_specs=[row, pl.BlockSpec((N_MOD, None, 1, D), lambda b, i: (0, b, 0, 0)),
                  row, row, row, row,
                  _resident((D, D)), _resident((D, D)), _resident((D, D))],
        out_specs=row,
        compiler_params=_params("arbitrary", "arbitrary"),
        name="merge_out_projection",
    )(x, mod, yf, yr, af, ar, wf, wr, wo)


def _ffn_kernel(x_ref, xp_ref, xn_ref, mod_ref, g_ref, wup_ref, cw_ref, cb_ref, wdn_ref, fg_ref,
                o_ref, act_ref, *, final):
    i = pl.program_id(1)
    rows = TM + 2 * HALO
    gain, shift, scale = g_ref[...], mod_ref[3], mod_ref[4]
    keep_prev = (i > 0).astype(f32)
    keep_next = (i < pl.num_programs(1) - 1).astype(f32)
    h = jnp.concatenate([
        _modulated_norm(xp_ref[...], gain, shift, scale) * keep_prev,
        _modulated_norm(x_ref[...], gain, shift, scale),
        _modulated_norm(xn_ref[...], gain, shift, scale) * keep_next,
    ], axis=0).astype(bf16)

    def conv(col):
        u = jnp.dot(h, wup_ref[:, col:col + FF_TILE], preferred_element_type=f32)
        w = cw_ref[:, col:col + FF_TILE]
        full = (pltpu.roll(u, 1, 0) * w[0:1] + u * w[1:2] + pltpu.roll(u, rows - 1, 0) * w[2:3]
                + cb_ref[:, col:col + FF_TILE])
        return full[HALO:HALO + TM]

    for t in range(D_FF // FF_TILE):
        a = conv(t * FF_TILE)
        b = conv(D_FF + t * FF_TILE)
        gelu = 0.5 * a * (1.0 + lax.erf(a * (2.0 ** -0.5)))
        act_ref[:, t * FF_TILE:(t + 1) * FF_TILE] = (gelu * b).astype(bf16)

    y = jnp.dot(act_ref[...], wdn_ref[...], preferred_element_type=f32)
    out = x_ref[...] + mod_ref[5] * y
    if final:
        ms = jnp.mean(out * out, axis=-1, keepdims=True)
        out = out * lax.rsqrt(ms + EPS) * fg_ref[...]
    o_ref[...] = out


def _ffn_call(x, mod, gain, wup, cw, cb, wdn, final_g, final):
    tiles = TM // HALO
    row = pl.BlockSpec((None, TM, D), lambda b, i: (b, i, 0))
    prev = pl.BlockSpec((None, HALO, D), lambda b, i: (b, jnp.maximum(i * tiles - 1, 0), 0))
    nxt = pl.BlockSpec((None, HALO, D),
                       lambda b, i: (b, jnp.minimum((i + 1) * tiles, SEQ // HALO - 1), 0))
    return pl.pallas_call(
        functools.partial(_ffn_kernel, final=final),
        out_shape=jax.ShapeDtypeStruct((BATCH, SEQ, D), f32),
        grid=(BATCH, NT),
        in_specs=[row, prev, nxt,
                  pl.BlockSpec((N_MOD, None, 1, D), lambda b, i: (0, b, 0, 0)),
                  pl.BlockSpec((1, D), lambda b, i: (0, 0)),
                  _resident((D, 2 * D_FF)), _resident((3, 2 * D_FF)), _resident((1, 2 * D_FF)),
                  _resident((D_FF, D)),
                  pl.BlockSpec((1, D), lambda b, i: (0, 0))],
        out_specs=row,
        scratch_shapes=[pltpu.VMEM((TM, D_FF), bf16)],
        compiler_params=_params("arbitrary", "arbitrary"),
        name="conv_ffn_final" if final else "conv_ffn",
    )(x, x, x, mod, gain, wup, cw, cb, wdn, final_g)


def kernel(x, c, norm1_g, norm2_g, ada_w, ada_b, w_in, w_fourier, w_ret, w_out,
           ffn_up, conv_w, conv_b, ffn_down, final_g):
    assert x.shape == (BATCH, SEQ, D) and c.shape == (BATCH, D)
    mod_all = _ada_call(c, ada_w, ada_b).reshape(DEPTH, N_MOD, BATCH, 1, D)
    wfr_all, wfi_all = _fold_call(w_in)
    cos, sin = _rotary_tables()
    final_gain = final_g.reshape(1, D)

    for l in range(DEPTH):
        mod = mod_all[l]
        zr, zi, q, k, v, gs, af, ar = _inproj_call(
            x, mod, norm1_g[l].reshape(1, D), cos, sin,
            wfr_all[l], wfi_all[l], w_in[l, :, D:].astype(bf16))
        tr, ti = _dft1_call(zr, zi)
        yf = _dft2_call(tr, ti).reshape(BATCH, SEQ, D)
        yr = _ret_call(q, k, v, gs)
        x = _merge_call(x, mod, yf, yr, af, ar, w_fourier[l].astype(bf16),
                        w_ret[l].astype(bf16), w_out[l].astype(bf16))
        x = _ffn_call(x, mod, norm2_g[l].reshape(1, D), ffn_up[l].astype(bf16), conv_w[l],
                      conv_b[l].reshape(1, 2 * D_FF), ffn_down[l].astype(bf16), final_gain,
                      final=(l == DEPTH - 1))
    return x
```

```python
import functools

import numpy as np
import jax
import jax.numpy as jnp
from jax import lax
from jax.experimental import pallas as pl
from jax.experimental.pallas import tpu as pltpu

f32 = jnp.float32
bf16 = jnp.bfloat16

D = 1024
BATCH = 8
SEQ = 4096
DEPTH = 4
GROUP = 128
HEADS = 4
DK = D // HEADS
ROPE_BASE = 10000.0
D_FF = 2816
N_MOD = 6
EPS = 1e-6
D_REST = 6 * D

R = 64
TM = 512
NT = SEQ // TM
RET_CHUNK = 256
RET_ROWS = 1024
FFT_SLABS = 16
FF_TILE = 256
HALO = 8
VMEM_LIMIT = 56 * 1024 * 1024

_NT_DIMS = (((1,), (1,)), ((), ()))
_TN_DIMS = (((0,), (0,)), ((), ()))


def _resident(shape):
    nd = len(shape)
    return pl.BlockSpec(shape, lambda *_: (0,) * nd, pipeline_mode=pl.Buffered(1))


def _layer_resident(shape, layer):
    nd = len(shape)
    return pl.BlockSpec((None,) + tuple(shape), lambda *_: (layer,) + (0,) * nd,
                        pipeline_mode=pl.Buffered(1))


def _params(*sem):
    return pltpu.CompilerParams(dimension_semantics=sem, vmem_limit_bytes=VMEM_LIMIT)


def _group_dft_tables():
    n = np.arange(GROUP)
    ang = 2.0 * np.pi * np.outer(n, n) / GROUP
    s = 1.0 / np.sqrt(GROUP)
    return (np.cos(ang) * s).astype(np.float32), (-np.sin(ang) * s).astype(np.float32)


def _stage_tables():
    k = np.arange(R)
    ang = 2.0 * np.pi * np.outer(k, k) / R
    cr = np.cos(ang) / 8.0
    ci = -np.sin(ang) / 8.0
    stage1 = np.block([[cr, -ci], [ci, cr]])
    stage2 = np.concatenate([cr, -ci], axis=1)
    n2 = np.arange(R)[:, None, None]
    k1 = np.arange(R)[None, :, None]
    tw = 2.0 * np.pi * (n2 * k1) / SEQ * np.ones((1, 1, 128))
    return (stage1.astype(np.float32), stage2.astype(np.float32),
            np.cos(tw).astype(np.float32), (-np.sin(tw)).astype(np.float32))


def _retention_tables():
    c = RET_CHUNK
    j = np.arange(c, dtype=np.float64)
    diff = j[:, None] - j[None, :]
    ones = np.ones((1, DK))
    mask = np.zeros((HEADS, c, c))
    qdf, qdb, kdf, kdb = (np.zeros((HEADS, c, DK)) for _ in range(4))
    cdf, cdb = np.zeros(HEADS), np.zeros(HEADS)
    for h in range(HEADS):
        lf = np.log1p(-np.exp2(-5.0 - h))
        lb = np.log1p(-np.exp2(-5.5 - h))
        mask[h] = np.where(diff >= 0, np.exp(lf * np.maximum(diff, 0.0)),
                           np.exp(lb * np.maximum(-diff, 0.0)))
        qdf[h] = np.exp(lf * (j + 1.0))[:, None] * ones
        kdf[h] = np.exp(lf * (c - 1.0 - j))[:, None] * ones
        qdb[h] = np.exp(lb * (c - j))[:, None] * ones
        kdb[h] = np.exp(lb * j)[:, None] * ones
        cdf[h] = np.exp(lf * c)
        cdb[h] = np.exp(lb * c)
    f = np.float32
    return mask.astype(f), qdf.astype(f), qdb.astype(f), kdf.astype(f), kdb.astype(f), cdf, cdb


_CG, _SG_NEG = _group_dft_tables()
_STAGE1, _STAGE2, _TW_RE, _TW_IM = _stage_tables()
_RMASK, _QDF, _QDB, _KDF, _KDB, _CDF, _CDB = _retention_tables()


def _rotary_tables():
    half = DK // 2
    inv_freq = ROPE_BASE ** (-jnp.arange(half, dtype=f32) / half)
    ang = jnp.arange(SEQ, dtype=f32)[:, None] * inv_freq[None, :]
    return jnp.cos(ang), jnp.sin(ang)


def _ada_kernel(c_ref, w_ref, b_ref, o_ref):
    c = c_ref[...]
    act = (c * jax.nn.sigmoid(c)).astype(bf16)
    o_ref[...] = jnp.dot(act, w_ref[...].astype(bf16), preferred_element_type=f32) + b_ref[...]


def _ada_call(c, ada_w, ada_b):
    return pl.pallas_call(
        _ada_kernel,
        out_shape=jax.ShapeDtypeStruct((DEPTH, N_MOD, BATCH, D), f32),
        grid=(DEPTH, N_MOD),
        in_specs=[
            pl.BlockSpec((BATCH, D), lambda l, j: (0, 0)),
            pl.BlockSpec((None, D, D), lambda l, j: (l, 0, j)),
            pl.BlockSpec((None, None, 1, D), lambda l, j: (l, j, 0, 0)),
        ],
        out_specs=pl.BlockSpec((None, None, BATCH, D), lambda l, j: (l, j, 0, 0)),
        compiler_params=_params("arbitrary", "arbitrary"),
        name="ada_mod",
    )(c, ada_w, ada_b.reshape(DEPTH, N_MOD, 1, D))


def _fold_kernel(w_ref, cg_ref, sg_ref, wr_ref, wi_ref):
    w = w_ref[...]
    wr_ref[...] = jnp.dot(w, cg_ref[...], preferred_element_type=f32,
                          precision=lax.Precision.HIGHEST).astype(bf16)
    wi_ref[...] = jnp.dot(w, sg_ref[...], preferred_element_type=f32,
                          precision=lax.Precision.HIGHEST).astype(bf16)


def _fold_call(w_in):
    n_groups = D // GROUP
    out = jax.ShapeDtypeStruct((DEPTH, D, D), bf16)
    blk = pl.BlockSpec((None, D, GROUP), lambda l, g: (l, 0, g))
    tab = pl.BlockSpec((GROUP, GROUP), lambda l, g: (0, 0))
    return pl.pallas_call(
        _fold_kernel,
        out_shape=(out, out),
        grid=(DEPTH, n_groups),
        in_specs=[blk, tab, tab],
        out_specs=(blk, blk),
        compiler_params=_params("arbitrary", "arbitrary"),
        name="fold_group_dft",
    )(w_in, jnp.asarray(_CG), jnp.asarray(_SG_NEG))


def _modulated_norm(x, gain, shift, scale):
    ms = jnp.mean(x * x, axis=-1, keepdims=True)
    return (x * lax.rsqrt(ms + EPS) * gain) * (1.0 + scale) + shift


def _inproj_kernel(x_ref, mod_ref, g_ref, cos_ref, sin_ref, wr_ref, wi_ref, w_ref,
                   zr_ref, zi_ref, q_ref, k_ref, v_ref, gs_ref, af_ref, ar_ref):
    hb = _modulated_norm(x_ref[...], g_ref[...], mod_ref[0], mod_ref[1]).astype(bf16)

    def slab_major(w):
        p = jnp.dot(hb, w[...], preferred_element_type=f32)
        return pltpu.einshape("abc->bac", p.reshape(TM // R, R, D))

    zr_ref[...] = slab_major(wr_ref)
    zi_ref[...] = slab_major(wi_ref)

    cos = cos_ref[...]
    sin = sin_ref[...]
    half = DK // 2

    def rotary(sec, o_ref, scale):
        p = jnp.dot(hb, w_ref[:, sec * D:(sec + 1) * D], preferred_element_type=f32)
        for h in range(HEADS):
            t1 = p[:, h * DK:h * DK + half]
            t2 = p[:, h * DK + half:(h + 1) * DK]
            o_ref[:, h * DK:h * DK + half] = ((t1 * cos - t2 * sin) * scale).astype(bf16)
            o_ref[:, h * DK + half:(h + 1) * DK] = ((t1 * sin + t2 * cos) * scale).astype(bf16)

    rotary(0, q_ref, DK ** -0.5)
    rotary(1, k_ref, 1.0)
    v_ref[...] = jnp.dot(hb, w_ref[:, 2 * D:3 * D], preferred_element_type=f32).astype(bf16)
    g = jnp.dot(hb, w_ref[:, 3 * D:4 * D], preferred_element_type=f32)
    gs_ref[...] = (g * jax.nn.sigmoid(g)).astype(bf16)
    af = jnp.dot(hb, w_ref[:, 4 * D:5 * D], preferred_element_type=f32)
    af_ref[...] = jax.nn.sigmoid(af).astype(bf16)
    ar = jnp.dot(hb, w_ref[:, 5 * D:6 * D], preferred_element_type=f32)
    ar_ref[...] = jax.nn.sigmoid(ar).astype(bf16)


def _inproj_call(layer, x, mod, gain, cos, sin, wr, wi, w_rest):
    row = pl.BlockSpec((None, TM, D), lambda b, i: (b, i, 0))
    rot = pl.BlockSpec((TM, DK // 2), lambda b, i: (i, 0))
    slab = pl.BlockSpec((None, R, TM // R, D), lambda b, i: (b, 0, i, 0))
    act = jax.ShapeDtypeStruct((BATCH, SEQ, D), bf16)
    zt = jax.ShapeDtypeStruct((BATCH, R, R, D), f32)
    return pl.pallas_call(
        _inproj_kernel,
        out_shape=(zt, zt, act, act, act, act, act, act),
        grid=(BATCH, NT),
        in_specs=[
            row,
            pl.BlockSpec((N_MOD, None, 1, D), lambda b, i: (0, b, 0, 0)),
            pl.BlockSpec((1, D), lambda b, i: (0, 0)),
            rot, rot,
            _layer_resident((D, D), layer), _layer_resident((D, D), layer),
            _layer_resident((D, D_REST), layer),
        ],
        out_specs=(slab, slab, row, row, row, row, row, row),
        compiler_params=_params("arbitrary", "arbitrary"),
        name="in_projection",
    )(x, mod, gain, cos, sin, wr, wi, w_rest)


def _dft1_kernel(zr_ref, zi_ref, m_ref, twr_ref, twi_ref, tr_ref, ti_ref, sr_ref, si_ref):
    m = m_ref[...]
    for s in range(FFT_SLABS):
        z = jnp.concatenate([zr_ref[s].astype(bf16), zi_ref[s].astype(bf16)], axis=0)
        t = jnp.dot(m, z, preferred_element_type=f32)
        t_re, t_im = t[:R], t[R:]
        w_re = jnp.tile(twr_ref[s], (1, D // 128))
        w_im = jnp.tile(twi_ref[s], (1, D // 128))
        sr_ref[s] = t_re * w_re - t_im * w_im
        si_ref[s] = t_re * w_im + t_im * w_re
    tr_ref[...] = pltpu.einshape("abc->bac", sr_ref[...]).astype(bf16)
    ti_ref[...] = pltpu.einshape("abc->bac", si_ref[...]).astype(bf16)


def _dft1_call(zr, zi):
    nsteps = R // FFT_SLABS
    zin = pl.BlockSpec((None, FFT_SLABS, R, D), lambda b, j: (b, j, 0, 0))
    tw = pl.BlockSpec((FFT_SLABS, R, 128), lambda b, j: (j, 0, 0))
    tout = pl.BlockSpec((None, R, FFT_SLABS, D), lambda b, j: (b, 0, j, 0))
    t2 = jax.ShapeDtypeStruct((BATCH, R, R, D), bf16)
    return pl.pallas_call(
        _dft1_kernel,
        out_shape=(t2, t2),
        grid=(BATCH, nsteps),
        in_specs=[zin, zin, _resident((2 * R, 2 * R)), tw, tw],
        out_specs=(tout, tout),
        scratch_shapes=[pltpu.VMEM((FFT_SLABS, R, D), f32), pltpu.VMEM((FFT_SLABS, R, D), f32)],
        compiler_params=_params("arbitrary", "arbitrary"),
        name="dft_stage1",
    )(zr, zi, jnp.asarray(_STAGE1, dtype=bf16), jnp.asarray(_TW_RE), jnp.asarray(_TW_IM))


def _dft2_kernel(tr_ref, ti_ref, m_ref, y_ref, s_ref):
    m = m_ref[...]
    for s in range(FFT_SLABS):
        t = jnp.concatenate([tr_ref[s], ti_ref[s]], axis=0)
        s_ref[s] = jnp.dot(m, t, preferred_element_type=f32)
    y_ref[...] = pltpu.einshape("abc->bac", s_ref[...]).astype(bf16)


def _dft2_call(tr, ti):
    nsteps = R // FFT_SLABS
    tin = pl.BlockSpec((None, FFT_SLABS, R, D), lambda b, j: (b, j, 0, 0))
    yout = pl.BlockSpec((None, R, FFT_SLABS, D), lambda b, j: (b, 0, j, 0))
    return pl.pallas_call(
        _dft2_kernel,
        out_shape=jax.ShapeDtypeStruct((BATCH, R, R, D), bf16),
        grid=(BATCH, nsteps),
        in_specs=[tin, tin, _resident((R, 2 * R))],
        out_specs=yout,
        scratch_shapes=[pltpu.VMEM((FFT_SLABS, R, D), f32)],
        compiler_params=_params("arbitrary", "arbitrary"),
        name="dft_stage2",
    )(tr, ti, jnp.asarray(_STAGE2, dtype=bf16))


def _ret_kernel(q_ref, k_ref, v_ref, gs_ref, m_ref, qdf_ref, qdb_ref, kdf_ref, kdb_ref,
                o_ref, sf_ref, sb_ref, sb_all_ref):
    p = pl.program_id(1)
    j = pl.program_id(2)
    n_chunks = RET_ROWS // RET_CHUNK

    @pl.when(j == 0)
    def _():
        sf_ref[...] = jnp.zeros_like(sf_ref)
        sb_ref[...] = jnp.zeros_like(sb_ref)

    @pl.when(p == 0)
    def _():
        block = pl.num_programs(2) - 1 - j
        for c in reversed(range(n_chunks)):
            rows = slice(c * RET_CHUNK, (c + 1) * RET_CHUNK)
            for h in range(HEADS):
                cols = slice(h * DK, (h + 1) * DK)
                state = sb_ref[h]
                sb_all_ref[block * n_chunks + c, h] = state.astype(bf16)
                k_dec = (k_ref[rows, cols].astype(f32) * kdb_ref[h]).astype(bf16)
                sb_ref[h] = state * float(_CDB[h]) + lax.dot_general(
                    k_dec, v_ref[rows, cols], _TN_DIMS, preferred_element_type=f32)

    @pl.when(p == 1)
    def _():
        for c in range(n_chunks):
            rows = slice(c * RET_CHUNK, (c + 1) * RET_CHUNK)
            for h in range(HEADS):
                cols = slice(h * DK, (h + 1) * DK)
                q = q_ref[rows, cols]
                k = k_ref[rows, cols]
                v = v_ref[rows, cols]
                scores = lax.dot_general(q, k, _NT_DIMS, preferred_element_type=f32) * m_ref[h]
                qf = q.astype(f32)
                state = sf_ref[h]
                lhs = jnp.concatenate([scores.astype(bf16),
                                       (qf * qdf_ref[h]).astype(bf16),
                                       (qf * qdb_ref[h]).astype(bf16)], axis=1)
                rhs = jnp.concatenate([v, state.astype(bf16), sb_all_ref[j * n_chunks + c, h]],
                                      axis=0)
                y = jnp.dot(lhs, rhs, preferred_element_type=f32)
                k_dec = (k.astype(f32) * kdf_ref[h]).astype(bf16)
                sf_ref[h] = state * float(_CDF[h]) + lax.dot_general(
                    k_dec, v, _TN_DIMS, preferred_element_type=f32)
                mu = jnp.mean(y, axis=-1, keepdims=True)
                yc = y - mu
                var = jnp.mean(yc * yc, axis=-1, keepdims=True)
                yn = yc * lax.rsqrt(var + EPS)
                o_ref[rows, cols] = (yn * gs_ref[rows, cols].astype(f32)).astype(bf16)


def _ret_call(q, k, v, gs):
    nb = SEQ // RET_ROWS

    def scan_idx(b, p, j):
        return (b, j + (1 - p) * (nb - 1 - 2 * j), 0)

    def finish_idx(b, p, j):
        return (b, p * j, 0)

    scan = pl.BlockSpec((None, RET_ROWS, D), scan_idx)
    fin = pl.BlockSpec((None, RET_ROWS, D), finish_idx)
    table = _resident((HEADS, RET_CHUNK, DK))
    return pl.pallas_call(
        _ret_kernel,
        out_shape=jax.ShapeDtypeStruct((BATCH, SEQ, D), bf16),
        grid=(BATCH, 2, nb),
        in_specs=[fin, scan, scan, fin, _resident((HEADS, RET_CHUNK, RET_CHUNK)),
                  table, table, table, table],
        out_specs=fin,
        scratch_shapes=[pltpu.VMEM((HEADS, DK, DK), f32), pltpu.VMEM((HEADS, DK, DK), f32),
                        pltpu.VMEM((SEQ // RET_CHUNK, HEADS, DK, DK), bf16)],
        compiler_params=_params("arbitrary", "arbitrary", "arbitrary"),
        name="retention",
    )(q, k, v, gs, jnp.asarray(_RMASK), jnp.asarray(_QDF), jnp.asarray(_QDB),
      jnp.asarray(_KDF), jnp.asarray(_KDB))


def _merge_kernel(x_ref, mod_ref, yf_ref, yr_ref, af_ref, ar_ref, wf_ref, wr_ref, wo_ref, o_ref):
    a = jnp.dot(yf_ref[...], wf_ref[...], preferred_element_type=f32)
    b = jnp.dot(yr_ref[...], wr_ref[...], preferred_element_type=f32)
    merged = af_ref[...].astype(f32) * a + ar_ref[...].astype(f32) * b
    o = jnp.dot(merged.astype(bf16), wo_ref[...], preferred_element_type=f32)
    o_ref[...] = x_ref[...] + mod_ref[2] * o


def _merge_call(layer, x, mod, yf, yr, af, ar, wf, wr, wo):
    row = pl.BlockSpec((None, TM, D), lambda b, i: (b, i, 0))
    return pl.pallas_call(
        _merge_kernel,
        out_shape=jax.ShapeDtypeStruct((BATCH, SEQ, D), f32),
        grid=(BATCH, NT),
        in_specs=[row, pl.BlockSpec((N_MOD, None, 1, D), lambda b, i: (0, b, 0, 0)),
                  row, row, row, row,
                  _layer_resident((D, D), layer), _layer_resident((D, D), layer),
                  _layer_resident((D, D), layer)],
        out_specs=row,
        compiler_params=_params("arbitrary", "arbitrary"),
        name="merge_out_projection",
    )(x, mod, yf, yr, af, ar, wf, wr, wo)


def _ffn_kernel(x_ref, xp_ref, xn_ref, mod_ref, g_ref, wup_ref, cw_ref, cb_ref, wdn_ref, fg_ref,
                o_ref, act_ref, *, final):
    i = pl.program_id(1)
    rows = TM + 2 * HALO
    gain, shift, scale = g_ref[...], mod_ref[3], mod_ref[4]
    keep_prev = (i > 0).astype(f32)
    keep_next = (i < pl.num_programs(1) - 1).astype(f32)
    h = jnp.concatenate([
        _modulated_norm(xp_ref[...], gain, shift, scale) * keep_prev,
        _modulated_norm(x_ref[...], gain, shift, scale),
        _modulated_norm(xn_ref[...], gain, shift, scale) * keep_next,
    ], axis=0).astype(bf16)

    def conv(col):
        u = jnp.dot(h, wup_ref[:, col:col + FF_TILE], preferred_element_type=f32)
        w = cw_ref[:, col:col + FF_TILE]
        full = (pltpu.roll(u, 1, 0) * w[0:1] + u * w[1:2] + pltpu.roll(u, rows - 1, 0) * w[2:3]
                + cb_ref[:, col:col + FF_TILE])
        return full[HALO:HALO + TM]

    for t in range(D_FF // FF_TILE):
        a = conv(t * FF_TILE)
        b = conv(D_FF + t * FF_TILE)
        gelu = 0.5 * a * (1.0 + lax.erf(a * (2.0 ** -0.5)))
        act_ref[:, t * FF_TILE:(t + 1) * FF_TILE] = (gelu * b).astype(bf16)

    y = jnp.dot(act_ref[...], wdn_ref[...], preferred_element_type=f32)
    out = x_ref[...] + mod_ref[5] * y
    if final:
        ms = jnp.mean(out * out, axis=-1, keepdims=True)
        out = out * lax.rsqrt(ms + EPS) * fg_ref[...]
    o_ref[...] = out


def _ffn_call(layer, x, mod, gain, wup, cw, cb, wdn, final_g, final):
    tiles = TM // HALO
    row = pl.BlockSpec((None, TM, D), lambda b, i: (b, i, 0))
    prev = pl.BlockSpec((None, HALO, D), lambda b, i: (b, jnp.maximum(i * tiles - 1, 0), 0))
    nxt = pl.BlockSpec((None, HALO, D),
                       lambda b, i: (b, jnp.minimum((i + 1) * tiles, SEQ // HALO - 1), 0))
    return pl.pallas_call(
        functools.partial(_ffn_kernel, final=final),
        out_shape=jax.ShapeDtypeStruct((BATCH, SEQ, D), f32),
        grid=(BATCH, NT),
        in_specs=[row, prev, nxt,
                  pl.BlockSpec((N_MOD, None, 1, D), lambda b, i: (0, b, 0, 0)),
                  pl.BlockSpec((1, D), lambda b, i: (0, 0)),
                  _layer_resident((D, 2 * D_FF), layer), _layer_resident((3, 2 * D_FF), layer),
                  _layer_resident((1, 2 * D_FF), layer), _layer_resident((D_FF, D), layer),
                  pl.BlockSpec((1, D), lambda b, i: (0, 0))],
        out_specs=row,
        scratch_shapes=[pltpu.VMEM((TM, D_FF), bf16)],
        compiler_params=_params("arbitrary", "arbitrary"),
        name="conv_ffn_final" if final else "conv_ffn",
    )(x, x, x, mod, gain, wup, cw, cb, wdn, final_g)


def kernel(x, c, norm1_g, norm2_g, ada_w, ada_b, w_in, w_fourier, w_ret, w_out,
           ffn_up, conv_w, conv_b, ffn_down, final_g):
    assert x.shape == (BATCH, SEQ, D) and c.shape == (BATCH, D)
    mod_all = _ada_call(c, ada_w, ada_b).reshape(DEPTH, N_MOD, BATCH, 1, D)
    wfr_all, wfi_all = _fold_call(w_in)
    cos, sin = _rotary_tables()
    final_gain = final_g.reshape(1, D)
    w_rest = w_in[:, :, D:].astype(bf16)
    wf, wr, wo = w_fourier.astype(bf16), w_ret.astype(bf16), w_out.astype(bf16)
    wup, wdn = ffn_up.astype(bf16), ffn_down.astype(bf16)

    for l in range(DEPTH):
        mod = mod_all[l]
        zr, zi, q, k, v, gs, af, ar = _inproj_call(
            l, x, mod, norm1_g[l].reshape(1, D), cos, sin, wfr_all, wfi_all, w_rest)
        tr, ti = _dft1_call(zr, zi)
        yf = _dft2_call(tr, ti).reshape(BATCH, SEQ, D)
        yr = _ret_call(q, k, v, gs)
        x = _merge_call(l, x, mod, yf, yr, af, ar, wf, wr, wo)
        x = _ffn_call(l, x, mod, norm2_g[l].reshape(1, D), wup, conv_w,
                      conv_b.reshape(DEPTH, 1, 2 * D_FF), wdn, final_gain,
                      final=(l == DEPTH - 1))
    return x
```

```python
import functools

import numpy as np
import jax
import jax.numpy as jnp
from jax import lax
from jax.experimental import pallas as pl
from jax.experimental.pallas import tpu as pltpu

f32 = jnp.float32
bf16 = jnp.bfloat16

D = 1024
BATCH = 8
SEQ = 4096
DEPTH = 4
GROUP = 128
HEADS = 4
DK = D // HEADS
ROPE_BASE = 10000.0
D_FF = 2816
N_MOD = 6
EPS = 1e-6
D_REST = 6 * D

R = 64
TM = 512
NT = SEQ // TM
RET_CHUNK = 256
RET_ROWS = 1024
FFT_SLABS = 16
FF_TILE = 256
HALO = 8
VMEM_LIMIT = 56 * 1024 * 1024

_NT_DIMS = (((1,), (1,)), ((), ()))
_TN_DIMS = (((0,), (0,)), ((), ()))


def _resident(shape):
    nd = len(shape)
    return pl.BlockSpec(shape, lambda *_: (0,) * nd, pipeline_mode=pl.Buffered(1))


def _layer_resident(shape, layer):
    nd = len(shape)
    return pl.BlockSpec((None,) + tuple(shape), lambda *_: (layer,) + (0,) * nd,
                        pipeline_mode=pl.Buffered(1))


def _params(*sem):
    return pltpu.CompilerParams(dimension_semantics=sem, vmem_limit_bytes=VMEM_LIMIT)


def _group_dft_tables():
    n = np.arange(GROUP)
    ang = 2.0 * np.pi * np.outer(n, n) / GROUP
    s = 1.0 / np.sqrt(GROUP)
    return (np.cos(ang) * s).astype(np.float32), (-np.sin(ang) * s).astype(np.float32)


def _stage_tables():
    k = np.arange(R)
    ang = 2.0 * np.pi * np.outer(k, k) / R
    cr = np.cos(ang) / 8.0
    ci = -np.sin(ang) / 8.0
    stage1 = np.block([[cr, -ci], [ci, cr]])
    stage2 = np.concatenate([cr, -ci], axis=1)
    n2 = np.arange(R)[:, None, None]
    k1 = np.arange(R)[None, :, None]
    tw = 2.0 * np.pi * (n2 * k1) / SEQ * np.ones((1, 1, 128))
    return (stage1.astype(np.float32), stage2.astype(np.float32),
            np.cos(tw).astype(np.float32), (-np.sin(tw)).astype(np.float32))


def _retention_tables():
    c = RET_CHUNK
    j = np.arange(c, dtype=np.float64)
    diff = j[:, None] - j[None, :]
    ones = np.ones((1, DK))
    mask = np.zeros((HEADS, c, c))
    qdf, qdb, kdf, kdb = (np.zeros((HEADS, c, DK)) for _ in range(4))
    cdf, cdb = np.zeros(HEADS), np.zeros(HEADS)
    for h in range(HEADS):
        lf = np.log1p(-np.exp2(-5.0 - h))
        lb = np.log1p(-np.exp2(-5.5 - h))
        mask[h] = np.where(diff >= 0, np.exp(lf * np.maximum(diff, 0.0)),
                           np.exp(lb * np.maximum(-diff, 0.0)))
        qdf[h] = np.exp(lf * (j + 1.0))[:, None] * ones
        kdf[h] = np.exp(lf * (c - 1.0 - j))[:, None] * ones
        qdb[h] = np.exp(lb * (c - j))[:, None] * ones
        kdb[h] = np.exp(lb * j)[:, None] * ones
        cdf[h] = np.exp(lf * c)
        cdb[h] = np.exp(lb * c)
    dec = np.stack([qdf, qdb, kdf, kdb])
    return mask.astype(np.float32), dec.astype(np.float32), cdf, cdb


_CG, _SG_NEG = _group_dft_tables()
_STAGE1, _STAGE2, _TW_RE, _TW_IM = _stage_tables()
_RMASK, _RDEC, _CDF, _CDB = _retention_tables()
_QDF, _QDB, _KDF, _KDB = range(4)


def _rotary_tables():
    half = DK // 2
    inv_freq = ROPE_BASE ** (-jnp.arange(half, dtype=f32) / half)
    ang = jnp.arange(SEQ, dtype=f32)[:, None] * inv_freq[None, :]
    return jnp.cos(ang), jnp.sin(ang)


def _ada_kernel(c_ref, w_ref, b_ref, o_ref):
    c = c_ref[...]
    act = (c * jax.nn.sigmoid(c)).astype(bf16)
    o_ref[...] = jnp.dot(act, w_ref[...].astype(bf16), preferred_element_type=f32) + b_ref[...]


def _ada_call(c, ada_w, ada_b):
    return pl.pallas_call(
        _ada_kernel,
        out_shape=jax.ShapeDtypeStruct((DEPTH, N_MOD, BATCH, D), f32),
        grid=(DEPTH, N_MOD),
        in_specs=[
            pl.BlockSpec((BATCH, D), lambda l, j: (0, 0)),
            pl.BlockSpec((None, D, D), lambda l, j: (l, 0, j)),
            pl.BlockSpec((None, None, 1, D), lambda l, j: (l, j, 0, 0)),
        ],
        out_specs=pl.BlockSpec((None, None, BATCH, D), lambda l, j: (l, j, 0, 0)),
        compiler_params=_params("arbitrary", "arbitrary"),
        name="ada_mod",
    )(c, ada_w, ada_b.reshape(DEPTH, N_MOD, 1, D))


def _fold_kernel(w_ref, cg_ref, sg_ref, wr_ref, wi_ref):
    w = w_ref[...]
    wr_ref[...] = jnp.dot(w, cg_ref[...], preferred_element_type=f32,
                          precision=lax.Precision.HIGHEST).astype(bf16)
    wi_ref[...] = jnp.dot(w, sg_ref[...], preferred_element_type=f32,
                          precision=lax.Precision.HIGHEST).astype(bf16)


def _fold_call(w_in):
    n_groups = D // GROUP
    out = jax.ShapeDtypeStruct((DEPTH, D, D), bf16)
    blk = pl.BlockSpec((None, D, GROUP), lambda l, g: (l, 0, g))
    tab = pl.BlockSpec((GROUP, GROUP), lambda l, g: (0, 0))
    return pl.pallas_call(
        _fold_kernel,
        out_shape=(out, out),
        grid=(DEPTH, n_groups),
        in_specs=[blk, tab, tab],
        out_specs=(blk, blk),
        compiler_params=_params("arbitrary", "arbitrary"),
        name="fold_group_dft",
    )(w_in, jnp.asarray(_CG), jnp.asarray(_SG_NEG))


def _modulated_norm(x, gain, shift, scale):
    ms = jnp.mean(x * x, axis=-1, keepdims=True)
    return (x * lax.rsqrt(ms + EPS) * gain) * (1.0 + scale) + shift


def _inproj_kernel(x_ref, mod_ref, g_ref, cos_ref, sin_ref, w_ref,
                   q_ref, k_ref, v_ref, gs_ref, af_ref, ar_ref):
    hb = _modulated_norm(x_ref[...], g_ref[...], mod_ref[0], mod_ref[1]).astype(bf16)
    cos = cos_ref[...]
    sin = sin_ref[...]
    half = DK // 2

    def rotary(sec, o_ref, scale):
        p = jnp.dot(hb, w_ref[:, sec * D:(sec + 1) * D], preferred_element_type=f32)
        for h in range(HEADS):
            t1 = p[:, h * DK:h * DK + half]
            t2 = p[:, h * DK + half:(h + 1) * DK]
            o_ref[:, h * DK:h * DK + half] = ((t1 * cos - t2 * sin) * scale).astype(bf16)
            o_ref[:, h * DK + half:(h + 1) * DK] = ((t1 * sin + t2 * cos) * scale).astype(bf16)

    rotary(0, q_ref, DK ** -0.5)
    rotary(1, k_ref, 1.0)
    v_ref[...] = jnp.dot(hb, w_ref[:, 2 * D:3 * D], preferred_element_type=f32).astype(bf16)
    g = jnp.dot(hb, w_ref[:, 3 * D:4 * D], preferred_element_type=f32)
    gs_ref[...] = (g * jax.nn.sigmoid(g)).astype(bf16)
    af = jnp.dot(hb, w_ref[:, 4 * D:5 * D], preferred_element_type=f32)
    af_ref[...] = jax.nn.sigmoid(af).astype(bf16)
    ar = jnp.dot(hb, w_ref[:, 5 * D:6 * D], preferred_element_type=f32)
    ar_ref[...] = jax.nn.sigmoid(ar).astype(bf16)


def _inproj_call(layer, x, mod, gain, cos, sin, w_rest):
    row = pl.BlockSpec((None, TM, D), lambda b, i: (b, i, 0))
    rot = pl.BlockSpec((TM, DK // 2), lambda b, i: (i, 0))
    act = jax.ShapeDtypeStruct((BATCH, SEQ, D), bf16)
    return pl.pallas_call(
        _inproj_kernel,
        out_shape=(act, act, act, act, act, act),
        grid=(BATCH, NT),
        in_specs=[
            row,
            pl.BlockSpec((N_MOD, None, 1, D), lambda b, i: (0, b, 0, 0)),
            pl.BlockSpec((1, D), lambda b, i: (0, 0)),
            rot, rot,
            _layer_resident((D, D_REST), layer),
        ],
        out_specs=(row, row, row, row, row, row),
        compiler_params=_params("arbitrary", "arbitrary"),
        name="in_projection",
    )(x, mod, gain, cos, sin, w_rest)


def _fourier_kernel(x_ref, mod_ref, g_ref, wr_ref, wi_ref, m_ref, twr_ref, twi_ref,
                    tr_ref, ti_ref, sr_ref, si_ref):
    x = pltpu.einshape("abc->bac", x_ref[...]).reshape(FFT_SLABS * R, D)
    hb = _modulated_norm(x, g_ref[...], mod_ref[0], mod_ref[1]).astype(bf16)
    zr = jnp.dot(hb, wr_ref[...], preferred_element_type=f32).astype(bf16)
    zi = jnp.dot(hb, wi_ref[...], preferred_element_type=f32).astype(bf16)
    m = m_ref[...]
    for s in range(FFT_SLABS):
        z = jnp.concatenate([zr[s * R:(s + 1) * R], zi[s * R:(s + 1) * R]], axis=0)
        t = jnp.dot(m, z, preferred_element_type=f32)
        t_re, t_im = t[:R], t[R:]
        w_re = jnp.tile(twr_ref[s], (1, D // 128))
        w_im = jnp.tile(twi_ref[s], (1, D // 128))
        sr_ref[s] = t_re * w_re - t_im * w_im
        si_ref[s] = t_re * w_im + t_im * w_re
    tr_ref[...] = pltpu.einshape("abc->bac", sr_ref[...]).astype(bf16)
    ti_ref[...] = pltpu.einshape("abc->bac", si_ref[...]).astype(bf16)


def _fourier_call(layer, x, mod, gain, wr, wi):
    nsteps = R // FFT_SLABS
    slab = pl.BlockSpec((None, R, FFT_SLABS, D), lambda b, j: (b, 0, j, 0))
    tw = pl.BlockSpec((FFT_SLABS, R, 128), lambda b, j: (j, 0, 0))
    t2 = jax.ShapeDtypeStruct((BATCH, R, R, D), bf16)
    return pl.pallas_call(
        _fourier_kernel,
        out_shape=(t2, t2),
        grid=(BATCH, nsteps),
        in_specs=[slab,
                  pl.BlockSpec((N_MOD, None, 1, D), lambda b, j: (0, b, 0, 0)),
                  pl.BlockSpec((1, D), lambda b, j: (0, 0)),
                  _layer_resident((D, D), layer), _layer_resident((D, D), layer),
                  _resident((2 * R, 2 * R)), tw, tw],
        out_specs=(slab, slab),
        scratch_shapes=[pltpu.VMEM((FFT_SLABS, R, D), f32), pltpu.VMEM((FFT_SLABS, R, D), f32)],
        compiler_params=_params("arbitrary", "arbitrary"),
        name="fourier_projection_dft1",
    )(x.reshape(BATCH, R, R, D), mod, gain, wr, wi,
      jnp.asarray(_STAGE1, dtype=bf16), jnp.asarray(_TW_RE), jnp.asarray(_TW_IM))


def _dft2_kernel(tr_ref, ti_ref, m_ref, y_ref, s_ref):
    m = m_ref[...]
    for s in range(FFT_SLABS):
        t = jnp.concatenate([tr_ref[s], ti_ref[s]], axis=0)
        s_ref[s] = jnp.dot(m, t, preferred_element_type=f32)
    y_ref[...] = pltpu.einshape("abc->bac", s_ref[...]).astype(bf16)


def _dft2_call(tr, ti):
    nsteps = R // FFT_SLABS
    tin = pl.BlockSpec((None, FFT_SLABS, R, D), lambda b, j: (b, j, 0, 0))
    yout = pl.BlockSpec((None, R, FFT_SLABS, D), lambda b, j: (b, 0, j, 0))
    return pl.pallas_call(
        _dft2_kernel,
        out_shape=jax.ShapeDtypeStruct((BATCH, R, R, D), bf16),
        grid=(BATCH, nsteps),
        in_specs=[tin, tin, _resident((R, 2 * R))],
        out_specs=yout,
        scratch_shapes=[pltpu.VMEM((FFT_SLABS, R, D), f32)],
        compiler_params=_params("arbitrary", "arbitrary"),
        name="dft_stage2",
    )(tr, ti, jnp.asarray(_STAGE2, dtype=bf16))


def _ret_kernel(q_ref, k_ref, v_ref, gs_ref, m_ref, dec_ref,
                o_ref, sf_ref, sb_ref, sb_all_ref, decb_ref):
    p = pl.program_id(1)
    j = pl.program_id(2)
    n_chunks = RET_ROWS // RET_CHUNK

    @pl.when(j == 0)
    def _():
        sf_ref[...] = jnp.zeros_like(sf_ref)
        sb_ref[...] = jnp.zeros_like(sb_ref)
        decb_ref[...] = dec_ref[...].astype(bf16)

    @pl.when(p == 0)
    def _():
        block = pl.num_programs(2) - 1 - j
        for c in reversed(range(n_chunks)):
            rows = slice(c * RET_CHUNK, (c + 1) * RET_CHUNK)
            for h in range(HEADS):
                cols = slice(h * DK, (h + 1) * DK)
                state = sb_ref[h]
                sb_all_ref[block * n_chunks + c, h] = state.astype(bf16)
                k_dec = k_ref[rows, cols] * decb_ref[_KDB, h]
                sb_ref[h] = state * float(_CDB[h]) + lax.dot_general(
                    k_dec, v_ref[rows, cols], _TN_DIMS, preferred_element_type=f32)

    @pl.when(p == 1)
    def _():
        for c in range(n_chunks):
            rows = slice(c * RET_CHUNK, (c + 1) * RET_CHUNK)
            for h in range(HEADS):
                cols = slice(h * DK, (h + 1) * DK)
                q = q_ref[rows, cols]
                k = k_ref[rows, cols]
                v = v_ref[rows, cols]
                scores = lax.dot_general(q, k, _NT_DIMS, preferred_element_type=f32) * m_ref[h]
                state = sf_ref[h]
                lhs = jnp.concatenate([scores.astype(bf16), q * decb_ref[_QDF, h],
                                       q * decb_ref[_QDB, h]], axis=1)
                rhs = jnp.concatenate([v, state.astype(bf16), sb_all_ref[j * n_chunks + c, h]],
                                      axis=0)
                y = jnp.dot(lhs, rhs, preferred_element_type=f32)
                k_dec = k * decb_ref[_KDF, h]
                sf_ref[h] = state * float(_CDF[h]) + lax.dot_general(
                    k_dec, v, _TN_DIMS, preferred_element_type=f32)
                mu = jnp.mean(y, axis=-1, keepdims=True)
                yc = y - mu
                var = jnp.mean(yc * yc, axis=-1, keepdims=True)
                yn = yc * lax.rsqrt(var + EPS)
                o_ref[rows, cols] = (yn * gs_ref[rows, cols].astype(f32)).astype(bf16)


def _ret_call(q, k, v, gs):
    nb = SEQ // RET_ROWS

    def scan_idx(b, p, j):
        return (b, j + (1 - p) * (nb - 1 - 2 * j), 0)

    def finish_idx(b, p, j):
        return (b, p * j, 0)

    scan = pl.BlockSpec((None, RET_ROWS, D), scan_idx)
    fin = pl.BlockSpec((None, RET_ROWS, D), finish_idx)
    return pl.pallas_call(
        _ret_kernel,
        out_shape=jax.ShapeDtypeStruct((BATCH, SEQ, D), bf16),
        grid=(BATCH, 2, nb),
        in_specs=[fin, scan, scan, fin, _resident((HEADS, RET_CHUNK, RET_CHUNK)),
                  _resident((4, HEADS, RET_CHUNK, DK))],
        out_specs=fin,
        scratch_shapes=[pltpu.VMEM((HEADS, DK, DK), f32), pltpu.VMEM((HEADS, DK, DK), f32),
                        pltpu.VMEM((SEQ // RET_CHUNK, HEADS, DK, DK), bf16),
                        pltpu.VMEM((4, HEADS, RET_CHUNK, DK), bf16)],
        compiler_params=_params("arbitrary", "arbitrary", "arbitrary"),
        name="retention",
    )(q, k, v, gs, jnp.asarray(_RMASK), jnp.asarray(_RDEC))


def _merge_kernel(x_ref, mod_ref, yf_ref, yr_ref, af_ref, ar_ref, wf_ref, wr_ref, wo_ref, o_ref):
    a = jnp.dot(yf_ref[...], wf_ref[...], preferred_element_type=f32)
    b = jnp.dot(yr_ref[...], wr_ref[...], preferred_element_type=f32)
    merged = af_ref[...].astype(f32) * a + ar_ref[...].astype(f32) * b
    o = jnp.dot(merged.astype(bf16), wo_ref[...], preferred_element_type=f32)
    o_ref[...] = x_ref[...] + mod_ref[2] * o


def _merge_call(layer, x, mod, yf, yr, af, ar, wf, wr, wo):
    row = pl.BlockSpec((None, TM, D), lambda b, i: (b, i, 0))
    return pl.pallas_call(
        _merge_kernel,
        out_shape=jax.ShapeDtypeStruct((BATCH, SEQ, D), f32),
        grid=(BATCH, NT),
        in_specs=[row, pl.BlockSpec((N_MOD, None, 1, D), lambda b, i: (0, b, 0, 0)),
                  row, row, row, row,
                  _layer_resident((D, D), layer), _layer_resident((D, D), layer),
                  _layer_resident((D, D), layer)],
        out_specs=row,
        compiler_params=_params("arbitrary", "arbitrary"),
        name="merge_out_projection",
    )(x, mod, yf, yr, af, ar, wf, wr, wo)


def _ffn_kernel(x_ref, xp_ref, xn_ref, mod_ref, g_ref, wup_ref, cw_ref, cb_ref, wdn_ref, fg_ref,
                o_ref, act_ref, *, final):
    i = pl.program_id(1)
    rows = TM + 2 * HALO
    gain, shift, scale = g_ref[...], mod_ref[3], mod_ref[4]
    keep_prev = (i > 0).astype(f32)
    keep_next = (i < pl.num_programs(1) - 1).astype(f32)
    h = jnp.concatenate([
        _modulated_norm(xp_ref[...], gain, shift, scale) * keep_prev,
        _modulated_norm(x_ref[...], gain, shift, scale),
        _modulated_norm(xn_ref[...], gain, shift, scale) * keep_next,
    ], axis=0).astype(bf16)

    def conv(col):
        u = jnp.dot(h, wup_ref[:, col:col + FF_TILE], preferred_element_type=f32)
        w = cw_ref[:, col:col + FF_TILE]
        full = (pltpu.roll(u, 1, 0) * w[0:1] + u * w[1:2] + pltpu.roll(u, rows - 1, 0) * w[2:3]
                + cb_ref[:, col:col + FF_TILE])
        return full[HALO:HALO + TM]

    for t in range(D_FF // FF_TILE):
        a = conv(t * FF_TILE)
        b = conv(D_FF + t * FF_TILE)
        gelu = 0.5 * a * (1.0 + lax.erf(a * (2.0 ** -0.5)))
        act_ref[:, t * FF_TILE:(t + 1) * FF_TILE] = (gelu * b).astype(bf16)

    y = jnp.dot(act_ref[...], wdn_ref[...], preferred_element_type=f32)
    out = x_ref[...] + mod_ref[5] * y
    if final:
        ms = jnp.mean(out * out, axis=-1, keepdims=True)
        out = out * lax.rsqrt(ms + EPS) * fg_ref[...]
    o_ref[...] = out


def _ffn_call(layer, x, mod, gain, wup, cw, cb, wdn, final_g, final):
    tiles = TM // HALO
    row = pl.BlockSpec((None, TM, D), lambda b, i: (b, i, 0))
    prev = pl.BlockSpec((None, HALO, D), lambda b, i: (b, jnp.maximum(i * tiles - 1, 0), 0))
    nxt = pl.BlockSpec((None, HALO, D),
                       lambda b, i: (b, jnp.minimum((i + 1) * tiles, SEQ // HALO - 1), 0))
    return pl.pallas_call(
        functools.partial(_ffn_kernel, final=final),
        out_shape=jax.ShapeDtypeStruct((BATCH, SEQ, D), f32),
        grid=(BATCH, NT),
        in_specs=[row, prev, nxt,
                  pl.BlockSpec((N_MOD, None, 1, D), lambda b, i: (0, b, 0, 0)),
                  pl.BlockSpec((1, D), lambda b, i: (0, 0)),
                  _layer_resident((D, 2 * D_FF), layer), _layer_resident((3, 2 * D_FF), layer),
                  _layer_resident((1, 2 * D_FF), layer), _layer_resident((D_FF, D), layer),
                  pl.BlockSpec((1, D), lambda b, i: (0, 0))],
        out_specs=row,
        scratch_shapes=[pltpu.VMEM((TM, D_FF), bf16)],
        compiler_params=_params("arbitrary", "arbitrary"),
        name="conv_ffn_final" if final else "conv_ffn",
    )(x, x, x, mod, gain, wup, cw, cb, wdn, final_g)


def kernel(x, c, norm1_g, norm2_g, ada_w, ada_b, w_in, w_fourier, w_ret, w_out,
           ffn_up, conv_w, conv_b, ffn_down, final_g):
    assert x.shape == (BATCH, SEQ, D) and c.shape == (BATCH, D)
    mod_all = _ada_call(c, ada_w, ada_b).reshape(DEPTH, N_MOD, BATCH, 1, D)
    wfr_all, wfi_all = _fold_call(w_in)
    cos, sin = _rotary_tables()
    final_gain = final_g.reshape(1, D)
    w_rest = w_in[:, :, D:].astype(bf16)
    wf, wr, wo = w_fourier.astype(bf16), w_ret.astype(bf16), w_out.astype(bf16)
    wup, wdn = ffn_up.astype(bf16), ffn_down.astype(bf16)

    for l in range(DEPTH):
        mod = mod_all[l]
        gain1 = norm1_g[l].reshape(1, D)
        q, k, v, gs, af, ar = _inproj_call(l, x, mod, gain1, cos, sin, w_rest)
        tr, ti = _fourier_call(l, x, mod, gain1, wfr_all, wfi_all)
        yf = _dft2_call(tr, ti).reshape(BATCH, SEQ, D)
        yr = _ret_call(q, k, v, gs)
        x = _merge_call(l, x, mod, yf, yr, af, ar, wf, wr, wo)
        x = _ffn_call(l, x, mod, norm2_g[l].reshape(1, D), wup, conv_w,
                      conv_b.reshape(DEPTH, 1, 2 * D_FF), wdn, final_gain,
                      final=(l == DEPTH - 1))
    return x
```

```python
import functools

import numpy as np
import jax
import jax.numpy as jnp
from jax import lax
from jax.experimental import pallas as pl
from jax.experimental.pallas import tpu as pltpu

f32 = jnp.float32
bf16 = jnp.bfloat16

D = 1024
BATCH = 8
SEQ = 4096
DEPTH = 4
GROUP = 128
HEADS = 4
DK = D // HEADS
ROPE_BASE = 10000.0
D_FF = 2816
N_MOD = 6
EPS = 1e-6
D_REST = 6 * D

R = 64
TM = 512
NT = SEQ // TM
RET_CHUNK = 256
RET_ROWS = 1024
FFT_SLABS = 16
FF_TILE = 256
HALO = 8
VMEM_LIMIT = 56 * 1024 * 1024

_NT_DIMS = (((1,), (1,)), ((), ()))
_TN_DIMS = (((0,), (0,)), ((), ()))


def _resident(shape):
    nd = len(shape)
    return pl.BlockSpec(shape, lambda *_: (0,) * nd, pipeline_mode=pl.Buffered(1))


def _layer_resident(shape, layer):
    nd = len(shape)
    return pl.BlockSpec((None,) + tuple(shape), lambda *_: (layer,) + (0,) * nd,
                        pipeline_mode=pl.Buffered(1))


def _params(*sem):
    return pltpu.CompilerParams(dimension_semantics=sem, vmem_limit_bytes=VMEM_LIMIT)


def _group_dft_tables():
    n = np.arange(GROUP)
    ang = 2.0 * np.pi * np.outer(n, n) / GROUP
    s = 1.0 / np.sqrt(GROUP)
    return (np.cos(ang) * s).astype(np.float32), (-np.sin(ang) * s).astype(np.float32)


def _stage_tables():
    k = np.arange(R)
    ang = 2.0 * np.pi * np.outer(k, k) / R
    cr = np.cos(ang) / 8.0
    ci = -np.sin(ang) / 8.0
    stage2 = np.concatenate([cr, -ci], axis=1)
    n2 = np.arange(R)[:, None, None]
    k1 = np.arange(R)[None, :, None]
    n1 = np.arange(R)[None, None, :]
    a = np.exp(-2j * np.pi * (n2 * k1 / SEQ + n1 * k1 / R)) / 8.0
    stage1 = np.concatenate([np.concatenate([a.real, -a.imag], axis=2),
                             np.concatenate([a.imag, a.real], axis=2)], axis=1)
    return stage1.astype(np.float32), stage2.astype(np.float32)


def _retention_tables():
    c = RET_CHUNK
    j = np.arange(c, dtype=np.float64)
    diff = j[:, None] - j[None, :]
    ones = np.ones((1, DK))
    mask = np.zeros((HEADS, c, c))
    qdf, qdb, kdf, kdb = (np.zeros((HEADS, c, DK)) for _ in range(4))
    cdf, cdb = np.zeros(HEADS), np.zeros(HEADS)
    for h in range(HEADS):
        lf = np.log1p(-np.exp2(-5.0 - h))
        lb = np.log1p(-np.exp2(-5.5 - h))
        mask[h] = np.where(diff >= 0, np.exp(lf * np.maximum(diff, 0.0)),
                           np.exp(lb * np.maximum(-diff, 0.0)))
        qdf[h] = np.exp(lf * (j + 1.0))[:, None] * ones
        kdf[h] = np.exp(lf * (c - 1.0 - j))[:, None] * ones
        qdb[h] = np.exp(lb * (c - j))[:, None] * ones
        kdb[h] = np.exp(lb * j)[:, None] * ones
        cdf[h] = np.exp(lf * c)
        cdb[h] = np.exp(lb * c)
    dec = np.stack([qdf, qdb, kdf, kdb])
    return mask.astype(np.float32), dec.astype(np.float32), cdf, cdb


_GROUP_TABLE = np.concatenate(_group_dft_tables(), axis=1)
_GROUP_HI = _GROUP_TABLE.astype(bf16)
_GROUP_LO = (_GROUP_TABLE - _GROUP_HI.astype(np.float32)).astype(bf16)
_STAGE1, _STAGE2 = _stage_tables()
_RMASK, _RDEC, _CDF, _CDB = _retention_tables()
_QDF, _QDB, _KDF, _KDB = range(4)


def _rotary_tables():
    half = DK // 2
    inv_freq = ROPE_BASE ** (-jnp.arange(half, dtype=f32) / half)
    ang = jnp.arange(SEQ, dtype=f32)[:, None] * inv_freq[None, :]
    return jnp.cos(ang), jnp.sin(ang)


def _ada_kernel(c_ref, w_ref, b_ref, o_ref):
    c = c_ref[...]
    act = (c * jax.nn.sigmoid(c)).astype(bf16)
    o_ref[...] = jnp.dot(act, w_ref[...].astype(bf16), preferred_element_type=f32) + b_ref[...]


def _ada_call(c, ada_w, ada_b):
    return pl.pallas_call(
        _ada_kernel,
        out_shape=jax.ShapeDtypeStruct((DEPTH, N_MOD, BATCH, D), f32),
        grid=(DEPTH, N_MOD),
        in_specs=[
            pl.BlockSpec((BATCH, D), lambda l, j: (0, 0)),
            pl.BlockSpec((None, D, D), lambda l, j: (l, 0, j)),
            pl.BlockSpec((None, None, 1, D), lambda l, j: (l, j, 0, 0)),
        ],
        out_specs=pl.BlockSpec((None, None, BATCH, D), lambda l, j: (l, j, 0, 0)),
        compiler_params=_params("arbitrary", "arbitrary"),
        name="ada_mod",
    )(c, ada_w, ada_b.reshape(DEPTH, N_MOD, 1, D))


def _prep_kernel(w_ref, t_hi_ref, t_lo_ref, wr_ref, wi_ref, rest_ref):
    j = pl.program_id(1)

    @pl.when(j == 0)
    def _():
        t_hi = t_hi_ref[...]
        t_lo = t_lo_ref[...]
        for g in range(D // GROUP):
            cols = slice(g * GROUP, (g + 1) * GROUP)
            w = w_ref[:, cols]
            hi = w.astype(bf16)
            lo = (w - hi.astype(f32)).astype(bf16)
            folded = (jnp.dot(hi, t_hi, preferred_element_type=f32)
                      + jnp.dot(lo, t_hi, preferred_element_type=f32)
                      + jnp.dot(hi, t_lo, preferred_element_type=f32))
            wr_ref[:, cols] = folded[:, :GROUP].astype(bf16)
            wi_ref[:, cols] = folded[:, GROUP:].astype(bf16)

    @pl.when(j > 0)
    def _():
        rest_ref[...] = w_ref[...].astype(bf16)


def _prep_call(w_in):
    sq = jax.ShapeDtypeStruct((DEPTH, D, D), bf16)
    tab = pl.BlockSpec((GROUP, 2 * GROUP), lambda l, j: (0, 0))
    fold_out = pl.BlockSpec((None, D, D), lambda l, j: (l, 0, 0))
    return pl.pallas_call(
        _prep_kernel,
        out_shape=(sq, sq, jax.ShapeDtypeStruct((DEPTH, D, D_REST), bf16)),
        grid=(DEPTH, 1 + D_REST // D),
        in_specs=[pl.BlockSpec((None, D, D), lambda l, j: (l, 0, j)), tab, tab],
        out_specs=(fold_out, fold_out,
                   pl.BlockSpec((None, D, D), lambda l, j: (l, 0, jnp.maximum(j - 1, 0)))),
        compiler_params=_params("arbitrary", "arbitrary"),
        name="weight_prep",
    )(w_in, jnp.asarray(_GROUP_HI), jnp.asarray(_GROUP_LO))


def _modulated_norm(x, gain, shift, scale):
    ms = jnp.mean(x * x, axis=-1, keepdims=True)
    return (x * lax.rsqrt(ms + EPS) * gain) * (1.0 + scale) + shift


def _inproj_kernel(x_ref, mod_ref, g_ref, cos_ref, sin_ref, w_ref,
                   q_ref, k_ref, v_ref, gs_ref, af_ref, ar_ref):
    hb = _modulated_norm(x_ref[...], g_ref[...], mod_ref[0], mod_ref[1]).astype(bf16)
    cos = cos_ref[...]
    sin = sin_ref[...]
    half = DK // 2

    def rotary(sec, o_ref, scale):
        p = jnp.dot(hb, w_ref[:, sec * D:(sec + 1) * D], preferred_element_type=f32)
        for h in range(HEADS):
            t1 = p[:, h * DK:h * DK + half]
            t2 = p[:, h * DK + half:(h + 1) * DK]
            o_ref[:, h * DK:h * DK + half] = ((t1 * cos - t2 * sin) * scale).astype(bf16)
            o_ref[:, h * DK + half:(h + 1) * DK] = ((t1 * sin + t2 * cos) * scale).astype(bf16)

    rotary(0, q_ref, DK ** -0.5)
    rotary(1, k_ref, 1.0)
    g = jnp.dot(hb, w_ref[:, 3 * D:4 * D], preferred_element_type=f32)
    gs_ref[...] = (g * jax.nn.sigmoid(g)).astype(bf16)
    af = jnp.dot(hb, w_ref[:, 4 * D:5 * D], preferred_element_type=f32)
    af_ref[...] = jax.nn.sigmoid(af).astype(bf16)
    ar = jnp.dot(hb, w_ref[:, 5 * D:6 * D], preferred_element_type=f32)
    ar_ref[...] = jax.nn.sigmoid(ar).astype(bf16)
    v_ref[...] = jnp.dot(hb, w_ref[:, 2 * D:3 * D], preferred_element_type=f32).astype(bf16)


def _inproj_call(layer, x, mod, gain, cos, sin, w_rest):
    row = pl.BlockSpec((None, TM, D), lambda b, i: (b, i, 0))
    rot = pl.BlockSpec((TM, DK // 2), lambda b, i: (i, 0))
    act = jax.ShapeDtypeStruct((BATCH, SEQ, D), bf16)
    return pl.pallas_call(
        _inproj_kernel,
        out_shape=(act, act, act, act, act, act),
        grid=(BATCH, NT),
        in_specs=[
            row,
            pl.BlockSpec((N_MOD, None, 1, D), lambda b, i: (0, b, 0, 0)),
            pl.BlockSpec((1, D), lambda b, i: (0, 0)),
            rot, rot,
            _layer_resident((D, D_REST), layer),
        ],
        out_specs=(row, row, row, row, row, row),
        compiler_params=_params("arbitrary", "arbitrary"),
        name="in_projection",
    )(x, mod, gain, cos, sin, w_rest)


def _fourier_kernel(x_ref, mod_ref, g_ref, wr_ref, wi_ref, m_ref,
                    tr_ref, ti_ref, sr_ref, si_ref):
    x = pltpu.einshape("abc->bac", x_ref[...]).reshape(FFT_SLABS * R, D)
    hb = _modulated_norm(x, g_ref[...], mod_ref[0], mod_ref[1]).astype(bf16)
    zr = jnp.dot(hb, wr_ref[...], preferred_element_type=f32).astype(bf16)
    zi = jnp.dot(hb, wi_ref[...], preferred_element_type=f32).astype(bf16)
    for s in range(FFT_SLABS):
        z = jnp.concatenate([zr[s * R:(s + 1) * R], zi[s * R:(s + 1) * R]], axis=0)
        t = jnp.dot(m_ref[s], z, preferred_element_type=f32)
        sr_ref[s] = t[:R]
        si_ref[s] = t[R:]
    tr_ref[...] = pltpu.einshape("abc->bac", sr_ref[...]).astype(bf16)
    ti_ref[...] = pltpu.einshape("abc->bac", si_ref[...]).astype(bf16)


def _fourier_call(layer, x, mod, gain, wr, wi):
    nsteps = R // FFT_SLABS
    slab = pl.BlockSpec((None, R, FFT_SLABS, D), lambda b, j: (b, 0, j, 0))
    t2 = jax.ShapeDtypeStruct((BATCH, R, R, D), bf16)
    return pl.pallas_call(
        _fourier_kernel,
        out_shape=(t2, t2),
        grid=(BATCH, nsteps),
        in_specs=[slab,
                  pl.BlockSpec((N_MOD, None, 1, D), lambda b, j: (0, b, 0, 0)),
                  pl.BlockSpec((1, D), lambda b, j: (0, 0)),
                  _layer_resident((D, D), layer), _layer_resident((D, D), layer),
                  pl.BlockSpec((FFT_SLABS, 2 * R, 2 * R), lambda b, j: (j, 0, 0))],
        out_specs=(slab, slab),
        scratch_shapes=[pltpu.VMEM((FFT_SLABS, R, D), f32), pltpu.VMEM((FFT_SLABS, R, D), f32)],
        compiler_params=_params("arbitrary", "arbitrary"),
        name="fourier_projection_dft1",
    )(x.reshape(BATCH, R, R, D), mod, gain, wr, wi,
      jnp.asarray(_STAGE1, dtype=bf16))


def _dft2_kernel(tr_ref, ti_ref, m_ref, y_ref, s_ref):
    m = m_ref[...]
    for s in range(FFT_SLABS):
        t = jnp.concatenate([tr_ref[s], ti_ref[s]], axis=0)
        s_ref[s] = jnp.dot(m, t, preferred_element_type=f32)
    y_ref[...] = pltpu.einshape("abc->bac", s_ref[...]).astype(bf16)


def _dft2_call(tr, ti):
    nsteps = R // FFT_SLABS
    tin = pl.BlockSpec((None, FFT_SLABS, R, D), lambda b, j: (b, j, 0, 0))
    yout = pl.BlockSpec((None, R, FFT_SLABS, D), lambda b, j: (b, 0, j, 0))
    return pl.pallas_call(
        _dft2_kernel,
        out_shape=jax.ShapeDtypeStruct((BATCH, R, R, D), bf16),
        grid=(BATCH, nsteps),
        in_specs=[tin, tin, _resident((R, 2 * R))],
        out_specs=yout,
        scratch_shapes=[pltpu.VMEM((FFT_SLABS, R, D), f32)],
        compiler_params=_params("arbitrary", "arbitrary"),
        name="dft_stage2",
    )(tr, ti, jnp.asarray(_STAGE2, dtype=bf16))


def _ret_kernel(q_ref, k_ref, v_ref, gs_ref, m_ref, dec_ref,
                o_ref, sf_ref, sb_ref, sb_all_ref, decb_ref):
    p = pl.program_id(1)
    j = pl.program_id(2)
    n_chunks = RET_ROWS // RET_CHUNK

    @pl.when(j == 0)
    def _():
        sf_ref[...] = jnp.zeros_like(sf_ref)
        sb_ref[...] = jnp.zeros_like(sb_ref)
        decb_ref[...] = dec_ref[...].astype(bf16)

    @pl.when(p == 0)
    def _():
        block = pl.num_programs(2) - 1 - j
        for c in reversed(range(n_chunks)):
            rows = slice(c * RET_CHUNK, (c + 1) * RET_CHUNK)
            for h in range(HEADS):
                cols = slice(h * DK, (h + 1) * DK)
                state = sb_ref[h]
                sb_all_ref[block * n_chunks + c, h] = state.astype(bf16)
                k_dec = k_ref[rows, cols] * decb_ref[_KDB, h]
                sb_ref[h] = state * float(_CDB[h]) + lax.dot_general(
                    k_dec, v_ref[rows, cols], _TN_DIMS, preferred_element_type=f32)

    @pl.when(p == 1)
    def _():
        for c in range(n_chunks):
            rows = slice(c * RET_CHUNK, (c + 1) * RET_CHUNK)
            for h in range(HEADS):
                cols = slice(h * DK, (h + 1) * DK)
                q = q_ref[rows, cols]
                k = k_ref[rows, cols]
                v = v_ref[rows, cols]
                scores = lax.dot_general(q, k, _NT_DIMS, preferred_element_type=f32) * m_ref[h]
                state = sf_ref[h]
                lhs = jnp.concatenate([scores.astype(bf16), q * decb_ref[_QDF, h],
                                       q * decb_ref[_QDB, h]], axis=1)
                rhs = jnp.concatenate([v, state.astype(bf16), sb_all_ref[j * n_chunks + c, h]],
                                      axis=0)
                y = jnp.dot(lhs, rhs, preferred_element_type=f32)
                k_dec = k * decb_ref[_KDF, h]
                sf_ref[h] = state * float(_CDF[h]) + lax.dot_general(
                    k_dec, v, _TN_DIMS, preferred_element_type=f32)
                mu = jnp.mean(y, axis=-1, keepdims=True)
                yc = y - mu
                var = jnp.mean(yc * yc, axis=-1, keepdims=True)
                yn = yc * lax.rsqrt(var + EPS)
                o_ref[rows, cols] = (yn * gs_ref[rows, cols].astype(f32)).astype(bf16)


def _ret_call(q, k, v, gs):
    nb = SEQ // RET_ROWS

    def scan_idx(b, p, j):
        return (b, j + (1 - p) * (nb - 1 - 2 * j), 0)

    def finish_idx(b, p, j):
        return (b, p * j, 0)

    scan = pl.BlockSpec((None, RET_ROWS, D), scan_idx)
    fin = pl.BlockSpec((None, RET_ROWS, D), finish_idx)
    return pl.pallas_call(
        _ret_kernel,
        out_shape=jax.ShapeDtypeStruct((BATCH, SEQ, D), bf16),
        grid=(BATCH, 2, nb),
        in_specs=[fin, scan, scan, fin, _resident((HEADS, RET_CHUNK, RET_CHUNK)),
                  _resident((4, HEADS, RET_CHUNK, DK))],
        out_specs=fin,
        scratch_shapes=[pltpu.VMEM((HEADS, DK, DK), f32), pltpu.VMEM((HEADS, DK, DK), f32),
                        pltpu.VMEM((SEQ // RET_CHUNK, HEADS, DK, DK), bf16),
                        pltpu.VMEM((4, HEADS, RET_CHUNK, DK), bf16)],
        compiler_params=_params("arbitrary", "arbitrary", "arbitrary"),
        name="retention",
    )(q, k, v, gs, jnp.asarray(_RMASK), jnp.asarray(_RDEC))


def _merge_kernel(x_ref, mod_ref, yf_ref, yr_ref, af_ref, ar_ref, wf_ref, wr_ref, wo_ref, o_ref):
    a = jnp.dot(yf_ref[...], wf_ref[...], preferred_element_type=f32)
    b = jnp.dot(yr_ref[...], wr_ref[...], preferred_element_type=f32)
    merged = af_ref[...].astype(f32) * a + ar_ref[...].astype(f32) * b
    o = jnp.dot(merged.astype(bf16), wo_ref[...], preferred_element_type=f32)
    o_ref[...] = x_ref[...] + mod_ref[2] * o


def _merge_call(layer, x, mod, yf, yr, af, ar, wf, wr, wo):
    row = pl.BlockSpec((None, TM, D), lambda b, i: (b, i, 0))
    return pl.pallas_call(
        _merge_kernel,
        out_shape=jax.ShapeDtypeStruct((BATCH, SEQ, D), f32),
        grid=(BATCH, NT),
        in_specs=[row, pl.BlockSpec((N_MOD, None, 1, D), lambda b, i: (0, b, 0, 0)),
                  row, row, row, row,
                  _layer_resident((D, D), layer), _layer_resident((D, D), layer),
                  _layer_resident((D, D), layer)],
        out_specs=row,
        compiler_params=_params("arbitrary", "arbitrary"),
        name="merge_out_projection",
    )(x, mod, yf, yr, af, ar, wf, wr, wo)


def _ffn_kernel(x_ref, xp_ref, xn_ref, mod_ref, g_ref, wup_ref, cw_ref, cb_ref, wdn_ref, fg_ref,
                o_ref, act_ref, *, final):
    i = pl.program_id(1)
    rows = TM + 2 * HALO
    gain, shift, scale = g_ref[...], mod_ref[3], mod_ref[4]
    keep_prev = (i > 0).astype(f32)
    keep_next = (i < pl.num_programs(1) - 1).astype(f32)
    h = jnp.concatenate([
        _modulated_norm(xp_ref[...], gain, shift, scale) * keep_prev,
        _modulated_norm(x_ref[...], gain, shift, scale),
        _modulated_norm(xn_ref[...], gain, shift, scale) * keep_next,
    ], axis=0).astype(bf16)

    def conv(col):
        u = jnp.dot(h, wup_ref[:, col:col + FF_TILE], preferred_element_type=f32)
        w = cw_ref[:, col:col + FF_TILE]
        full = (pltpu.roll(u, 1, 0) * w[0:1] + u * w[1:2] + pltpu.roll(u, rows - 1, 0) * w[2:3]
                + cb_ref[:, col:col + FF_TILE])
        return full[HALO:HALO + TM]

    for t in range(D_FF // FF_TILE):
        a = conv(t * FF_TILE)
        b = conv(D_FF + t * FF_TILE)
        gelu = 0.5 * a * (1.0 + lax.erf(a * (2.0 ** -0.5)))
        act_ref[:, t * FF_TILE:(t + 1) * FF_TILE] = (gelu * b).astype(bf16)

    y = jnp.dot(act_ref[...], wdn_ref[...], preferred_element_type=f32)
    out = x_ref[...] + mod_ref[5] * y
    if final:
        ms = jnp.mean(out * out, axis=-1, keepdims=True)
        out = out * lax.rsqrt(ms + EPS) * fg_ref[...]
    o_ref[...] = out


def _ffn_call(layer, x, mod, gain, wup, cw, cb, wdn, final_g, final):
    tiles = TM // HALO
    row = pl.BlockSpec((None, TM, D), lambda b, i: (b, i, 0))
    prev = pl.BlockSpec((None, HALO, D), lambda b, i: (b, jnp.maximum(i * tiles - 1, 0), 0))
    nxt = pl.BlockSpec((None, HALO, D),
                       lambda b, i: (b, jnp.minimum((i + 1) * tiles, SEQ // HALO - 1), 0))
    return pl.pallas_call(
        functools.partial(_ffn_kernel, final=final),
        out_shape=jax.ShapeDtypeStruct((BATCH, SEQ, D), f32),
        grid=(BATCH, NT),
        in_specs=[row, prev, nxt,
                  pl.BlockSpec((N_MOD, None, 1, D), lambda b, i: (0, b, 0, 0)),
                  pl.BlockSpec((1, D), lambda b, i: (0, 0)),
                  _layer_resident((D, 2 * D_FF), layer), _layer_resident((3, 2 * D_FF), layer),
                  _layer_resident((1, 2 * D_FF), layer), _layer_resident((D_FF, D), layer),
                  pl.BlockSpec((1, D), lambda b, i: (0, 0))],
        out_specs=row,
        scratch_shapes=[pltpu.VMEM((TM, D_FF), bf16)],
        compiler_params=_params("arbitrary", "arbitrary"),
        name="conv_ffn_final" if final else "conv_ffn",
    )(x, x, x, mod, gain, wup, cw, cb, wdn, final_g)


def kernel(x, c, norm1_g, norm2_g, ada_w, ada_b, w_in, w_fourier, w_ret, w_out,
           ffn_up, conv_w, conv_b, ffn_down, final_g):
    assert x.shape == (BATCH, SEQ, D) and c.shape == (BATCH, D)
    mod_all = _ada_call(c, ada_w, ada_b).reshape(DEPTH, N_MOD, BATCH, 1, D)
    wfr_all, wfi_all, w_rest = _prep_call(w_in)
    cos, sin = _rotary_tables()
    final_gain = final_g.reshape(1, D)
    wf, wr, wo = w_fourier.astype(bf16), w_ret.astype(bf16), w_out.astype(bf16)
    wup, wdn = ffn_up.astype(bf16), ffn_down.astype(bf16)

    for l in range(DEPTH):
        mod = mod_all[l]
        gain1 = norm1_g[l].reshape(1, D)
        q, k, v, gs, af, ar = _inproj_call(l, x, mod, gain1, cos, sin, w_rest)
        tr, ti = _fourier_call(l, x, mod, gain1, wfr_all, wfi_all)
        yf = _dft2_call(tr, ti).reshape(BATCH, SEQ, D)
        yr = _ret_call(q, k, v, gs)
        x = _merge_call(l, x, mod, yf, yr, af, ar, wf, wr, wo)
        x = _ffn_call(l, x, mod, norm2_g[l].reshape(1, D), wup, conv_w,
                      conv_b.reshape(DEPTH, 1, 2 * D_FF), wdn, final_gain,
                      final=(l == DEPTH - 1))
    return x
```

```python
import functools

import numpy as np
import jax
import jax.numpy as jnp
from jax import lax
from jax.experimental import pallas as pl
from jax.experimental.pallas import tpu as pltpu

f32 = jnp.float32
bf16 = jnp.bfloat16

D = 1024
BATCH = 8
SEQ = 4096
DEPTH = 4
GROUP = 128
HEADS = 4
DK = D // HEADS
ROPE_BASE = 10000.0
D_FF = 2816
N_MOD = 6
EPS = 1e-6
D_REST = 6 * D

R = 64
TM = 512
NT = SEQ // TM
RET_CHUNK = 256
RET_ROWS = 1024
FFT_SLABS = 16
FF_TILE = 256
DOWN_TILES = 3
HALO = 8
VMEM_LIMIT = 56 * 1024 * 1024

_NT_DIMS = (((1,), (1,)), ((), ()))
_TN_DIMS = (((0,), (0,)), ((), ()))


def _resident(shape):
    nd = len(shape)
    return pl.BlockSpec(shape, lambda *_: (0,) * nd, pipeline_mode=pl.Buffered(1))


def _layer_resident(shape, layer):
    nd = len(shape)
    return pl.BlockSpec((None,) + tuple(shape), lambda *_: (layer,) + (0,) * nd,
                        pipeline_mode=pl.Buffered(1))


def _params(*sem):
    return pltpu.CompilerParams(dimension_semantics=sem, vmem_limit_bytes=VMEM_LIMIT)


def _group_dft_tables():
    n = np.arange(GROUP)
    ang = 2.0 * np.pi * np.outer(n, n) / GROUP
    s = 1.0 / np.sqrt(GROUP)
    return (np.cos(ang) * s).astype(np.float32), (-np.sin(ang) * s).astype(np.float32)


def _stage_tables():
    k = np.arange(R)
    ang = 2.0 * np.pi * np.outer(k, k) / R
    cr = np.cos(ang) / 8.0
    ci = -np.sin(ang) / 8.0
    stage2 = np.concatenate([cr, -ci], axis=1)
    n2 = np.arange(R)[:, None, None]
    k1 = np.arange(R)[None, :, None]
    n1 = np.arange(R)[None, None, :]
    a = np.exp(-2j * np.pi * (n2 * k1 / SEQ + n1 * k1 / R)) / 8.0
    stage1 = np.concatenate([np.concatenate([a.real, -a.imag], axis=2),
                             np.concatenate([a.imag, a.real], axis=2)], axis=1)
    return stage1.astype(np.float32), stage2.astype(np.float32)


def _retention_tables():
    c = RET_CHUNK
    j = np.arange(c, dtype=np.float64)
    diff = j[:, None] - j[None, :]
    ones = np.ones((1, DK))
    mask = np.zeros((HEADS, c, c))
    qdf, qdb, kdf, kdb = (np.zeros((HEADS, c, DK)) for _ in range(4))
    cdf, cdb = np.zeros(HEADS), np.zeros(HEADS)
    for h in range(HEADS):
        lf = np.log1p(-np.exp2(-5.0 - h))
        lb = np.log1p(-np.exp2(-5.5 - h))
        mask[h] = np.where(diff >= 0, np.exp(lf * np.maximum(diff, 0.0)),
                           np.exp(lb * np.maximum(-diff, 0.0)))
        qdf[h] = np.exp(lf * (j + 1.0))[:, None] * ones
        kdf[h] = np.exp(lf * (c - 1.0 - j))[:, None] * ones
        qdb[h] = np.exp(lb * (c - j))[:, None] * ones
        kdb[h] = np.exp(lb * j)[:, None] * ones
        cdf[h] = np.exp(lf * c)
        cdb[h] = np.exp(lb * c)
    dec = np.stack([qdf, qdb, kdf, kdb])
    return mask.astype(np.float32), dec.astype(np.float32), cdf, cdb


_GROUP_TABLE = np.concatenate(_group_dft_tables(), axis=1)
_GROUP_HI = _GROUP_TABLE.astype(bf16)
_GROUP_LO = (_GROUP_TABLE - _GROUP_HI.astype(np.float32)).astype(bf16)
_STAGE1, _STAGE2 = _stage_tables()
_RMASK, _RDEC, _CDF, _CDB = _retention_tables()
_QDF, _QDB, _KDF, _KDB = range(4)


def _rotary_tables():
    half = DK // 2
    inv_freq = ROPE_BASE ** (-jnp.arange(half, dtype=f32) / half)
    ang = jnp.arange(SEQ, dtype=f32)[:, None] * inv_freq[None, :]
    return jnp.cos(ang), jnp.sin(ang)


def _ada_kernel(c_ref, w_ref, b_ref, o_ref):
    c = c_ref[...]
    act = (c * jax.nn.sigmoid(c)).astype(bf16)
    o_ref[...] = jnp.dot(act, w_ref[...].astype(bf16), preferred_element_type=f32) + b_ref[...]


def _ada_call(c, ada_w, ada_b):
    return pl.pallas_call(
        _ada_kernel,
        out_shape=jax.ShapeDtypeStruct((DEPTH, N_MOD, BATCH, D), f32),
        grid=(DEPTH, N_MOD),
        in_specs=[
            pl.BlockSpec((BATCH, D), lambda l, j: (0, 0)),
            pl.BlockSpec((None, D, D), lambda l, j: (l, 0, j)),
            pl.BlockSpec((None, None, 1, D), lambda l, j: (l, j, 0, 0)),
        ],
        out_specs=pl.BlockSpec((None, None, BATCH, D), lambda l, j: (l, j, 0, 0)),
        compiler_params=_params("arbitrary", "arbitrary"),
        name="ada_mod",
    )(c, ada_w, ada_b.reshape(DEPTH, N_MOD, 1, D))


def _prep_kernel(w_ref, t_hi_ref, t_lo_ref, wr_ref, wi_ref, rest_ref):
    j = pl.program_id(1)

    @pl.when(j == 0)
    def _():
        t_hi = t_hi_ref[...]
        t_lo = t_lo_ref[...]
        for g in range(D // GROUP):
            cols = slice(g * GROUP, (g + 1) * GROUP)
            w = w_ref[:, cols]
            hi = w.astype(bf16)
            lo = (w - hi.astype(f32)).astype(bf16)
            folded = (jnp.dot(hi, t_hi, preferred_element_type=f32)
                      + jnp.dot(lo, t_hi, preferred_element_type=f32)
                      + jnp.dot(hi, t_lo, preferred_element_type=f32))
            wr_ref[:, cols] = folded[:, :GROUP].astype(bf16)
            wi_ref[:, cols] = folded[:, GROUP:].astype(bf16)

    @pl.when(j > 0)
    def _():
        rest_ref[...] = w_ref[...].astype(bf16)


def _prep_call(w_in):
    sq = jax.ShapeDtypeStruct((DEPTH, D, D), bf16)
    tab = pl.BlockSpec((GROUP, 2 * GROUP), lambda l, j: (0, 0))
    fold_out = pl.BlockSpec((None, D, D), lambda l, j: (l, 0, 0))
    return pl.pallas_call(
        _prep_kernel,
        out_shape=(sq, sq, jax.ShapeDtypeStruct((DEPTH, D, D_REST), bf16)),
        grid=(DEPTH, 1 + D_REST // D),
        in_specs=[pl.BlockSpec((None, D, D), lambda l, j: (l, 0, j)), tab, tab],
        out_specs=(fold_out, fold_out,
                   pl.BlockSpec((None, D, D), lambda l, j: (l, 0, jnp.maximum(j - 1, 0)))),
        compiler_params=_params("arbitrary", "arbitrary"),
        name="weight_prep",
    )(w_in, jnp.asarray(_GROUP_HI), jnp.asarray(_GROUP_LO))


def _modulated_norm(x, gain, shift, scale):
    ms = jnp.mean(x * x, axis=-1, keepdims=True)
    return (x * lax.rsqrt(ms + EPS) * gain) * (1.0 + scale) + shift


def _inproj_kernel(x_ref, mod_ref, g_ref, cos_ref, sin_ref, w_ref,
                   q_ref, k_ref, v_ref, gs_ref, af_ref, ar_ref):
    hb = _modulated_norm(x_ref[...], g_ref[...], mod_ref[0], mod_ref[1]).astype(bf16)
    cos = cos_ref[...]
    sin = sin_ref[...]
    half = DK // 2

    def rotary(sec, o_ref, scale):
        p = jnp.dot(hb, w_ref[:, sec * D:(sec + 1) * D], preferred_element_type=f32)
        for h in range(HEADS):
            t1 = p[:, h * DK:h * DK + half]
            t2 = p[:, h * DK + half:(h + 1) * DK]
            o_ref[:, h * DK:h * DK + half] = ((t1 * cos - t2 * sin) * scale).astype(bf16)
            o_ref[:, h * DK + half:(h + 1) * DK] = ((t1 * sin + t2 * cos) * scale).astype(bf16)

    rotary(0, q_ref, DK ** -0.5)
    rotary(1, k_ref, 1.0)
    g = jnp.dot(hb, w_ref[:, 3 * D:4 * D], preferred_element_type=f32)
    gs_ref[...] = (g * jax.nn.sigmoid(g)).astype(bf16)
    af = jnp.dot(hb, w_ref[:, 4 * D:5 * D], preferred_element_type=f32)
    af_ref[...] = jax.nn.sigmoid(af).astype(bf16)
    ar = jnp.dot(hb, w_ref[:, 5 * D:6 * D], preferred_element_type=f32)
    ar_ref[...] = jax.nn.sigmoid(ar).astype(bf16)
    v_ref[...] = jnp.dot(hb, w_ref[:, 2 * D:3 * D], preferred_element_type=f32).astype(bf16)


def _inproj_call(layer, x, mod, gain, cos, sin, w_rest):
    row = pl.BlockSpec((None, TM, D), lambda b, i: (b, i, 0))
    rot = pl.BlockSpec((TM, DK // 2), lambda b, i: (i, 0))
    act = jax.ShapeDtypeStruct((BATCH, SEQ, D), bf16)
    return pl.pallas_call(
        _inproj_kernel,
        out_shape=(act, act, act, act, act, act),
        grid=(BATCH, NT),
        in_specs=[
            row,
            pl.BlockSpec((N_MOD, None, 1, D), lambda b, i: (0, b, 0, 0)),
            pl.BlockSpec((1, D), lambda b, i: (0, 0)),
            rot, rot,
            _layer_resident((D, D_REST), layer),
        ],
        out_specs=(row, row, row, row, row, row),
        compiler_params=_params("arbitrary", "arbitrary"),
        name="in_projection",
    )(x, mod, gain, cos, sin, w_rest)


def _fourier_kernel(x_ref, mod_ref, g_ref, wr_ref, wi_ref, m_ref,
                    tr_ref, ti_ref, sr_ref, si_ref):
    x = pltpu.einshape("abc->bac", x_ref[...]).reshape(FFT_SLABS * R, D)
    hb = _modulated_norm(x, g_ref[...], mod_ref[0], mod_ref[1]).astype(bf16)
    zr = jnp.dot(hb, wr_ref[...], preferred_element_type=f32).astype(bf16)
    zi = jnp.dot(hb, wi_ref[...], preferred_element_type=f32).astype(bf16)
    for s in range(FFT_SLABS):
        z = jnp.concatenate([zr[s * R:(s + 1) * R], zi[s * R:(s + 1) * R]], axis=0)
        t = jnp.dot(m_ref[s], z, preferred_element_type=f32)
        sr_ref[s] = t[:R]
        si_ref[s] = t[R:]
    tr_ref[...] = pltpu.einshape("abc->bac", sr_ref[...]).astype(bf16)
    ti_ref[...] = pltpu.einshape("abc->bac", si_ref[...]).astype(bf16)


def _fourier_call(layer, x, mod, gain, wr, wi):
    nsteps = R // FFT_SLABS
    slab = pl.BlockSpec((None, R, FFT_SLABS, D), lambda b, j: (b, 0, j, 0))
    t2 = jax.ShapeDtypeStruct((BATCH, R, R, D), bf16)
    return pl.pallas_call(
        _fourier_kernel,
        out_shape=(t2, t2),
        grid=(BATCH, nsteps),
        in_specs=[slab,
                  pl.BlockSpec((N_MOD, None, 1, D), lambda b, j: (0, b, 0, 0)),
                  pl.BlockSpec((1, D), lambda b, j: (0, 0)),
                  _layer_resident((D, D), layer), _layer_resident((D, D), layer),
                  pl.BlockSpec((FFT_SLABS, 2 * R, 2 * R), lambda b, j: (j, 0, 0))],
        out_specs=(slab, slab),
        scratch_shapes=[pltpu.VMEM((FFT_SLABS, R, D), f32), pltpu.VMEM((FFT_SLABS, R, D), f32)],
        compiler_params=_params("arbitrary", "arbitrary"),
        name="fourier_projection_dft1",
    )(x.reshape(BATCH, R, R, D), mod, gain, wr, wi,
      jnp.asarray(_STAGE1, dtype=bf16))


def _dft2_kernel(tr_ref, ti_ref, m_ref, y_ref, s_ref):
    m = m_ref[...]
    for s in range(FFT_SLABS):
        t = jnp.concatenate([tr_ref[s], ti_ref[s]], axis=0)
        s_ref[s] = jnp.dot(m, t, preferred_element_type=f32)
    y_ref[...] = pltpu.einshape("abc->bac", s_ref[...]).astype(bf16)


def _dft2_call(tr, ti):
    nsteps = R // FFT_SLABS
    tin = pl.BlockSpec((None, FFT_SLABS, R, D), lambda b, j: (b, j, 0, 0))
    yout = pl.BlockSpec((None, R, FFT_SLABS, D), lambda b, j: (b, 0, j, 0))
    return pl.pallas_call(
        _dft2_kernel,
        out_shape=jax.ShapeDtypeStruct((BATCH, R, R, D), bf16),
        grid=(BATCH, nsteps),
        in_specs=[tin, tin, _resident((R, 2 * R))],
        out_specs=yout,
        scratch_shapes=[pltpu.VMEM((FFT_SLABS, R, D), f32)],
        compiler_params=_params("arbitrary", "arbitrary"),
        name="dft_stage2",
    )(tr, ti, jnp.asarray(_STAGE2, dtype=bf16))


def _ret_kernel(q_ref, k_ref, v_ref, m_ref, dec_ref,
                o_ref, sf_ref, sb_ref, sb_all_ref, decb_ref):
    p = pl.program_id(1)
    j = pl.program_id(2)
    n_chunks = RET_ROWS // RET_CHUNK

    @pl.when(j == 0)
    def _():
        sf_ref[...] = jnp.zeros_like(sf_ref)
        sb_ref[...] = jnp.zeros_like(sb_ref)
        decb_ref[...] = dec_ref[...].astype(bf16)

    @pl.when(p == 0)
    def _():
        block = pl.num_programs(2) - 1 - j
        for c in reversed(range(n_chunks)):
            rows = slice(c * RET_CHUNK, (c + 1) * RET_CHUNK)
            for h in range(HEADS):
                cols = slice(h * DK, (h + 1) * DK)
                state = sb_ref[h]
                sb_all_ref[block * n_chunks + c, h] = state.astype(bf16)
                k_dec = k_ref[rows, cols] * decb_ref[_KDB, h]
                sb_ref[h] = state * float(_CDB[h]) + lax.dot_general(
                    k_dec, v_ref[rows, cols], _TN_DIMS, preferred_element_type=f32)

    @pl.when(p == 1)
    def _():
        for c in range(n_chunks):
            rows = slice(c * RET_CHUNK, (c + 1) * RET_CHUNK)
            for h in range(HEADS):
                cols = slice(h * DK, (h + 1) * DK)
                q = q_ref[rows, cols]
                k = k_ref[rows, cols]
                v = v_ref[rows, cols]
                scores = lax.dot_general(q, k, _NT_DIMS, preferred_element_type=f32) * m_ref[h]
                state = sf_ref[h]
                lhs = jnp.concatenate([scores.astype(bf16), q * decb_ref[_QDF, h],
                                       q * decb_ref[_QDB, h]], axis=1)
                rhs = jnp.concatenate([v, state.astype(bf16), sb_all_ref[j * n_chunks + c, h]],
                                      axis=0)
                y = jnp.dot(lhs, rhs, preferred_element_type=f32)
                k_dec = k * decb_ref[_KDF, h]
                sf_ref[h] = state * float(_CDF[h]) + lax.dot_general(
                    k_dec, v, _TN_DIMS, preferred_element_type=f32)
                o_ref[rows, cols] = y.astype(bf16)


def _ret_call(q, k, v):
    nb = SEQ // RET_ROWS

    def scan_idx(b, p, j):
        return (b, j + (1 - p) * (nb - 1 - 2 * j), 0)

    def finish_idx(b, p, j):
        return (b, p * j, 0)

    scan = pl.BlockSpec((None, RET_ROWS, D), scan_idx)
    fin = pl.BlockSpec((None, RET_ROWS, D), finish_idx)
    return pl.pallas_call(
        _ret_kernel,
        out_shape=jax.ShapeDtypeStruct((BATCH, SEQ, D), bf16),
        grid=(BATCH, 2, nb),
        in_specs=[fin, scan, scan, _resident((HEADS, RET_CHUNK, RET_CHUNK)),
                  _resident((4, HEADS, RET_CHUNK, DK))],
        out_specs=fin,
        scratch_shapes=[pltpu.VMEM((HEADS, DK, DK), f32), pltpu.VMEM((HEADS, DK, DK), f32),
                        pltpu.VMEM((SEQ // RET_CHUNK, HEADS, DK, DK), bf16),
                        pltpu.VMEM((4, HEADS, RET_CHUNK, DK), bf16)],
        compiler_params=_params("arbitrary", "arbitrary", "arbitrary"),
        name="retention",
    )(q, k, v, jnp.asarray(_RMASK), jnp.asarray(_RDEC))


def _merge_kernel(x_ref, mod_ref, yf_ref, yr_ref, gs_ref, af_ref, ar_ref, wf_ref, wr_ref, wo_ref,
                  o_ref):
    a = jnp.dot(yf_ref[...], wf_ref[...], preferred_element_type=f32)
    gated = []
    for h in range(HEADS):
        cols = slice(h * DK, (h + 1) * DK)
        y = yr_ref[:, cols].astype(f32)
        mu = jnp.mean(y, axis=-1, keepdims=True)
        yc = y - mu
        var = jnp.mean(yc * yc, axis=-1, keepdims=True)
        yn = yc * lax.rsqrt(var + EPS)
        gated.append((yn * gs_ref[:, cols].astype(f32)).astype(bf16))
    b = jnp.dot(jnp.concatenate(gated, axis=1), wr_ref[...], preferred_element_type=f32)
    merged = af_ref[...].astype(f32) * a + ar_ref[...].astype(f32) * b
    o = jnp.dot(merged.astype(bf16), wo_ref[...], preferred_element_type=f32)
    o_ref[...] = x_ref[...] + mod_ref[2] * o


def _merge_call(layer, x, mod, yf, yr, gs, af, ar, wf, wr, wo):
    row = pl.BlockSpec((None, TM, D), lambda b, i: (b, i, 0))
    return pl.pallas_call(
        _merge_kernel,
        out_shape=jax.ShapeDtypeStruct((BATCH, SEQ, D), f32),
        grid=(BATCH, NT),
        in_specs=[row, pl.BlockSpec((N_MOD, None, 1, D), lambda b, i: (0, b, 0, 0)),
                  row, row, row, row, row,
                  _layer_resident((D, D), layer), _layer_resident((D, D), layer),
                  _layer_resident((D, D), layer)],
        out_specs=row,
        compiler_params=_params("arbitrary", "arbitrary"),
        name="merge_out_projection",
    )(x, mod, yf, yr, gs, af, ar, wf, wr, wo)


def _ffn_kernel(x_ref, xp_ref, xn_ref, mod_ref, g_ref, wup_ref, cw_ref, cb_ref, wdn_ref, fg_ref,
                o_ref, act_ref, *, final):
    i = pl.program_id(1)
    rows = TM + 2 * HALO
    gain, shift, scale = g_ref[...], mod_ref[3], mod_ref[4]
    keep_prev = (i > 0).astype(f32)
    keep_next = (i < pl.num_programs(1) - 1).astype(f32)
    h = jnp.concatenate([
        _modulated_norm(xp_ref[...], gain, shift, scale) * keep_prev,
        _modulated_norm(x_ref[...], gain, shift, scale),
        _modulated_norm(xn_ref[...], gain, shift, scale) * keep_next,
    ], axis=0).astype(bf16)

    def conv(col):
        u = jnp.dot(h, wup_ref[:, col:col + FF_TILE], preferred_element_type=f32)
        w = cw_ref[:, col:col + FF_TILE]
        full = (pltpu.roll(u, 1, 0) * w[0:1] + u * w[1:2] + pltpu.roll(u, rows - 1, 0) * w[2:3]
                + cb_ref[:, col:col + FF_TILE])
        return full[HALO:HALO + TM]

    y = None
    n_tiles = D_FF // FF_TILE
    for t in range(n_tiles):
        a = conv(t * FF_TILE)
        b = conv(D_FF + t * FF_TILE)
        gelu = 0.5 * a * (1.0 + lax.erf(a * (2.0 ** -0.5)))
        act_ref[:, t * FF_TILE:(t + 1) * FF_TILE] = (gelu * b).astype(bf16)
        if (t + 1) % DOWN_TILES == 0 or t == n_tiles - 1:
            lo = (t // DOWN_TILES) * DOWN_TILES * FF_TILE
            hi = (t + 1) * FF_TILE
            part = jnp.dot(act_ref[:, lo:hi], wdn_ref[lo:hi, :], preferred_element_type=f32)
            y = part if y is None else y + part
    out = x_ref[...] + mod_ref[5] * y
    if final:
        ms = jnp.mean(out * out, axis=-1, keepdims=True)
        out = out * lax.rsqrt(ms + EPS) * fg_ref[...]
    o_ref[...] = out


def _ffn_call(layer, x, mod, gain, wup, cw, cb, wdn, final_g, final):
    tiles = TM // HALO
    row = pl.BlockSpec((None, TM, D), lambda b, i: (b, i, 0))
    prev = pl.BlockSpec((None, HALO, D), lambda b, i: (b, jnp.maximum(i * tiles - 1, 0), 0))
    nxt = pl.BlockSpec((None, HALO, D),
                       lambda b, i: (b, jnp.minimum((i + 1) * tiles, SEQ // HALO - 1), 0))
    return pl.pallas_call(
        functools.partial(_ffn_kernel, final=final),
        out_shape=jax.ShapeDtypeStruct((BATCH, SEQ, D), f32),
        grid=(BATCH, NT),
        in_specs=[row, prev, nxt,
                  pl.BlockSpec((N_MOD, None, 1, D), lambda b, i: (0, b, 0, 0)),
                  pl.BlockSpec((1, D), lambda b, i: (0, 0)),
                  _layer_resident((D, 2 * D_FF), layer), _layer_resident((3, 2 * D_FF), layer),
                  _layer_resident((1, 2 * D_FF), layer), _layer_resident((D_FF, D), layer),
                  pl.BlockSpec((1, D), lambda b, i: (0, 0))],
        out_specs=row,
        scratch_shapes=[pltpu.VMEM((TM, D_FF), bf16)],
        compiler_params=_params("arbitrary", "arbitrary"),
        name="conv_ffn_final" if final else "conv_ffn",
    )(x, x, x, mod, gain, wup, cw, cb, wdn, final_g)


def kernel(x, c, norm1_g, norm2_g, ada_w, ada_b, w_in, w_fourier, w_ret, w_out,
           ffn_up, conv_w, conv_b, ffn_down, final_g):
    assert x.shape == (BATCH, SEQ, D) and c.shape == (BATCH, D)
    mod_all = _ada_call(c, ada_w, ada_b).reshape(DEPTH, N_MOD, BATCH, 1, D)
    wfr_all, wfi_all, w_rest = _prep_call(w_in)
    cos, sin = _rotary_tables()
    final_gain = final_g.reshape(1, D)
    wf, wr, wo = w_fourier.astype(bf16), w_ret.astype(bf16), w_out.astype(bf16)
    wup, wdn = ffn_up.astype(bf16), ffn_down.astype(bf16)

    for l in range(DEPTH):
        mod = mod_all[l]
        gain1 = norm1_g[l].reshape(1, D)
        q, k, v, gs, af, ar = _inproj_call(l, x, mod, gain1, cos, sin, w_rest)
        tr, ti = _fourier_call(l, x, mod, gain1, wfr_all, wfi_all)
        yf = _dft2_call(tr, ti).reshape(BATCH, SEQ, D)
        yr = _ret_call(q, k, v)
        x = _merge_call(l, x, mod, yf, yr, gs, af, ar, wf, wr, wo)
        x = _ffn_call(l, x, mod, norm2_g[l].reshape(1, D), wup, conv_w,
                      conv_b.reshape(DEPTH, 1, 2 * D_FF), wdn, final_gain,
                      final=(l == DEPTH - 1))
    return x
```

```python
import functools

import numpy as np
import jax
import jax.numpy as jnp
from jax import lax
from jax.experimental import pallas as pl
from jax.experimental.pallas import tpu as pltpu

f32 = jnp.float32
bf16 = jnp.bfloat16

D = 1024
BATCH = 8
SEQ = 4096
DEPTH = 4
GROUP = 128
HEADS = 4
DK = D // HEADS
ROPE_BASE = 10000.0
D_FF = 2816
N_MOD = 6
EPS = 1e-6
D_REST = 6 * D

R = 64
TM = 512
NT = SEQ // TM
RET_CHUNK = 256
RET_ROWS = 1024
FFT_SLABS = 16
FF_TILE = 256
HALO = 8
VMEM_LIMIT = 56 * 1024 * 1024

_NT_DIMS = (((1,), (1,)), ((), ()))
_TN_DIMS = (((0,), (0,)), ((), ()))


def _resident(shape):
    nd = len(shape)
    return pl.BlockSpec(shape, lambda *_: (0,) * nd, pipeline_mode=pl.Buffered(1))


def _layer_resident(shape, layer):
    nd = len(shape)
    return pl.BlockSpec((None,) + tuple(shape), lambda *_: (layer,) + (0,) * nd,
                        pipeline_mode=pl.Buffered(1))


def _params(*sem):
    return pltpu.CompilerParams(dimension_semantics=sem, vmem_limit_bytes=VMEM_LIMIT)


def _group_dft_tables():
    n = np.arange(GROUP)
    ang = 2.0 * np.pi * np.outer(n, n) / GROUP
    s = 1.0 / np.sqrt(GROUP)
    return (np.cos(ang) * s).astype(np.float32), (-np.sin(ang) * s).astype(np.float32)


def _stage_tables():
    k = np.arange(R)
    ang = 2.0 * np.pi * np.outer(k, k) / R
    cr = np.cos(ang) / 8.0
    ci = -np.sin(ang) / 8.0
    stage2 = np.concatenate([cr, -ci], axis=1)
    n2 = np.arange(R)[:, None, None]
    k1 = np.arange(R)[None, :, None]
    n1 = np.arange(R)[None, None, :]
    a = np.exp(-2j * np.pi * (n2 * k1 / SEQ + n1 * k1 / R)) / 8.0
    stage1 = np.concatenate([np.concatenate([a.real, -a.imag], axis=2),
                             np.concatenate([a.imag, a.real], axis=2)], axis=1)
    return stage1.astype(np.float32), stage2.astype(np.float32)


def _retention_tables():
    c = RET_CHUNK
    j = np.arange(c, dtype=np.float64)
    diff = j[:, None] - j[None, :]
    ones = np.ones((1, DK))
    mask = np.zeros((HEADS, c, c))
    qdf, qdb, kdf, kdb = (np.zeros((HEADS, c, DK)) for _ in range(4))
    cdf, cdb = np.zeros(HEADS), np.zeros(HEADS)
    for h in range(HEADS):
        lf = np.log1p(-np.exp2(-5.0 - h))
        lb = np.log1p(-np.exp2(-5.5 - h))
        mask[h] = np.where(diff >= 0, np.exp(lf * np.maximum(diff, 0.0)),
                           np.exp(lb * np.maximum(-diff, 0.0)))
        qdf[h] = np.exp(lf * (j + 1.0))[:, None] * ones
        kdf[h] = np.exp(lf * (c - 1.0 - j))[:, None] * ones
        qdb[h] = np.exp(lb * (c - j))[:, None] * ones
        kdb[h] = np.exp(lb * j)[:, None] * ones
        cdf[h] = np.exp(lf * c)
        cdb[h] = np.exp(lb * c)
    dec = np.stack([qdf, qdb, kdf, kdb])
    return mask.astype(np.float32), dec.astype(np.float32), cdf, cdb


_GROUP_TABLE = np.concatenate(_group_dft_tables(), axis=1)
_GROUP_HI = _GROUP_TABLE.astype(bf16)
_GROUP_LO = (_GROUP_TABLE - _GROUP_HI.astype(np.float32)).astype(bf16)
_STAGE1, _STAGE2 = _stage_tables()
_RMASK, _RDEC, _CDF, _CDB = _retention_tables()
_QDF, _QDB, _KDF, _KDB = range(4)


def _rotary_tables():
    half = DK // 2
    inv_freq = ROPE_BASE ** (-jnp.arange(half, dtype=f32) / half)
    ang = jnp.arange(SEQ, dtype=f32)[:, None] * inv_freq[None, :]
    return jnp.cos(ang), jnp.sin(ang)


def _ada_kernel(c_ref, w_ref, b_ref, o_ref):
    c = c_ref[...]
    act = (c * jax.nn.sigmoid(c)).astype(bf16)
    o_ref[...] = jnp.dot(act, w_ref[...].astype(bf16), preferred_element_type=f32) + b_ref[...]


def _ada_call(c, ada_w, ada_b):
    return pl.pallas_call(
        _ada_kernel,
        out_shape=jax.ShapeDtypeStruct((DEPTH, N_MOD, BATCH, D), f32),
        grid=(DEPTH, N_MOD),
        in_specs=[
            pl.BlockSpec((BATCH, D), lambda l, j: (0, 0)),
            pl.BlockSpec((None, D, D), lambda l, j: (l, 0, j)),
            pl.BlockSpec((None, None, 1, D), lambda l, j: (l, j, 0, 0)),
        ],
        out_specs=pl.BlockSpec((None, None, BATCH, D), lambda l, j: (l, j, 0, 0)),
        compiler_params=_params("arbitrary", "arbitrary"),
        name="ada_mod",
    )(c, ada_w, ada_b.reshape(DEPTH, N_MOD, 1, D))


def _prep_kernel(w_ref, t_hi_ref, t_lo_ref, wr_ref, wi_ref, rest_ref):
    j = pl.program_id(1)

    @pl.when(j == 0)
    def _():
        t_hi = t_hi_ref[...]
        t_lo = t_lo_ref[...]
        for g in range(D // GROUP):
            cols = slice(g * GROUP, (g + 1) * GROUP)
            w = w_ref[:, cols]
            hi = w.astype(bf16)
            lo = (w - hi.astype(f32)).astype(bf16)
            folded = (jnp.dot(hi, t_hi, preferred_element_type=f32)
                      + jnp.dot(lo, t_hi, preferred_element_type=f32)
                      + jnp.dot(hi, t_lo, preferred_element_type=f32))
            wr_ref[:, cols] = folded[:, :GROUP].astype(bf16)
            wi_ref[:, cols] = folded[:, GROUP:].astype(bf16)

    @pl.when(j > 0)
    def _():
        rest_ref[...] = w_ref[...].astype(bf16)


def _prep_call(w_in):
    sq = jax.ShapeDtypeStruct((DEPTH, D, D), bf16)
    tab = pl.BlockSpec((GROUP, 2 * GROUP), lambda l, j: (0, 0))
    fold_out = pl.BlockSpec((None, D, D), lambda l, j: (l, 0, 0))
    return pl.pallas_call(
        _prep_kernel,
        out_shape=(sq, sq, jax.ShapeDtypeStruct((DEPTH, D, D_REST), bf16)),
        grid=(DEPTH, 1 + D_REST // D),
        in_specs=[pl.BlockSpec((None, D, D), lambda l, j: (l, 0, j)), tab, tab],
        out_specs=(fold_out, fold_out,
                   pl.BlockSpec((None, D, D), lambda l, j: (l, 0, jnp.maximum(j - 1, 0)))),
        compiler_params=_params("arbitrary", "arbitrary"),
        name="weight_prep",
    )(w_in, jnp.asarray(_GROUP_HI), jnp.asarray(_GROUP_LO))


def _modulated_norm(x, gain, shift, scale):
    ms = jnp.mean(x * x, axis=-1, keepdims=True)
    return (x * lax.rsqrt(ms + EPS) * gain) * (1.0 + scale) + shift


def _inproj_kernel(x_ref, mod_ref, g_ref, cos_ref, sin_ref, w_ref,
                   q_ref, k_ref, v_ref, gs_ref, af_ref, ar_ref):
    hb = _modulated_norm(x_ref[...], g_ref[...], mod_ref[0], mod_ref[1]).astype(bf16)
    cos = cos_ref[...]
    sin = sin_ref[...]
    half = DK // 2

    def rotary(sec, o_ref, scale):
        p = jnp.dot(hb, w_ref[:, sec * D:(sec + 1) * D], preferred_element_type=f32)
        for h in range(HEADS):
            t1 = p[:, h * DK:h * DK + half]
            t2 = p[:, h * DK + half:(h + 1) * DK]
            o_ref[:, h * DK:h * DK + half] = ((t1 * cos - t2 * sin) * scale).astype(bf16)
            o_ref[:, h * DK + half:(h + 1) * DK] = ((t1 * sin + t2 * cos) * scale).astype(bf16)

    rotary(0, q_ref, DK ** -0.5)
    rotary(1, k_ref, 1.0)
    g = jnp.dot(hb, w_ref[:, 3 * D:4 * D], preferred_element_type=f32)
    gs_ref[...] = (g * jax.nn.sigmoid(g)).astype(bf16)
    af = jnp.dot(hb, w_ref[:, 4 * D:5 * D], preferred_element_type=f32)
    af_ref[...] = jax.nn.sigmoid(af).astype(bf16)
    ar = jnp.dot(hb, w_ref[:, 5 * D:6 * D], preferred_element_type=f32)
    ar_ref[...] = jax.nn.sigmoid(ar).astype(bf16)
    v_ref[...] = jnp.dot(hb, w_ref[:, 2 * D:3 * D], preferred_element_type=f32).astype(bf16)


def _inproj_call(layer, x, mod, gain, cos, sin, w_rest):
    row = pl.BlockSpec((None, TM, D), lambda b, i: (b, i, 0))
    rot = pl.BlockSpec((TM, DK // 2), lambda b, i: (i, 0))
    act = jax.ShapeDtypeStruct((BATCH, SEQ, D), bf16)
    return pl.pallas_call(
        _inproj_kernel,
        out_shape=(act, act, act, act, act, act),
        grid=(BATCH, NT),
        in_specs=[
            row,
            pl.BlockSpec((N_MOD, None, 1, D), lambda b, i: (0, b, 0, 0)),
            pl.BlockSpec((1, D), lambda b, i: (0, 0)),
            rot, rot,
            _layer_resident((D, D_REST), layer),
        ],
        out_specs=(row, row, row, row, row, row),
        compiler_params=_params("arbitrary", "arbitrary"),
        name="in_projection",
    )(x, mod, gain, cos, sin, w_rest)


def _fourier_kernel(x_ref, mod_ref, g_ref, wr_ref, wi_ref, m_ref,
                    tr_ref, ti_ref, sr_ref, si_ref):
    x = pltpu.einshape("abc->bac", x_ref[...]).reshape(FFT_SLABS * R, D)
    hb = _modulated_norm(x, g_ref[...], mod_ref[0], mod_ref[1]).astype(bf16)
    zr = jnp.dot(hb, wr_ref[...], preferred_element_type=f32).astype(bf16)
    zi = jnp.dot(hb, wi_ref[...], preferred_element_type=f32).astype(bf16)
    for s in range(FFT_SLABS):
        z = jnp.concatenate([zr[s * R:(s + 1) * R], zi[s * R:(s + 1) * R]], axis=0)
        t = jnp.dot(m_ref[s], z, preferred_element_type=f32)
        sr_ref[s] = t[:R]
        si_ref[s] = t[R:]
    tr_ref[...] = pltpu.einshape("abc->bac", sr_ref[...]).astype(bf16)
    ti_ref[...] = pltpu.einshape("abc->bac", si_ref[...]).astype(bf16)


def _fourier_call(layer, x, mod, gain, wr, wi):
    nsteps = R // FFT_SLABS
    slab = pl.BlockSpec((None, R, FFT_SLABS, D), lambda b, j: (b, 0, j, 0))
    t2 = jax.ShapeDtypeStruct((BATCH, R, R, D), bf16)
    return pl.pallas_call(
        _fourier_kernel,
        out_shape=(t2, t2),
        grid=(BATCH, nsteps),
        in_specs=[slab,
                  pl.BlockSpec((N_MOD, None, 1, D), lambda b, j: (0, b, 0, 0)),
                  pl.BlockSpec((1, D), lambda b, j: (0, 0)),
                  _layer_resident((D, D), layer), _layer_resident((D, D), layer),
                  pl.BlockSpec((FFT_SLABS, 2 * R, 2 * R), lambda b, j: (j, 0, 0))],
        out_specs=(slab, slab),
        scratch_shapes=[pltpu.VMEM((FFT_SLABS, R, D), f32), pltpu.VMEM((FFT_SLABS, R, D), f32)],
        compiler_params=_params("arbitrary", "arbitrary"),
        name="fourier_projection_dft1",
    )(x.reshape(BATCH, R, R, D), mod, gain, wr, wi,
      jnp.asarray(_STAGE1, dtype=bf16))


def _dft2_kernel(tr_ref, ti_ref, m_ref, y_ref, s_ref):
    m = m_ref[...]
    for s in range(FFT_SLABS):
        t = jnp.concatenate([tr_ref[s], ti_ref[s]], axis=0)
        s_ref[s] = jnp.dot(m, t, preferred_element_type=f32)
    y_ref[...] = pltpu.einshape("abc->bac", s_ref[...]).astype(bf16)


def _dft2_call(tr, ti):
    nsteps = R // FFT_SLABS
    tin = pl.BlockSpec((None, FFT_SLABS, R, D), lambda b, j: (b, j, 0, 0))
    yout = pl.BlockSpec((None, R, FFT_SLABS, D), lambda b, j: (b, 0, j, 0))
    return pl.pallas_call(
        _dft2_kernel,
        out_shape=jax.ShapeDtypeStruct((BATCH, R, R, D), bf16),
        grid=(BATCH, nsteps),
        in_specs=[tin, tin, _resident((R, 2 * R))],
        out_specs=yout,
        scratch_shapes=[pltpu.VMEM((FFT_SLABS, R, D), f32)],
        compiler_params=_params("arbitrary", "arbitrary"),
        name="dft_stage2",
    )(tr, ti, jnp.asarray(_STAGE2, dtype=bf16))


def _ret_kernel(q_ref, k_ref, v_ref, m_ref, dec_ref,
                o_ref, sf_ref, sb_ref, sb_all_ref, decb_ref):
    p = pl.program_id(1)
    j = pl.program_id(2)
    n_chunks = RET_ROWS // RET_CHUNK

    @pl.when(j == 0)
    def _():
        sf_ref[...] = jnp.zeros_like(sf_ref)
        sb_ref[...] = jnp.zeros_like(sb_ref)
        decb_ref[...] = dec_ref[...].astype(bf16)

    @pl.when(p == 0)
    def _():
        block = pl.num_programs(2) - 1 - j
        for c in reversed(range(n_chunks)):
            rows = slice(c * RET_CHUNK, (c + 1) * RET_CHUNK)
            for h in range(HEADS):
                cols = slice(h * DK, (h + 1) * DK)
                state = sb_ref[h]
                sb_all_ref[block * n_chunks + c, h] = state.astype(bf16)
                k_dec = k_ref[rows, cols] * decb_ref[_KDB, h]
                sb_ref[h] = state * float(_CDB[h]) + lax.dot_general(
                    k_dec, v_ref[rows, cols], _TN_DIMS, preferred_element_type=f32)

    @pl.when(p == 1)
    def _():
        for c in range(n_chunks):
            rows = slice(c * RET_CHUNK, (c + 1) * RET_CHUNK)
            for h in range(HEADS):
                cols = slice(h * DK, (h + 1) * DK)
                q = q_ref[rows, cols]
                k = k_ref[rows, cols]
                v = v_ref[rows, cols]
                scores = lax.dot_general(q, k, _NT_DIMS, preferred_element_type=f32) * m_ref[h]
                state = sf_ref[h]
                lhs = jnp.concatenate([scores.astype(bf16), q * decb_ref[_QDF, h],
                                       q * decb_ref[_QDB, h]], axis=1)
                rhs = jnp.concatenate([v, state.astype(bf16), sb_all_ref[j * n_chunks + c, h]],
                                      axis=0)
                y = jnp.dot(lhs, rhs, preferred_element_type=f32)
                k_dec = k * decb_ref[_KDF, h]
                sf_ref[h] = state * float(_CDF[h]) + lax.dot_general(
                    k_dec, v, _TN_DIMS, preferred_element_type=f32)
                o_ref[rows, cols] = y.astype(bf16)


def _ret_call(q, k, v):
    nb = SEQ // RET_ROWS

    def scan_idx(b, p, j):
        return (b, j + (1 - p) * (nb - 1 - 2 * j), 0)

    def finish_idx(b, p, j):
        return (b, p * j, 0)

    scan = pl.BlockSpec((None, RET_ROWS, D), scan_idx)
    fin = pl.BlockSpec((None, RET_ROWS, D), finish_idx)
    return pl.pallas_call(
        _ret_kernel,
        out_shape=jax.ShapeDtypeStruct((BATCH, SEQ, D), bf16),
        grid=(BATCH, 2, nb),
        in_specs=[fin, scan, scan, _resident((HEADS, RET_CHUNK, RET_CHUNK)),
                  _resident((4, HEADS, RET_CHUNK, DK))],
        out_specs=fin,
        scratch_shapes=[pltpu.VMEM((HEADS, DK, DK), f32), pltpu.VMEM((HEADS, DK, DK), f32),
                        pltpu.VMEM((SEQ // RET_CHUNK, HEADS, DK, DK), bf16),
                        pltpu.VMEM((4, HEADS, RET_CHUNK, DK), bf16)],
        compiler_params=_params("arbitrary", "arbitrary", "arbitrary"),
        name="retention",
    )(q, k, v, jnp.asarray(_RMASK), jnp.asarray(_RDEC))


def _merge_kernel(x_ref, mod_ref, yf_ref, yr_ref, gs_ref, af_ref, ar_ref, wf_ref, wr_ref, wo_ref,
                  o_ref):
    a = jnp.dot(yf_ref[...], wf_ref[...], preferred_element_type=f32)
    gated = []
    for h in range(HEADS):
        cols = slice(h * DK, (h + 1) * DK)
        y = yr_ref[:, cols].astype(f32)
        mu = jnp.mean(y, axis=-1, keepdims=True)
        yc = y - mu
        var = jnp.mean(yc * yc, axis=-1, keepdims=True)
        yn = yc * lax.rsqrt(var + EPS)
        gated.append((yn * gs_ref[:, cols].astype(f32)).astype(bf16))
    b = jnp.dot(jnp.concatenate(gated, axis=1), wr_ref[...], preferred_element_type=f32)
    merged = af_ref[...].astype(f32) * a + ar_ref[...].astype(f32) * b
    o = jnp.dot(merged.astype(bf16), wo_ref[...], preferred_element_type=f32)
    o_ref[...] = x_ref[...] + mod_ref[2] * o


def _merge_call(layer, x, mod, yf, yr, gs, af, ar, wf, wr, wo):
    row = pl.BlockSpec((None, TM, D), lambda b, i: (b, i, 0))
    return pl.pallas_call(
        _merge_kernel,
        out_shape=jax.ShapeDtypeStruct((BATCH, SEQ, D), f32),
        grid=(BATCH, NT),
        in_specs=[row, pl.BlockSpec((N_MOD, None, 1, D), lambda b, i: (0, b, 0, 0)),
                  row, row, row, row, row,
                  _layer_resident((D, D), layer), _layer_resident((D, D), layer),
                  _layer_resident((D, D), layer)],
        out_specs=row,
        compiler_params=_params("arbitrary", "arbitrary"),
        name="merge_out_projection",
    )(x, mod, yf, yr, gs, af, ar, wf, wr, wo)


def _ffn_kernel(x_ref, xp_ref, xn_ref, mod_ref, g_ref, wup_ref, cw_ref, cb_ref, wdn_ref, fg_ref,
                o_ref, act_ref, *, final):
    i = pl.program_id(1)
    rows = TM + 2 * HALO
    gain, shift, scale = g_ref[...], mod_ref[3], mod_ref[4]
    keep_prev = (i > 0).astype(f32)
    keep_next = (i < pl.num_programs(1) - 1).astype(f32)
    h = jnp.concatenate([
        _modulated_norm(xp_ref[...], gain, shift, scale) * keep_prev,
        _modulated_norm(x_ref[...], gain, shift, scale),
        _modulated_norm(xn_ref[...], gain, shift, scale) * keep_next,
    ], axis=0).astype(bf16)

    def conv(col):
        u = jnp.dot(h, wup_ref[:, col:col + FF_TILE], preferred_element_type=f32)
        w = cw_ref[:, col:col + FF_TILE]
        full = (pltpu.roll(u, 1, 0) * w[0:1] + u * w[1:2] + pltpu.roll(u, rows - 1, 0) * w[2:3]
                + cb_ref[:, col:col + FF_TILE])
        return full[HALO:HALO + TM]

    for t in range(D_FF // FF_TILE):
        a = conv(t * FF_TILE)
        b = conv(D_FF + t * FF_TILE)
        gelu = 0.5 * a * (1.0 + lax.erf(a * (2.0 ** -0.5)))
        act_ref[:, t * FF_TILE:(t + 1) * FF_TILE] = (gelu * b).astype(bf16)

    y = jnp.dot(act_ref[...], wdn_ref[...], preferred_element_type=f32)
    out = x_ref[...] + mod_ref[5] * y
    if final:
        ms = jnp.mean(out * out, axis=-1, keepdims=True)
        out = out * lax.rsqrt(ms + EPS) * fg_ref[...]
    o_ref[...] = out


def _ffn_call(layer, x, mod, gain, wup, cw, cb, wdn, final_g, final):
    tiles = TM // HALO
    row = pl.BlockSpec((None, TM, D), lambda b, i: (b, i, 0))
    prev = pl.BlockSpec((None, HALO, D), lambda b, i: (b, jnp.maximum(i * tiles - 1, 0), 0))
    nxt = pl.BlockSpec((None, HALO, D),
                       lambda b, i: (b, jnp.minimum((i + 1) * tiles, SEQ // HALO - 1), 0))
    return pl.pallas_call(
        functools.partial(_ffn_kernel, final=final),
        out_shape=jax.ShapeDtypeStruct((BATCH, SEQ, D), f32),
        grid=(BATCH, NT),
        in_specs=[row, prev, nxt,
                  pl.BlockSpec((N_MOD, None, 1, D), lambda b, i: (0, b, 0, 0)),
                  pl.BlockSpec((1, D), lambda b, i: (0, 0)),
                  _layer_resident((D, 2 * D_FF), layer), _layer_resident((3, 2 * D_FF), layer),
                  _layer_resident((1, 2 * D_FF), layer), _layer_resident((D_FF, D), layer),
                  pl.BlockSpec((1, D), lambda b, i: (0, 0))],
        out_specs=row,
        scratch_shapes=[pltpu.VMEM((TM, D_FF), bf16)],
        compiler_params=_params("arbitrary", "arbitrary"),
        name="conv_ffn_final" if final else "conv_ffn",
    )(x, x, x, mod, gain, wup, cw, cb, wdn, final_g)


def kernel(x, c, norm1_g, norm2_g, ada_w, ada_b, w_in, w_fourier, w_ret, w_out,
           ffn_up, conv_w, conv_b, ffn_down, final_g):
    assert x.shape == (BATCH, SEQ, D) and c.shape == (BATCH, D)
    mod_all = _ada_call(c, ada_w, ada_b).reshape(DEPTH, N_MOD, BATCH, 1, D)
    wfr_all, wfi_all, w_rest = _prep_call(w_in)
    cos, sin = _rotary_tables()
    final_gain = final_g.reshape(1, D)
    wf, wr, wo = w_fourier.astype(bf16), w_ret.astype(bf16), w_out.astype(bf16)
    wup, wdn = ffn_up.astype(bf16), ffn_down.astype(bf16)

    for l in range(DEPTH):
        mod = mod_all[l]
        gain1 = norm1_g[l].reshape(1, D)
        q, k, v, gs, af, ar = _inproj_call(l, x, mod, gain1, cos, sin, w_rest)
        tr, ti = _fourier_call(l, x, mod, gain1, wfr_all, wfi_all)
        yf = _dft2_call(tr, ti).reshape(BATCH, SEQ, D)
        yr = _ret_call(q, k, v)
        x = _merge_call(l, x, mod, yf, yr, gs, af, ar, wf, wr, wo)
        x = _ffn_call(l, x, mod, norm2_g[l].reshape(1, D), wup, conv_w,
                      conv_b.reshape(DEPTH, 1, 2 * D_FF), wdn, final_gain,
                      final=(l == DEPTH - 1))
    return x
```

```python
import functools

import numpy as np
import jax
import jax.numpy as jnp
from jax import lax
from jax.experimental import pallas as pl
from jax.experimental.pallas import tpu as pltpu

f32 = jnp.float32
bf16 = jnp.bfloat16

D = 1024
BATCH = 8
SEQ = 4096
DEPTH = 4
GROUP = 128
HEADS = 4
DK = D // HEADS
ROPE_BASE = 10000.0
D_FF = 2816
N_MOD = 6
EPS = 1e-6
D_REST = 6 * D

R = 64
TM = 512
NT = SEQ // TM
RET_CHUNK = 256
RET_ROWS = 1024
FFT_SLABS = 16
FF_TILE = 256
HALO = 8
VMEM_LIMIT = 56 * 1024 * 1024

_NT_DIMS = (((1,), (1,)), ((), ()))
_TN_DIMS = (((0,), (0,)), ((), ()))


def _resident(shape):
    nd = len(shape)
    return pl.BlockSpec(shape, lambda *_: (0,) * nd, pipeline_mode=pl.Buffered(1))


def _layer_resident(shape, layer):
    nd = len(shape)
    return pl.BlockSpec((None,) + tuple(shape), lambda *_: (layer,) + (0,) * nd,
                        pipeline_mode=pl.Buffered(1))


def _params(*sem):
    return pltpu.CompilerParams(dimension_semantics=sem, vmem_limit_bytes=VMEM_LIMIT)


def _group_dft_tables():
    n = np.arange(GROUP)
    ang = 2.0 * np.pi * np.outer(n, n) / GROUP
    s = 1.0 / np.sqrt(GROUP)
    return (np.cos(ang) * s).astype(np.float32), (-np.sin(ang) * s).astype(np.float32)


def _stage_tables():
    k = np.arange(R)
    ang = 2.0 * np.pi * np.outer(k, k) / R
    cr = np.cos(ang) / 8.0
    ci = -np.sin(ang) / 8.0
    stage2 = np.concatenate([cr, -ci], axis=1)
    n2 = np.arange(R)[:, None, None]
    k1 = np.arange(R)[None, :, None]
    n1 = np.arange(R)[None, None, :]
    a = np.exp(-2j * np.pi * (n2 * k1 / SEQ + n1 * k1 / R)) / 8.0
    stage1 = np.concatenate([np.concatenate([a.real, -a.imag], axis=2),
                             np.concatenate([a.imag, a.real], axis=2)], axis=1)
    return stage1.astype(np.float32), stage2.astype(np.float32)


def _retention_tables():
    c = RET_CHUNK
    j = np.arange(c, dtype=np.float64)
    diff = j[:, None] - j[None, :]
    ones = np.ones((1, DK))
    mask = np.zeros((HEADS, c, c))
    qdf, qdb, kdf, kdb = (np.zeros((HEADS, c, DK)) for _ in range(4))
    cdf, cdb = np.zeros(HEADS), np.zeros(HEADS)
    for h in range(HEADS):
        lf = np.log1p(-np.exp2(-5.0 - h))
        lb = np.log1p(-np.exp2(-5.5 - h))
        mask[h] = np.where(diff >= 0, np.exp(lf * np.maximum(diff, 0.0)),
                           np.exp(lb * np.maximum(-diff, 0.0)))
        qdf[h] = np.exp(lf * (j + 1.0))[:, None] * ones
        kdf[h] = np.exp(lf * (c - 1.0 - j))[:, None] * ones
        qdb[h] = np.exp(lb * (c - j))[:, None] * ones
        kdb[h] = np.exp(lb * j)[:, None] * ones
        cdf[h] = np.exp(lf * c)
        cdb[h] = np.exp(lb * c)
    dec = np.stack([qdf, qdb, kdf, kdb])
    return mask.astype(np.float32), dec.astype(np.float32), cdf, cdb


_GROUP_TABLE = np.concatenate(_group_dft_tables(), axis=1)
_STAGE1, _STAGE2 = _stage_tables()
_RMASK, _RDEC, _CDF, _CDB = _retention_tables()
_QDF, _QDB, _KDF, _KDB = range(4)


def _rotary_tables():
    half = DK // 2
    inv_freq = ROPE_BASE ** (-jnp.arange(half, dtype=f32) / half)
    ang = jnp.arange(SEQ, dtype=f32)[:, None] * inv_freq[None, :]
    return jnp.cos(ang), jnp.sin(ang)


def _ada_kernel(c_ref, w_ref, b_ref, o_ref):
    c = c_ref[...]
    act = (c * jax.nn.sigmoid(c)).astype(bf16)
    o_ref[...] = jnp.dot(act, w_ref[...].astype(bf16), preferred_element_type=f32) + b_ref[...]


def _ada_call(c, ada_w, ada_b):
    return pl.pallas_call(
        _ada_kernel,
        out_shape=jax.ShapeDtypeStruct((DEPTH, N_MOD, BATCH, D), f32),
        grid=(DEPTH, N_MOD),
        in_specs=[
            pl.BlockSpec((BATCH, D), lambda l, j: (0, 0)),
            pl.BlockSpec((None, D, D), lambda l, j: (l, 0, j)),
            pl.BlockSpec((None, None, 1, D), lambda l, j: (l, j, 0, 0)),
        ],
        out_specs=pl.BlockSpec((None, None, BATCH, D), lambda l, j: (l, j, 0, 0)),
        compiler_params=_params("arbitrary", "arbitrary"),
        name="ada_mod",
    )(c, ada_w, ada_b.reshape(DEPTH, N_MOD, 1, D))


def _prep_kernel(w_ref, t_ref, wr_ref, wi_ref, rest_ref):
    j = pl.program_id(1)

    @pl.when(j == 0)
    def _():
        table = t_ref[...]
        t_hi = table.astype(bf16)
        t_lo = (table - t_hi.astype(f32)).astype(bf16)
        for g in range(D // GROUP):
            cols = slice(g * GROUP, (g + 1) * GROUP)
            w = w_ref[:, cols]
            hi = w.astype(bf16)
            lo = (w - hi.astype(f32)).astype(bf16)
            folded = (jnp.dot(hi, t_hi, preferred_element_type=f32)
                      + jnp.dot(lo, t_hi, preferred_element_type=f32)
                      + jnp.dot(hi, t_lo, preferred_element_type=f32))
            wr_ref[:, cols] = folded[:, :GROUP].astype(bf16)
            wi_ref[:, cols] = folded[:, GROUP:].astype(bf16)

    @pl.when(j > 0)
    def _():
        rest_ref[...] = w_ref[...].astype(bf16)


def _prep_call(w_in):
    sq = jax.ShapeDtypeStruct((DEPTH, D, D), bf16)
    tab = pl.BlockSpec((GROUP, 2 * GROUP), lambda l, j: (0, 0))
    fold_out = pl.BlockSpec((None, D, D), lambda l, j: (l, 0, 0))
    return pl.pallas_call(
        _prep_kernel,
        out_shape=(sq, sq, jax.ShapeDtypeStruct((DEPTH, D, D_REST), bf16)),
        grid=(DEPTH, 1 + D_REST // D),
        in_specs=[pl.BlockSpec((None, D, D), lambda l, j: (l, 0, j)), tab],
        out_specs=(fold_out, fold_out,
                   pl.BlockSpec((None, D, D), lambda l, j: (l, 0, jnp.maximum(j - 1, 0)))),
        compiler_params=_params("arbitrary", "arbitrary"),
        name="weight_prep",
    )(w_in, jnp.asarray(_GROUP_TABLE))


def _modulated_norm(x, gain, shift, scale):
    ms = jnp.mean(x * x, axis=-1, keepdims=True)
    return (x * lax.rsqrt(ms + EPS) * gain) * (1.0 + scale) + shift


def _inproj_kernel(x_ref, mod_ref, g_ref, cos_ref, sin_ref, w_ref,
                   ht_ref, q_ref, k_ref, v_ref, gs_ref, af_ref, ar_ref):
    hidden = _modulated_norm(x_ref[...], g_ref[...], mod_ref[0], mod_ref[1])
    ht_ref[...] = jnp.swapaxes(hidden.reshape(TM // R, R, D), 0, 1)
    hb = hidden.astype(bf16)
    cos = cos_ref[...]
    sin = sin_ref[...]
    half = DK // 2

    def rotary(sec, o_ref, scale):
        p = jnp.dot(hb, w_ref[:, sec * D:(sec + 1) * D], preferred_element_type=f32)
        for h in range(HEADS):
            t1 = p[:, h * DK:h * DK + half]
            t2 = p[:, h * DK + half:(h + 1) * DK]
            o_ref[:, h * DK:h * DK + half] = ((t1 * cos - t2 * sin) * scale).astype(bf16)
            o_ref[:, h * DK + half:(h + 1) * DK] = ((t1 * sin + t2 * cos) * scale).astype(bf16)

    rotary(0, q_ref, DK ** -0.5)
    rotary(1, k_ref, 1.0)
    g = jnp.dot(hb, w_ref[:, 3 * D:4 * D], preferred_element_type=f32)
    gs_ref[...] = (g * jax.nn.sigmoid(g)).astype(bf16)
    af = jnp.dot(hb, w_ref[:, 4 * D:5 * D], preferred_element_type=f32)
    af_ref[...] = jax.nn.sigmoid(af).astype(bf16)
    ar = jnp.dot(hb, w_ref[:, 5 * D:6 * D], preferred_element_type=f32)
    ar_ref[...] = jax.nn.sigmoid(ar).astype(bf16)
    v_ref[...] = jnp.dot(hb, w_ref[:, 2 * D:3 * D], preferred_element_type=f32).astype(bf16)


def _inproj_call(layer, x, mod, gain, cos, sin, w_rest):
    row = pl.BlockSpec((None, TM, D), lambda b, i: (b, i, 0))
    rot = pl.BlockSpec((TM, DK // 2), lambda b, i: (i, 0))
    slab = pl.BlockSpec((None, R, TM // R, D), lambda b, i: (b, 0, i, 0))
    act = jax.ShapeDtypeStruct((BATCH, SEQ, D), bf16)
    return pl.pallas_call(
        _inproj_kernel,
        out_shape=(jax.ShapeDtypeStruct((BATCH, R, R, D), f32), act, act, act, act, act, act),
        grid=(BATCH, NT),
        in_specs=[
            row,
            pl.BlockSpec((N_MOD, None, 1, D), lambda b, i: (0, b, 0, 0)),
            pl.BlockSpec((1, D), lambda b, i: (0, 0)),
            rot, rot,
            _layer_resident((D, D_REST), layer),
        ],
        out_specs=(slab, row, row, row, row, row, row),
        compiler_params=_params("arbitrary", "arbitrary"),
        name="in_projection",
    )(x, mod, gain, cos, sin, w_rest)


def _fourier_kernel(h_ref, wr_ref, wi_ref, m_ref, tr_ref, ti_ref, sr_ref, si_ref):
    hb = h_ref[...].reshape(FFT_SLABS * R, D).astype(bf16)
    zr = jnp.dot(hb, wr_ref[...], preferred_element_type=f32).astype(bf16)
    zi = jnp.dot(hb, wi_ref[...], preferred_element_type=f32).astype(bf16)
    for s in range(FFT_SLABS):
        z = jnp.concatenate([zr[s * R:(s + 1) * R], zi[s * R:(s + 1) * R]], axis=0)
        t = jnp.dot(m_ref[s].astype(bf16), z, preferred_element_type=f32)
        sr_ref[s] = t[:R]
        si_ref[s] = t[R:]
    tr_ref[...] = jnp.swapaxes(sr_ref[...], 0, 1).astype(bf16)
    ti_ref[...] = jnp.swapaxes(si_ref[...], 0, 1).astype(bf16)


def _fourier_call(layer, ht, wr, wi):
    nsteps = R // FFT_SLABS
    slab = pl.BlockSpec((None, R, FFT_SLABS, D), lambda b, j: (b, 0, j, 0))
    t2 = jax.ShapeDtypeStruct((BATCH, R, R, D), bf16)
    return pl.pallas_call(
        _fourier_kernel,
        out_shape=(t2, t2),
        grid=(BATCH, nsteps),
        in_specs=[pl.BlockSpec((None, FFT_SLABS, R, D), lambda b, j: (b, j, 0, 0)),
                  _layer_resident((D, D), layer), _layer_resident((D, D), layer),
                  pl.BlockSpec((FFT_SLABS, 2 * R, 2 * R), lambda b, j: (j, 0, 0))],
        out_specs=(slab, slab),
        scratch_shapes=[pltpu.VMEM((FFT_SLABS, R, D), f32), pltpu.VMEM((FFT_SLABS, R, D), f32)],
        compiler_params=_params("arbitrary", "arbitrary"),
        name="fourier_projection_dft1",
    )(ht, wr, wi, jnp.asarray(_STAGE1))


def _dft2_kernel(tr_ref, ti_ref, m_ref, y_ref, s_ref):
    m = m_ref[...].astype(bf16)
    for s in range(FFT_SLABS):
        t = jnp.concatenate([tr_ref[s], ti_ref[s]], axis=0)
        s_ref[s] = jnp.dot(m, t, preferred_element_type=f32)
    y_ref[...] = jnp.swapaxes(s_ref[...], 0, 1).astype(bf16)


def _dft2_call(tr, ti):
    nsteps = R // FFT_SLABS
    tin = pl.BlockSpec((None, FFT_SLABS, R, D), lambda b, j: (b, j, 0, 0))
    yout = pl.BlockSpec((None, R, FFT_SLABS, D), lambda b, j: (b, 0, j, 0))
    return pl.pallas_call(
        _dft2_kernel,
        out_shape=jax.ShapeDtypeStruct((BATCH, R, R, D), bf16),
        grid=(BATCH, nsteps),
        in_specs=[tin, tin, _resident((R, 2 * R))],
        out_specs=yout,
        scratch_shapes=[pltpu.VMEM((FFT_SLABS, R, D), f32)],
        compiler_params=_params("arbitrary", "arbitrary"),
        name="dft_stage2",
    )(tr, ti, jnp.asarray(_STAGE2))


def _ret_kernel(q_ref, k_ref, v_ref, m_ref, dec_ref,
                o_ref, sf_ref, sb_ref, sb_all_ref, decb_ref):
    p = pl.program_id(1)
    j = pl.program_id(2)
    n_chunks = RET_ROWS // RET_CHUNK

    @pl.when(j == 0)
    def _():
        sf_ref[...] = jnp.zeros_like(sf_ref)
        sb_ref[...] = jnp.zeros_like(sb_ref)
        decb_ref[...] = dec_ref[...].astype(bf16)

    @pl.when(p == 0)
    def _():
        block = pl.num_programs(2) - 1 - j
        for c in reversed(range(n_chunks)):
            rows = slice(c * RET_CHUNK, (c + 1) * RET_CHUNK)
            for h in range(HEADS):
                cols = slice(h * DK, (h + 1) * DK)
                state = sb_ref[h]
                sb_all_ref[block * n_chunks + c, h] = state.astype(bf16)
                k_dec = k_ref[rows, cols] * decb_ref[_KDB, h]
                sb_ref[h] = state * float(_CDB[h]) + lax.dot_general(
                    k_dec, v_ref[rows, cols], _TN_DIMS, preferred_element_type=f32)

    @pl.when(p == 1)
    def _():
        for c in range(n_chunks):
            rows = slice(c * RET_CHUNK, (c + 1) * RET_CHUNK)
            for h in range(HEADS):
                cols = slice(h * DK, (h + 1) * DK)
                q = q_ref[rows, cols]
                k = k_ref[rows, cols]
                v = v_ref[rows, cols]
                scores = lax.dot_general(q, k, _NT_DIMS, preferred_element_type=f32) * m_ref[h]
                state = sf_ref[h]
                lhs = jnp.concatenate([scores.astype(bf16), q * decb_ref[_QDF, h],
                                       q * decb_ref[_QDB, h]], axis=1)
                rhs = jnp.concatenate([v, state.astype(bf16), sb_all_ref[j * n_chunks + c, h]],
                                      axis=0)
                y = jnp.dot(lhs, rhs, preferred_element_type=f32)
                k_dec = k * decb_ref[_KDF, h]
                sf_ref[h] = state * float(_CDF[h]) + lax.dot_general(
                    k_dec, v, _TN_DIMS, preferred_element_type=f32)
                o_ref[rows, cols] = y.astype(bf16)


def _ret_call(q, k, v):
    nb = SEQ // RET_ROWS

    def scan_idx(b, p, j):
        return (b, j + (1 - p) * (nb - 1 - 2 * j), 0)

    def finish_idx(b, p, j):
        return (b, p * j, 0)

    scan = pl.BlockSpec((None, RET_ROWS, D), scan_idx)
    fin = pl.BlockSpec((None, RET_ROWS, D), finish_idx)
    return pl.pallas_call(
        _ret_kernel,
        out_shape=jax.ShapeDtypeStruct((BATCH, SEQ, D), bf16),
        grid=(BATCH, 2, nb),
        in_specs=[fin, scan, scan, _resident((HEADS, RET_CHUNK, RET_CHUNK)),
                  _resident((4, HEADS, RET_CHUNK, DK))],
        out_specs=fin,
        scratch_shapes=[pltpu.VMEM((HEADS, DK, DK), f32), pltpu.VMEM((HEADS, DK, DK), f32),
                        pltpu.VMEM((SEQ // RET_CHUNK, HEADS, DK, DK), bf16),
                        pltpu.VMEM((4, HEADS, RET_CHUNK, DK), bf16)],
        compiler_params=_params("arbitrary", "arbitrary", "arbitrary"),
        name="retention",
    )(q, k, v, jnp.asarray(_RMASK), jnp.asarray(_RDEC))


def _merge_kernel(x_ref, mod_ref, yf_ref, yr_ref, gs_ref, af_ref, ar_ref, wf_ref, wr_ref, wo_ref,
                  o_ref):
    a = jnp.dot(yf_ref[...], wf_ref[...], preferred_element_type=f32)
    gated = []
    for h in range(HEADS):
        cols = slice(h * DK, (h + 1) * DK)
        y = yr_ref[:, cols].astype(f32)
        mu = jnp.mean(y, axis=-1, keepdims=True)
        yc = y - mu
        var = jnp.mean(yc * yc, axis=-1, keepdims=True)
        yn = yc * lax.rsqrt(var + EPS)
        gated.append((yn * gs_ref[:, cols].astype(f32)).astype(bf16))
    b = jnp.dot(jnp.concatenate(gated, axis=1), wr_ref[...], preferred_element_type=f32)
    merged = af_ref[...].astype(f32) * a + ar_ref[...].astype(f32) * b
    o = jnp.dot(merged.astype(bf16), wo_ref[...], preferred_element_type=f32)
    o_ref[...] = x_ref[...] + mod_ref[2] * o


def _merge_call(layer, x, mod, yf, yr, gs, af, ar, wf, wr, wo):
    row = pl.BlockSpec((None, TM, D), lambda b, i: (b, i, 0))
    return pl.pallas_call(
        _merge_kernel,
        out_shape=jax.ShapeDtypeStruct((BATCH, SEQ, D), f32),
        grid=(BATCH, NT),
        in_specs=[row, pl.BlockSpec((N_MOD, None, 1, D), lambda b, i: (0, b, 0, 0)),
                  row, row, row, row, row,
                  _layer_resident((D, D), layer), _layer_resident((D, D), layer),
                  _layer_resident((D, D), layer)],
        out_specs=row,
        compiler_params=_params("arbitrary", "arbitrary"),
        name="merge_out_projection",
    )(x, mod, yf, yr, gs, af, ar, wf, wr, wo)


def _ffn_kernel(x_ref, xp_ref, xn_ref, mod_ref, g_ref, wup_ref, cw_ref, cb_ref, wdn_ref, fg_ref,
                o_ref, act_ref, *, final):
    i = pl.program_id(1)
    rows = TM + 2 * HALO
    gain, shift, scale = g_ref[...], mod_ref[3], mod_ref[4]
    keep_prev = (i > 0).astype(f32)
    keep_next = (i < pl.num_programs(1) - 1).astype(f32)
    h = jnp.concatenate([
        _modulated_norm(xp_ref[...], gain, shift, scale) * keep_prev,
        _modulated_norm(x_ref[...], gain, shift, scale),
        _modulated_norm(xn_ref[...], gain, shift, scale) * keep_next,
    ], axis=0).astype(bf16)

    def conv(col):
        u = jnp.dot(h, wup_ref[:, col:col + FF_TILE], preferred_element_type=f32)
        w = cw_ref[:, col:col + FF_TILE]
        full = (pltpu.roll(u, 1, 0) * w[0:1] + u * w[1:2] + pltpu.roll(u, rows - 1, 0) * w[2:3]
                + cb_ref[:, col:col + FF_TILE])
        return full[HALO:HALO + TM]

    for t in range(D_FF // FF_TILE):
        a = conv(t * FF_TILE)
        b = conv(D_FF + t * FF_TILE)
        gelu = 0.5 * a * (1.0 + lax.erf(a * (2.0 ** -0.5)))
        act_ref[:, t * FF_TILE:(t + 1) * FF_TILE] = (gelu * b).astype(bf16)

    y = jnp.dot(act_ref[...], wdn_ref[...], preferred_element_type=f32)
    out = x_ref[...] + mod_ref[5] * y
    if final:
        ms = jnp.mean(out * out, axis=-1, keepdims=True)
        out = out * lax.rsqrt(ms + EPS) * fg_ref[...]
    o_ref[...] = out


def _ffn_call(layer, x, mod, gain, wup, cw, cb, wdn, final_g, final):
    tiles = TM // HALO
    row = pl.BlockSpec((None, TM, D), lambda b, i: (b, i, 0))
    prev = pl.BlockSpec((None, HALO, D), lambda b, i: (b, jnp.maximum(i * tiles - 1, 0), 0))
    nxt = pl.BlockSpec((None, HALO, D),
                       lambda b, i: (b, jnp.minimum((i + 1) * tiles, SEQ // HALO - 1), 0))
    return pl.pallas_call(
        functools.partial(_ffn_kernel, final=final),
        out_shape=jax.ShapeDtypeStruct((BATCH, SEQ, D), f32),
        grid=(BATCH, NT),
        in_specs=[row, prev, nxt,
                  pl.BlockSpec((N_MOD, None, 1, D), lambda b, i: (0, b, 0, 0)),
                  pl.BlockSpec((1, D), lambda b, i: (0, 0)),
                  _layer_resident((D, 2 * D_FF), layer), _layer_resident((3, 2 * D_FF), layer),
                  _layer_resident((1, 2 * D_FF), layer), _layer_resident((D_FF, D), layer),
                  pl.BlockSpec((1, D), lambda b, i: (0, 0))],
        out_specs=row,
        scratch_shapes=[pltpu.VMEM((TM, D_FF), bf16)],
        compiler_params=_params("arbitrary", "arbitrary"),
        name="conv_ffn_final" if final else "conv_ffn",
    )(x, x, x, mod, gain, wup, cw, cb, wdn, final_g)


def kernel(x, c, norm1_g, norm2_g, ada_w, ada_b, w_in, w_fourier, w_ret, w_out,
           ffn_up, conv_w, conv_b, ffn_down, final_g):
    assert x.shape == (BATCH, SEQ, D) and c.shape == (BATCH, D)
    mod_all = _ada_call(c, ada_w, ada_b).reshape(DEPTH, N_MOD, BATCH, 1, D)
    wfr_all, wfi_all, w_rest = _prep_call(w_in)
    cos, sin = _rotary_tables()
    final_gain = final_g.reshape(1, D)
    wf, wr, wo = w_fourier.astype(bf16), w_ret.astype(bf16), w_out.astype(bf16)
    wup, wdn = ffn_up.astype(bf16), ffn_down.astype(bf16)

    for l in range(DEPTH):
        mod = mod_all[l]
        ht, q, k, v, gs, af, ar = _inproj_call(l, x, mod, norm1_g[l].reshape(1, D), cos, sin,
                                               w_rest)
        tr, ti = _fourier_call(l, ht, wfr_all, wfi_all)
        yf = _dft2_call(tr, ti).reshape(BATCH, SEQ, D)
        yr = _ret_call(q, k, v)
        x = _merge_call(l, x, mod, yf, yr, gs, af, ar, wf, wr, wo)
        x = _ffn_call(l, x, mod, norm2_g[l].reshape(1, D), wup, conv_w,
                      conv_b.reshape(DEPTH, 1, 2 * D_FF), wdn, final_gain,
                      final=(l == DEPTH - 1))
    return x
```

```python
import functools

import numpy as np
import jax
import jax.numpy as jnp
from jax import lax
from jax.experimental import pallas as pl
from jax.experimental.pallas import tpu as pltpu

f32 = jnp.float32
bf16 = jnp.bfloat16

D = 1024
BATCH = 8
SEQ = 4096
DEPTH = 4
GROUP = 128
HEADS = 4
DK = D // HEADS
ROPE_BASE = 10000.0
D_FF = 2816
N_MOD = 6
EPS = 1e-6
D_REST = 6 * D

R = 64
TM = 512
NT = SEQ // TM
RET_CHUNK = 256
RET_ROWS = 1024
FFT_SLABS = 16
FF_TILE = 256
HALO = 8
VMEM_LIMIT = 56 * 1024 * 1024
RET_VMEM_LIMIT = 60 * 1024 * 1024

_NT_DIMS = (((1,), (1,)), ((), ()))
_TN_DIMS = (((0,), (0,)), ((), ()))


def _resident(shape):
    nd = len(shape)
    return pl.BlockSpec(shape, lambda *_: (0,) * nd, pipeline_mode=pl.Buffered(1))


def _layer_resident(shape, layer):
    nd = len(shape)
    return pl.BlockSpec((None,) + tuple(shape), lambda *_: (layer,) + (0,) * nd,
                        pipeline_mode=pl.Buffered(1))


def _params(*sem, vmem_limit=VMEM_LIMIT):
    return pltpu.CompilerParams(dimension_semantics=sem, vmem_limit_bytes=vmem_limit)


def _group_dft_tables():
    n = np.arange(GROUP)
    ang = 2.0 * np.pi * np.outer(n, n) / GROUP
    s = 1.0 / np.sqrt(GROUP)
    return (np.cos(ang) * s).astype(np.float32), (-np.sin(ang) * s).astype(np.float32)


def _stage_tables():
    k = np.arange(R)
    ang = 2.0 * np.pi * np.outer(k, k) / R
    cr = np.cos(ang) / 8.0
    ci = -np.sin(ang) / 8.0
    stage2 = np.concatenate([cr, -ci], axis=1)
    n2 = np.arange(R)[:, None, None]
    k1 = np.arange(R)[None, :, None]
    n1 = np.arange(R)[None, None, :]
    a = np.exp(-2j * np.pi * (n2 * k1 / SEQ + n1 * k1 / R)) / 8.0
    stage1 = np.concatenate([np.concatenate([a.real, -a.imag], axis=2),
                             np.concatenate([a.imag, a.real], axis=2)], axis=1)
    return stage1.astype(np.float32), stage2.astype(np.float32)


def _retention_tables():
    c = RET_CHUNK
    j = np.arange(c, dtype=np.float64)
    diff = j[:, None] - j[None, :]
    ones = np.ones((1, DK))
    mask = np.zeros((HEADS, c, c))
    qdf, qdb, kdf, kdb = (np.zeros((HEADS, c, DK)) for _ in range(4))
    cdf, cdb = np.zeros(HEADS), np.zeros(HEADS)
    for h in range(HEADS):
        lf = np.log1p(-np.exp2(-5.0 - h))
        lb = np.log1p(-np.exp2(-5.5 - h))
        mask[h] = np.where(diff >= 0, np.exp(lf * np.maximum(diff, 0.0)),
                           np.exp(lb * np.maximum(-diff, 0.0)))
        qdf[h] = np.exp(lf * (j + 1.0))[:, None] * ones
        kdf[h] = np.exp(lf * (c - 1.0 - j))[:, None] * ones
        qdb[h] = np.exp(lb * (c - j))[:, None] * ones
        kdb[h] = np.exp(lb * j)[:, None] * ones
        cdf[h] = np.exp(lf * c)
        cdb[h] = np.exp(lb * c)
    dec = np.stack([qdf, qdb, kdf, kdb])
    return mask.astype(np.float32), dec.astype(np.float32), cdf, cdb


_GROUP_TABLE = np.concatenate(_group_dft_tables(), axis=1)
_STAGE1, _STAGE2 = _stage_tables()
_RMASK, _RDEC, _CDF, _CDB = _retention_tables()
_QDF, _QDB, _KDF, _KDB = range(4)


def _rotary_tables():
    half = DK // 2
    inv_freq = ROPE_BASE ** (-jnp.arange(half, dtype=f32) / half)
    ang = jnp.arange(SEQ, dtype=f32)[:, None] * inv_freq[None, :]
    return jnp.cos(ang), jnp.sin(ang)


def _ada_kernel(c_ref, w_ref, b_ref, o_ref):
    c = c_ref[...]
    act = (c * jax.nn.sigmoid(c)).astype(bf16)
    o_ref[...] = jnp.dot(act, w_ref[...].astype(bf16), preferred_element_type=f32) + b_ref[...]


def _ada_call(c, ada_w, ada_b):
    return pl.pallas_call(
        _ada_kernel,
        out_shape=jax.ShapeDtypeStruct((DEPTH, N_MOD, BATCH, D), f32),
        grid=(DEPTH, N_MOD),
        in_specs=[
            pl.BlockSpec((BATCH, D), lambda l, j: (0, 0)),
            pl.BlockSpec((None, D, D), lambda l, j: (l, 0, j)),
            pl.BlockSpec((None, None, 1, D), lambda l, j: (l, j, 0, 0)),
        ],
        out_specs=pl.BlockSpec((None, None, BATCH, D), lambda l, j: (l, j, 0, 0)),
        compiler_params=_params("arbitrary", "arbitrary"),
        name="ada_mod",
    )(c, ada_w, ada_b.reshape(DEPTH, N_MOD, 1, D))


def _prep_kernel(w_ref, t_ref, wr_ref, wi_ref, rest_ref):
    j = pl.program_id(1)

    @pl.when(j == 0)
    def _():
        table = t_ref[...]
        t_hi = table.astype(bf16)
        t_lo = (table - t_hi.astype(f32)).astype(bf16)
        for g in range(D // GROUP):
            cols = slice(g * GROUP, (g + 1) * GROUP)
            w = w_ref[:, cols]
            hi = w.astype(bf16)
            lo = (w - hi.astype(f32)).astype(bf16)
            folded = (jnp.dot(hi, t_hi, preferred_element_type=f32)
                      + jnp.dot(lo, t_hi, preferred_element_type=f32)
                      + jnp.dot(hi, t_lo, preferred_element_type=f32))
            wr_ref[:, cols] = folded[:, :GROUP].astype(bf16)
            wi_ref[:, cols] = folded[:, GROUP:].astype(bf16)

    @pl.when(j > 0)
    def _():
        rest_ref[...] = w_ref[...].astype(bf16)


def _prep_call(w_in):
    sq = jax.ShapeDtypeStruct((DEPTH, D, D), bf16)
    tab = pl.BlockSpec((GROUP, 2 * GROUP), lambda l, j: (0, 0))
    fold_out = pl.BlockSpec((None, D, D), lambda l, j: (l, 0, 0))
    return pl.pallas_call(
        _prep_kernel,
        out_shape=(sq, sq, jax.ShapeDtypeStruct((DEPTH, D, D_REST), bf16)),
        grid=(DEPTH, 1 + D_REST // D),
        in_specs=[pl.BlockSpec((None, D, D), lambda l, j: (l, 0, j)), tab],
        out_specs=(fold_out, fold_out,
                   pl.BlockSpec((None, D, D), lambda l, j: (l, 0, jnp.maximum(j - 1, 0)))),
        compiler_params=_params("arbitrary", "arbitrary"),
        name="weight_prep",
    )(w_in, jnp.asarray(_GROUP_TABLE))


def _modulated_norm(x, gain, shift, scale):
    ms = jnp.mean(x * x, axis=-1, keepdims=True)
    return (x * lax.rsqrt(ms + EPS) * gain) * (1.0 + scale) + shift


def _inproj_kernel(x_ref, mod_ref, g_ref, cos_ref, sin_ref, w_ref,
                   ht_ref, q_ref, k_ref, v_ref, gs_ref, af_ref, ar_ref):
    hidden = _modulated_norm(x_ref[...], g_ref[...], mod_ref[0], mod_ref[1])
    ht_ref[...] = jnp.swapaxes(hidden.reshape(TM // R, R, D), 0, 1)
    hb = hidden.astype(bf16)
    cos = cos_ref[...]
    sin = sin_ref[...]
    half = DK // 2

    def rotary(sec, o_ref, scale):
        p = jnp.dot(hb, w_ref[:, sec * D:(sec + 1) * D], preferred_element_type=f32)
        for h in range(HEADS):
            t1 = p[:, h * DK:h * DK + half]
            t2 = p[:, h * DK + half:(h + 1) * DK]
            o_ref[:, h * DK:h * DK + half] = ((t1 * cos - t2 * sin) * scale).astype(bf16)
            o_ref[:, h * DK + half:(h + 1) * DK] = ((t1 * sin + t2 * cos) * scale).astype(bf16)

    rotary(0, q_ref, DK ** -0.5)
    rotary(1, k_ref, 1.0)
    g = jnp.dot(hb, w_ref[:, 3 * D:4 * D], preferred_element_type=f32)
    gs_ref[...] = (g * jax.nn.sigmoid(g)).astype(bf16)
    af = jnp.dot(hb, w_ref[:, 4 * D:5 * D], preferred_element_type=f32)
    af_ref[...] = jax.nn.sigmoid(af).astype(bf16)
    ar = jnp.dot(hb, w_ref[:, 5 * D:6 * D], preferred_element_type=f32)
    ar_ref[...] = jax.nn.sigmoid(ar).astype(bf16)
    v_ref[...] = jnp.dot(hb, w_ref[:, 2 * D:3 * D], preferred_element_type=f32).astype(bf16)


def _inproj_call(layer, x, mod, gain, cos, sin, w_rest):
    row = pl.BlockSpec((None, TM, D), lambda b, i: (b, i, 0))
    rot = pl.BlockSpec((TM, DK // 2), lambda b, i: (i, 0))
    slab = pl.BlockSpec((None, R, TM // R, D), lambda b, i: (b, 0, i, 0))
    act = jax.ShapeDtypeStruct((BATCH, SEQ, D), bf16)
    return pl.pallas_call(
        _inproj_kernel,
        out_shape=(jax.ShapeDtypeStruct((BATCH, R, R, D), f32), act, act, act, act, act, act),
        grid=(BATCH, NT),
        in_specs=[
            row,
            pl.BlockSpec((N_MOD, None, 1, D), lambda b, i: (0, b, 0, 0)),
            pl.BlockSpec((1, D), lambda b, i: (0, 0)),
            rot, rot,
            _layer_resident((D, D_REST), layer),
        ],
        out_specs=(slab, row, row, row, row, row, row),
        compiler_params=_params("arbitrary", "arbitrary"),
        name="in_projection",
    )(x, mod, gain, cos, sin, w_rest)


def _fourier_kernel(h_ref, wr_ref, wi_ref, m_ref, tr_ref, ti_ref, sr_ref, si_ref):
    hb = h_ref[...].reshape(FFT_SLABS * R, D).astype(bf16)
    zr = jnp.dot(hb, wr_ref[...], preferred_element_type=f32).astype(bf16)
    zi = jnp.dot(hb, wi_ref[...], preferred_element_type=f32).astype(bf16)
    for s in range(FFT_SLABS):
        z = jnp.concatenate([zr[s * R:(s + 1) * R], zi[s * R:(s + 1) * R]], axis=0)
        t = jnp.dot(m_ref[s].astype(bf16), z, preferred_element_type=f32)
        sr_ref[s] = t[:R]
        si_ref[s] = t[R:]
    tr_ref[...] = jnp.swapaxes(sr_ref[...], 0, 1).astype(bf16)
    ti_ref[...] = jnp.swapaxes(si_ref[...], 0, 1).astype(bf16)


def _fourier_call(layer, ht, wr, wi):
    nsteps = R // FFT_SLABS
    slab = pl.BlockSpec((None, R, FFT_SLABS, D), lambda b, j: (b, 0, j, 0))
    t2 = jax.ShapeDtypeStruct((BATCH, R, R, D), bf16)
    return pl.pallas_call(
        _fourier_kernel,
        out_shape=(t2, t2),
        grid=(BATCH, nsteps),
        in_specs=[pl.BlockSpec((None, FFT_SLABS, R, D), lambda b, j: (b, j, 0, 0)),
                  _layer_resident((D, D), layer), _layer_resident((D, D), layer),
                  pl.BlockSpec((FFT_SLABS, 2 * R, 2 * R), lambda b, j: (j, 0, 0))],
        out_specs=(slab, slab),
        scratch_shapes=[pltpu.VMEM((FFT_SLABS, R, D), f32), pltpu.VMEM((FFT_SLABS, R, D), f32)],
        compiler_params=_params("arbitrary", "arbitrary"),
        name="fourier_projection_dft1",
    )(ht, wr, wi, jnp.asarray(_STAGE1))


def _dft2_kernel(tr_ref, ti_ref, m_ref, y_ref, s_ref):
    m = m_ref[...].astype(bf16)
    for s in range(FFT_SLABS):
        t = jnp.concatenate([tr_ref[s], ti_ref[s]], axis=0)
        s_ref[s] = jnp.dot(m, t, preferred_element_type=f32)
    y_ref[...] = jnp.swapaxes(s_ref[...], 0, 1).astype(bf16)


def _dft2_call(tr, ti):
    nsteps = R // FFT_SLABS
    tin = pl.BlockSpec((None, FFT_SLABS, R, D), lambda b, j: (b, j, 0, 0))
    yout = pl.BlockSpec((None, R, FFT_SLABS, D), lambda b, j: (b, 0, j, 0))
    return pl.pallas_call(
        _dft2_kernel,
        out_shape=jax.ShapeDtypeStruct((BATCH, R, R, D), bf16),
        grid=(BATCH, nsteps),
        in_specs=[tin, tin, _resident((R, 2 * R))],
        out_specs=yout,
        scratch_shapes=[pltpu.VMEM((FFT_SLABS, R, D), f32)],
        compiler_params=_params("arbitrary", "arbitrary"),
        name="dft_stage2",
    )(tr, ti, jnp.asarray(_STAGE2))


def _ret_kernel(q_ref, k_ref, v_ref, m_ref, dec_ref,
                o_ref, sf_ref, sb_ref, sb_all_ref, decb_ref):
    p = pl.program_id(1)
    j = pl.program_id(2)
    n_chunks = RET_ROWS // RET_CHUNK

    @pl.when(j == 0)
    def _():
        sf_ref[...] = jnp.zeros_like(sf_ref)
        sb_ref[...] = jnp.zeros_like(sb_ref)
        decb_ref[...] = dec_ref[...].astype(bf16)

    @pl.when(p == 0)
    def _():
        block = pl.num_programs(2) - 1 - j
        for c in reversed(range(n_chunks)):
            chunk = block * n_chunks + c
            seq_rows = pl.ds(pl.multiple_of(chunk * RET_CHUNK, RET_CHUNK), RET_CHUNK)
            for h in range(HEADS):
                cols = slice(h * DK, (h + 1) * DK)
                state = sb_ref[h]
                sb_all_ref[chunk, h] = state.astype(bf16)
                k_dec = k_ref[seq_rows, cols] * decb_ref[_KDB, h]
                sb_ref[h] = state * float(_CDB[h]) + lax.dot_general(
                    k_dec, v_ref[seq_rows, cols], _TN_DIMS, preferred_element_type=f32)

    @pl.when(p == 1)
    def _():
        for c in range(n_chunks):
            rows = slice(c * RET_CHUNK, (c + 1) * RET_CHUNK)
            chunk = j * n_chunks + c
            seq_rows = pl.ds(pl.multiple_of(chunk * RET_CHUNK, RET_CHUNK), RET_CHUNK)
            for h in range(HEADS):
                cols = slice(h * DK, (h + 1) * DK)
                q = q_ref[rows, cols]
                k = k_ref[seq_rows, cols]
                v = v_ref[seq_rows, cols]
                scores = lax.dot_general(q, k, _NT_DIMS, preferred_element_type=f32) * m_ref[h]
                state = sf_ref[h]
                lhs = jnp.concatenate([scores.astype(bf16), q * decb_ref[_QDF, h],
                                       q * decb_ref[_QDB, h]], axis=1)
                rhs = jnp.concatenate([v, state.astype(bf16), sb_all_ref[chunk, h]], axis=0)
                y = jnp.dot(lhs, rhs, preferred_element_type=f32)
                k_dec = k * decb_ref[_KDF, h]
                sf_ref[h] = state * float(_CDF[h]) + lax.dot_general(
                    k_dec, v, _TN_DIMS, preferred_element_type=f32)
                o_ref[rows, cols] = y.astype(bf16)


def _ret_call(q, k, v):
    def finish_idx(b, p, j):
        return (b, p * j, 0)

    whole = pl.BlockSpec((None, SEQ, D), lambda b, p, j: (b, 0, 0))
    fin = pl.BlockSpec((None, RET_ROWS, D), finish_idx)
    return pl.pallas_call(
        _ret_kernel,
        out_shape=jax.ShapeDtypeStruct((BATCH, SEQ, D), bf16),
        grid=(BATCH, 2, SEQ // RET_ROWS),
        in_specs=[fin, whole, whole, _resident((HEADS, RET_CHUNK, RET_CHUNK)),
                  _resident((4, HEADS, RET_CHUNK, DK))],
        out_specs=fin,
        scratch_shapes=[pltpu.VMEM((HEADS, DK, DK), f32), pltpu.VMEM((HEADS, DK, DK), f32),
                        pltpu.VMEM((SEQ // RET_CHUNK, HEADS, DK, DK), bf16),
                        pltpu.VMEM((4, HEADS, RET_CHUNK, DK), bf16)],
        compiler_params=_params("arbitrary", "arbitrary", "arbitrary",
                                vmem_limit=RET_VMEM_LIMIT),
        name="retention",
    )(q, k, v, jnp.asarray(_RMASK), jnp.asarray(_RDEC))


def _merge_kernel(x_ref, mod_ref, yf_ref, yr_ref, gs_ref, af_ref, ar_ref, wf_ref, wr_ref, wo_ref,
                  o_ref):
    a = jnp.dot(yf_ref[...], wf_ref[...], preferred_element_type=f32)
    gated = []
    for h in range(HEADS):
        cols = slice(h * DK, (h + 1) * DK)
        y = yr_ref[:, cols].astype(f32)
        mu = jnp.mean(y, axis=-1, keepdims=True)
        yc = y - mu
        var = jnp.mean(yc * yc, axis=-1, keepdims=True)
        yn = yc * lax.rsqrt(var + EPS)
        gated.append((yn * gs_ref[:, cols].astype(f32)).astype(bf16))
    b = jnp.dot(jnp.concatenate(gated, axis=1), wr_ref[...], preferred_element_type=f32)
    merged = af_ref[...].astype(f32) * a + ar_ref[...].astype(f32) * b
    o = jnp.dot(merged.astype(bf16), wo_ref[...], preferred_element_type=f32)
    o_ref[...] = x_ref[...] + mod_ref[2] * o


def _merge_call(layer, x, mod, yf, yr, gs, af, ar, wf, wr, wo):
    row = pl.BlockSpec((None, TM, D), lambda b, i: (b, i, 0))
    return pl.pallas_call(
        _merge_kernel,
        out_shape=jax.ShapeDtypeStruct((BATCH, SEQ, D), f32),
        grid=(BATCH, NT),
        in_specs=[row, pl.BlockSpec((N_MOD, None, 1, D), lambda b, i: (0, b, 0, 0)),
                  row, row, row, row, row,
                  _layer_resident((D, D), layer), _layer_resident((D, D), layer),
                  _layer_resident((D, D), layer)],
        out_specs=row,
        compiler_params=_params("arbitrary", "arbitrary"),
        name="merge_out_projection",
    )(x, mod, yf, yr, gs, af, ar, wf, wr, wo)


def _ffn_kernel(x_ref, xp_ref, xn_ref, mod_ref, g_ref, wup_ref, cw_ref, cb_ref, wdn_ref, fg_ref,
                o_ref, act_ref, *, final):
    i = pl.program_id(1)
    rows = TM + 2 * HALO
    gain, shift, scale = g_ref[...], mod_ref[3], mod_ref[4]
    keep_prev = (i > 0).astype(f32)
    keep_next = (i < pl.num_programs(1) - 1).astype(f32)
    h = jnp.concatenate([
        _modulated_norm(xp_ref[...], gain, shift, scale) * keep_prev,
        _modulated_norm(x_ref[...], gain, shift, scale),
        _modulated_norm(xn_ref[...], gain, shift, scale) * keep_next,
    ], axis=0).astype(bf16)

    def conv(col):
        u = jnp.dot(h, wup_ref[:, col:col + FF_TILE], preferred_element_type=f32)
        w = cw_ref[:, col:col + FF_TILE]
        full = (pltpu.roll(u, 1, 0) * w[0:1] + u * w[1:2] + pltpu.roll(u, rows - 1, 0) * w[2:3]
                + cb_ref[:, col:col + FF_TILE])
        return full[HALO:HALO + TM]

    for t in range(D_FF // FF_TILE):
        a = conv(t * FF_TILE)
        b = conv(D_FF + t * FF_TILE)
        gelu = 0.5 * a * (1.0 + lax.erf(a * (2.0 ** -0.5)))
        act_ref[:, t * FF_TILE:(t + 1) * FF_TILE] = (gelu * b).astype(bf16)

    y = jnp.dot(act_ref[...], wdn_ref[...], preferred_element_type=f32)
    out = x_ref[...] + mod_ref[5] * y
    if final:
        ms = jnp.mean(out * out, axis=-1, keepdims=True)
        out = out * lax.rsqrt(ms + EPS) * fg_ref[...]
    o_ref[...] = out


def _ffn_call(layer, x, mod, gain, wup, cw, cb, wdn, final_g, final):
    tiles = TM // HALO
    row = pl.BlockSpec((None, TM, D), lambda b, i: (b, i, 0))
    prev = pl.BlockSpec((None, HALO, D), lambda b, i: (b, jnp.maximum(i * tiles - 1, 0), 0))
    nxt = pl.BlockSpec((None, HALO, D),
                       lambda b, i: (b, jnp.minimum((i + 1) * tiles, SEQ // HALO - 1), 0))
    return pl.pallas_call(
        functools.partial(_ffn_kernel, final=final),
        out_shape=jax.ShapeDtypeStruct((BATCH, SEQ, D), f32),
        grid=(BATCH, NT),
        in_specs=[row, prev, nxt,
                  pl.BlockSpec((N_MOD, None, 1, D), lambda b, i: (0, b, 0, 0)),
                  pl.BlockSpec((1, D), lambda b, i: (0, 0)),
                  _layer_resident((D, 2 * D_FF), layer), _layer_resident((3, 2 * D_FF), layer),
                  _layer_resident((1, 2 * D_FF), layer), _layer_resident((D_FF, D), layer),
                  pl.BlockSpec((1, D), lambda b, i: (0, 0))],
        out_specs=row,
        scratch_shapes=[pltpu.VMEM((TM, D_FF), bf16)],
        compiler_params=_params("arbitrary", "arbitrary"),
        name="conv_ffn_final" if final else "conv_ffn",
    )(x, x, x, mod, gain, wup, cw, cb, wdn, final_g)


def kernel(x, c, norm1_g, norm2_g, ada_w, ada_b, w_in, w_fourier, w_ret, w_out,
           ffn_up, conv_w, conv_b, ffn_down, final_g):
    assert x.shape == (BATCH, SEQ, D) and c.shape == (BATCH, D)
    mod_all = _ada_call(c, ada_w, ada_b).reshape(DEPTH, N_MOD, BATCH, 1, D)
    wfr_all, wfi_all, w_rest = _prep_call(w_in)
    cos, sin = _rotary_tables()
    final_gain = final_g.reshape(1, D)
    wf, wr, wo = w_fourier.astype(bf16), w_ret.astype(bf16), w_out.astype(bf16)
    wup, wdn = ffn_up.astype(bf16), ffn_down.astype(bf16)

    for l in range(DEPTH):
        mod = mod_all[l]
        ht, q, k, v, gs, af, ar = _inproj_call(l, x, mod, norm1_g[l].reshape(1, D), cos, sin,
                                               w_rest)
        tr, ti = _fourier_call(l, ht, wfr_all, wfi_all)
        yf = _dft2_call(tr, ti).reshape(BATCH, SEQ, D)
        yr = _ret_call(q, k, v)
        x = _merge_call(l, x, mod, yf, yr, gs, af, ar, wf, wr, wo)
        x = _ffn_call(l, x, mod, norm2_g[l].reshape(1, D), wup, conv_w,
                      conv_b.reshape(DEPTH, 1, 2 * D_FF), wdn, final_gain,
                      final=(l == DEPTH - 1))
    return x
```

```python
import functools

import numpy as np
import jax
import jax.numpy as jnp
from jax import lax
from jax.experimental import pallas as pl
from jax.experimental.pallas import tpu as pltpu

f32 = jnp.float32
bf16 = jnp.bfloat16

D = 1024
BATCH = 8
SEQ = 4096
DEPTH = 4
GROUP = 128
HEADS = 4
DK = D // HEADS
ROPE_BASE = 10000.0
D_FF = 2816
N_MOD = 6
EPS = 1e-6
D_REST = 6 * D

R = 64
TM = 1024
NT = SEQ // TM
RET_CHUNK = 256
RET_ROWS = 1024
FFT_SLABS = 16
FF_TILE = 256
HALO = 8
VMEM_LIMIT = 56 * 1024 * 1024

_NT_DIMS = (((1,), (1,)), ((), ()))
_TN_DIMS = (((0,), (0,)), ((), ()))


def _resident(shape):
    nd = len(shape)
    return pl.BlockSpec(shape, lambda *_: (0,) * nd, pipeline_mode=pl.Buffered(1))


def _layer_resident(shape, layer):
    nd = len(shape)
    return pl.BlockSpec((None,) + tuple(shape), lambda *_: (layer,) + (0,) * nd,
                        pipeline_mode=pl.Buffered(1))


def _params(*sem):
    return pltpu.CompilerParams(dimension_semantics=sem, vmem_limit_bytes=VMEM_LIMIT)


def _group_dft_tables():
    n = np.arange(GROUP)
    ang = 2.0 * np.pi * np.outer(n, n) / GROUP
    s = 1.0 / np.sqrt(GROUP)
    return (np.cos(ang) * s).astype(np.float32), (-np.sin(ang) * s).astype(np.float32)


def _stage_tables():
    k = np.arange(R)
    ang = 2.0 * np.pi * np.outer(k, k) / R
    cr = np.cos(ang) / 8.0
    ci = -np.sin(ang) / 8.0
    stage2 = np.concatenate([cr, -ci], axis=1)
    n2 = np.arange(R)[:, None, None]
    k1 = np.arange(R)[None, :, None]
    n1 = np.arange(R)[None, None, :]
    a = np.exp(-2j * np.pi * (n2 * k1 / SEQ + n1 * k1 / R)) / 8.0
    stage1 = np.concatenate([np.concatenate([a.real, -a.imag], axis=2),
                             np.concatenate([a.imag, a.real], axis=2)], axis=1)
    return stage1.astype(np.float32), stage2.astype(np.float32)


def _retention_tables():
    c = RET_CHUNK
    j = np.arange(c, dtype=np.float64)
    diff = j[:, None] - j[None, :]
    ones = np.ones((1, DK))
    mask = np.zeros((HEADS, c, c))
    qdf, qdb, kdf, kdb = (np.zeros((HEADS, c, DK)) for _ in range(4))
    cdf, cdb = np.zeros(HEADS), np.zeros(HEADS)
    for h in range(HEADS):
        lf = np.log1p(-np.exp2(-5.0 - h))
        lb = np.log1p(-np.exp2(-5.5 - h))
        mask[h] = np.where(diff >= 0, np.exp(lf * np.maximum(diff, 0.0)),
                           np.exp(lb * np.maximum(-diff, 0.0)))
        qdf[h] = np.exp(lf * (j + 1.0))[:, None] * ones
        kdf[h] = np.exp(lf * (c - 1.0 - j))[:, None] * ones
        qdb[h] = np.exp(lb * (c - j))[:, None] * ones
        kdb[h] = np.exp(lb * j)[:, None] * ones
        cdf[h] = np.exp(lf * c)
        cdb[h] = np.exp(lb * c)
    dec = np.stack([qdf, qdb, kdf, kdb])
    return mask.astype(np.float32), dec.astype(np.float32), cdf, cdb


_GROUP_TABLE = np.concatenate(_group_dft_tables(), axis=1)
_STAGE1, _STAGE2 = _stage_tables()
_RMASK, _RDEC, _CDF, _CDB = _retention_tables()
_QDF, _QDB, _KDF, _KDB = range(4)


def _rotary_tables():
    half = DK // 2
    inv_freq = ROPE_BASE ** (-jnp.arange(half, dtype=f32) / half)
    ang = jnp.arange(SEQ, dtype=f32)[:, None] * inv_freq[None, :]
    return jnp.cos(ang), jnp.sin(ang)


def _ada_kernel(c_ref, w_ref, b_ref, o_ref):
    c = c_ref[...]
    act = (c * jax.nn.sigmoid(c)).astype(bf16)
    o_ref[...] = jnp.dot(act, w_ref[...].astype(bf16), preferred_element_type=f32) + b_ref[...]


def _ada_call(c, ada_w, ada_b):
    return pl.pallas_call(
        _ada_kernel,
        out_shape=jax.ShapeDtypeStruct((DEPTH, N_MOD, BATCH, D), f32),
        grid=(DEPTH, N_MOD),
        in_specs=[
            pl.BlockSpec((BATCH, D), lambda l, j: (0, 0)),
            pl.BlockSpec((None, D, D), lambda l, j: (l, 0, j)),
            pl.BlockSpec((None, None, 1, D), lambda l, j: (l, j, 0, 0)),
        ],
        out_specs=pl.BlockSpec((None, None, BATCH, D), lambda l, j: (l, j, 0, 0)),
        compiler_params=_params("arbitrary", "arbitrary"),
        name="ada_mod",
    )(c, ada_w, ada_b.reshape(DEPTH, N_MOD, 1, D))


def _prep_kernel(w_ref, t_ref, wr_ref, wi_ref, rest_ref):
    j = pl.program_id(1)

    @pl.when(j == 0)
    def _():
        table = t_ref[...]
        t_hi = table.astype(bf16)
        t_lo = (table - t_hi.astype(f32)).astype(bf16)
        for g in range(D // GROUP):
            cols = slice(g * GROUP, (g + 1) * GROUP)
            w = w_ref[:, cols]
            hi = w.astype(bf16)
            lo = (w - hi.astype(f32)).astype(bf16)
            folded = (jnp.dot(hi, t_hi, preferred_element_type=f32)
                      + jnp.dot(lo, t_hi, preferred_element_type=f32)
                      + jnp.dot(hi, t_lo, preferred_element_type=f32))
            wr_ref[:, cols] = folded[:, :GROUP].astype(bf16)
            wi_ref[:, cols] = folded[:, GROUP:].astype(bf16)

    @pl.when(j > 0)
    def _():
        rest_ref[...] = w_ref[...].astype(bf16)


def _prep_call(w_in):
    sq = jax.ShapeDtypeStruct((DEPTH, D, D), bf16)
    tab = pl.BlockSpec((GROUP, 2 * GROUP), lambda l, j: (0, 0))
    fold_out = pl.BlockSpec((None, D, D), lambda l, j: (l, 0, 0))
    return pl.pallas_call(
        _prep_kernel,
        out_shape=(sq, sq, jax.ShapeDtypeStruct((DEPTH, D, D_REST), bf16)),
        grid=(DEPTH, 1 + D_REST // D),
        in_specs=[pl.BlockSpec((None, D, D), lambda l, j: (l, 0, j)), tab],
        out_specs=(fold_out, fold_out,
                   pl.BlockSpec((None, D, D), lambda l, j: (l, 0, jnp.maximum(j - 1, 0)))),
        compiler_params=_params("arbitrary", "arbitrary"),
        name="weight_prep",
    )(w_in, jnp.asarray(_GROUP_TABLE))


def _modulated_norm(x, gain, shift, scale):
    ms = jnp.mean(x * x, axis=-1, keepdims=True)
    return (x * lax.rsqrt(ms + EPS) * gain) * (1.0 + scale) + shift


def _inproj_kernel(x_ref, mod_ref, g_ref, cos_ref, sin_ref, w_ref,
                   q_ref, k_ref, v_ref, gs_ref, af_ref, ar_ref):
    hb = _modulated_norm(x_ref[...], g_ref[...], mod_ref[0], mod_ref[1]).astype(bf16)
    cos = cos_ref[...]
    sin = sin_ref[...]
    half = DK // 2

    def rotary(sec, o_ref, scale):
        p = jnp.dot(hb, w_ref[:, sec * D:(sec + 1) * D], preferred_element_type=f32)
        for h in range(HEADS):
            t1 = p[:, h * DK:h * DK + half]
            t2 = p[:, h * DK + half:(h + 1) * DK]
            o_ref[:, h * DK:h * DK + half] = ((t1 * cos - t2 * sin) * scale).astype(bf16)
            o_ref[:, h * DK + half:(h + 1) * DK] = ((t1 * sin + t2 * cos) * scale).astype(bf16)

    rotary(0, q_ref, DK ** -0.5)
    rotary(1, k_ref, 1.0)
    g = jnp.dot(hb, w_ref[:, 3 * D:4 * D], preferred_element_type=f32)
    gs_ref[...] = (g * jax.nn.sigmoid(g)).astype(bf16)
    af = jnp.dot(hb, w_ref[:, 4 * D:5 * D], preferred_element_type=f32)
    af_ref[...] = jax.nn.sigmoid(af).astype(bf16)
    ar = jnp.dot(hb, w_ref[:, 5 * D:6 * D], preferred_element_type=f32)
    ar_ref[...] = jax.nn.sigmoid(ar).astype(bf16)
    v_ref[...] = jnp.dot(hb, w_ref[:, 2 * D:3 * D], preferred_element_type=f32).astype(bf16)


def _inproj_call(layer, x, mod, gain, cos, sin, w_rest):
    row = pl.BlockSpec((None, TM, D), lambda b, i: (b, i, 0))
    rot = pl.BlockSpec((TM, DK // 2), lambda b, i: (i, 0))
    act = jax.ShapeDtypeStruct((BATCH, SEQ, D), bf16)
    return pl.pallas_call(
        _inproj_kernel,
        out_shape=(act, act, act, act, act, act),
        grid=(BATCH, NT),
        in_specs=[
            row,
            pl.BlockSpec((N_MOD, None, 1, D), lambda b, i: (0, b, 0, 0)),
            pl.BlockSpec((1, D), lambda b, i: (0, 0)),
            rot, rot,
            _layer_resident((D, D_REST), layer),
        ],
        out_specs=(row, row, row, row, row, row),
        compiler_params=_params("arbitrary", "arbitrary"),
        name="in_projection",
    )(x, mod, gain, cos, sin, w_rest)


def _fourier_kernel(x_ref, mod_ref, g_ref, wr_ref, wi_ref, m_ref,
                    tr_ref, ti_ref, sr_ref, si_ref):
    x = jnp.swapaxes(x_ref[...], 0, 1).reshape(FFT_SLABS * R, D)
    hb = _modulated_norm(x, g_ref[...], mod_ref[0], mod_ref[1]).astype(bf16)
    zr = jnp.dot(hb, wr_ref[...], preferred_element_type=f32).astype(bf16)
    zi = jnp.dot(hb, wi_ref[...], preferred_element_type=f32).astype(bf16)
    for s in range(FFT_SLABS):
        z = jnp.concatenate([zr[s * R:(s + 1) * R], zi[s * R:(s + 1) * R]], axis=0)
        t = jnp.dot(m_ref[s].astype(bf16), z, preferred_element_type=f32)
        sr_ref[s] = t[:R]
        si_ref[s] = t[R:]
    tr_ref[...] = jnp.swapaxes(sr_ref[...], 0, 1).astype(bf16)
    ti_ref[...] = jnp.swapaxes(si_ref[...], 0, 1).astype(bf16)


def _fourier_call(layer, x, mod, gain, wr, wi):
    nsteps = R // FFT_SLABS
    slab = pl.BlockSpec((None, R, FFT_SLABS, D), lambda b, j: (b, 0, j, 0))
    t2 = jax.ShapeDtypeStruct((BATCH, R, R, D), bf16)
    return pl.pallas_call(
        _fourier_kernel,
        out_shape=(t2, t2),
        grid=(BATCH, nsteps),
        in_specs=[slab,
                  pl.BlockSpec((N_MOD, None, 1, D), lambda b, j: (0, b, 0, 0)),
                  pl.BlockSpec((1, D), lambda b, j: (0, 0)),
                  _layer_resident((D, D), layer), _layer_resident((D, D), layer),
                  pl.BlockSpec((FFT_SLABS, 2 * R, 2 * R), lambda b, j: (j, 0, 0))],
        out_specs=(slab, slab),
        scratch_shapes=[pltpu.VMEM((FFT_SLABS, R, D), f32), pltpu.VMEM((FFT_SLABS, R, D), f32)],
        compiler_params=_params("arbitrary", "arbitrary"),
        name="fourier_projection_dft1",
    )(x.reshape(BATCH, R, R, D), mod, gain, wr, wi, jnp.asarray(_STAGE1))


def _dft2_kernel(tr_ref, ti_ref, m_ref, y_ref, s_ref):
    m = m_ref[...].astype(bf16)
    for s in range(FFT_SLABS):
        t = jnp.concatenate([tr_ref[s], ti_ref[s]], axis=0)
        s_ref[s] = jnp.dot(m, t, preferred_element_type=f32)
    y_ref[...] = jnp.swapaxes(s_ref[...], 0, 1).astype(bf16)


def _dft2_call(tr, ti):
    nsteps = R // FFT_SLABS
    tin = pl.BlockSpec((None, FFT_SLABS, R, D), lambda b, j: (b, j, 0, 0))
    yout = pl.BlockSpec((None, R, FFT_SLABS, D), lambda b, j: (b, 0, j, 0))
    return pl.pallas_call(
        _dft2_kernel,
        out_shape=jax.ShapeDtypeStruct((BATCH, R, R, D), bf16),
        grid=(BATCH, nsteps),
        in_specs=[tin, tin, _resident((R, 2 * R))],
        out_specs=yout,
        scratch_shapes=[pltpu.VMEM((FFT_SLABS, R, D), f32)],
        compiler_params=_params("arbitrary", "arbitrary"),
        name="dft_stage2",
    )(tr, ti, jnp.asarray(_STAGE2))


def _ret_kernel(q_ref, k_ref, v_ref, m_ref, dec_ref,
                o_ref, sf_ref, sb_ref, sb_all_ref, decb_ref):
    p = pl.program_id(1)
    j = pl.program_id(2)
    n_chunks = RET_ROWS // RET_CHUNK

    @pl.when(j == 0)
    def _():
        sf_ref[...] = jnp.zeros_like(sf_ref)
        sb_ref[...] = jnp.zeros_like(sb_ref)
        decb_ref[...] = dec_ref[...].astype(bf16)

    @pl.when(p == 0)
    def _():
        block = pl.num_programs(2) - 1 - j
        for c in reversed(range(n_chunks)):
            rows = slice(c * RET_CHUNK, (c + 1) * RET_CHUNK)
            for h in range(HEADS):
                cols = slice(h * DK, (h + 1) * DK)
                state = sb_ref[h]
                sb_all_ref[block * n_chunks + c, h] = state.astype(bf16)
                k_dec = k_ref[rows, cols] * decb_ref[_KDB, h]
                sb_ref[h] = state * float(_CDB[h]) + lax.dot_general(
                    k_dec, v_ref[rows, cols], _TN_DIMS, preferred_element_type=f32)

    @pl.when(p == 1)
    def _():
        for c in range(n_chunks):
            rows = slice(c * RET_CHUNK, (c + 1) * RET_CHUNK)
            for h in range(HEADS):
                cols = slice(h * DK, (h + 1) * DK)
                q = q_ref[rows, cols]
                k = k_ref[rows, cols]
                v = v_ref[rows, cols]
                scores = lax.dot_general(q, k, _NT_DIMS, preferred_element_type=f32) * m_ref[h]
                state = sf_ref[h]
                lhs = jnp.concatenate([scores.astype(bf16), q * decb_ref[_QDF, h],
                                       q * decb_ref[_QDB, h]], axis=1)
                rhs = jnp.concatenate([v, state.astype(bf16), sb_all_ref[j * n_chunks + c, h]],
                                      axis=0)
                y = jnp.dot(lhs, rhs, preferred_element_type=f32)
                k_dec = k * decb_ref[_KDF, h]
                sf_ref[h] = state * float(_CDF[h]) + lax.dot_general(
                    k_dec, v, _TN_DIMS, preferred_element_type=f32)
                o_ref[rows, cols] = y.astype(bf16)


def _ret_call(q, k, v):
    nb = SEQ // RET_ROWS

    def scan_idx(b, p, j):
        return (b, j + (1 - p) * (nb - 1 - 2 * j), 0)

    def finish_idx(b, p, j):
        return (b, p * j, 0)

    scan = pl.BlockSpec((None, RET_ROWS, D), scan_idx)
    fin = pl.BlockSpec((None, RET_ROWS, D), finish_idx)
    return pl.pallas_call(
        _ret_kernel,
        out_shape=jax.ShapeDtypeStruct((BATCH, SEQ, D), bf16),
        grid=(BATCH, 2, nb),
        in_specs=[fin, scan, scan, _resident((HEADS, RET_CHUNK, RET_CHUNK)),
                  _resident((4, HEADS, RET_CHUNK, DK))],
        out_specs=fin,
        scratch_shapes=[pltpu.VMEM((HEADS, DK, DK), f32), pltpu.VMEM((HEADS, DK, DK), f32),
                        pltpu.VMEM((SEQ // RET_CHUNK, HEADS, DK, DK), bf16),
                        pltpu.VMEM((4, HEADS, RET_CHUNK, DK), bf16)],
        compiler_params=_params("arbitrary", "arbitrary", "arbitrary"),
        name="retention",
    )(q, k, v, jnp.asarray(_RMASK), jnp.asarray(_RDEC))


def _merge_kernel(x_ref, mod_ref, yf_ref, yr_ref, gs_ref, af_ref, ar_ref, wf_ref, wr_ref, wo_ref,
                  o_ref):
    a = jnp.dot(yf_ref[...], wf_ref[...], preferred_element_type=f32)
    gated = []
    for h in range(HEADS):
        cols = slice(h * DK, (h + 1) * DK)
        y = yr_ref[:, cols].astype(f32)
        mu = jnp.mean(y, axis=-1, keepdims=True)
        yc = y - mu
        var = jnp.mean(yc * yc, axis=-1, keepdims=True)
        yn = yc * lax.rsqrt(var + EPS)
        gated.append((yn * gs_ref[:, cols].astype(f32)).astype(bf16))
    b = jnp.dot(jnp.concatenate(gated, axis=1), wr_ref[...], preferred_element_type=f32)
    merged = af_ref[...].astype(f32) * a + ar_ref[...].astype(f32) * b
    o = jnp.dot(merged.astype(bf16), wo_ref[...], preferred_element_type=f32)
    o_ref[...] = x_ref[...] + mod_ref[2] * o


def _merge_call(layer, x, mod, yf, yr, gs, af, ar, wf, wr, wo):
    row = pl.BlockSpec((None, TM, D), lambda b, i: (b, i, 0))
    return pl.pallas_call(
        _merge_kernel,
        out_shape=jax.ShapeDtypeStruct((BATCH, SEQ, D), f32),
        grid=(BATCH, NT),
        in_specs=[row, pl.BlockSpec((N_MOD, None, 1, D), lambda b, i: (0, b, 0, 0)),
                  row, row, row, row, row,
                  _layer_resident((D, D), layer), _layer_resident((D, D), layer),
                  _layer_resident((D, D), layer)],
        out_specs=row,
        compiler_params=_params("arbitrary", "arbitrary"),
        name="merge_out_projection",
    )(x, mod, yf, yr, gs, af, ar, wf, wr, wo)


def _ffn_kernel(x_ref, xp_ref, xn_ref, mod_ref, g_ref, wup_ref, cw_ref, cb_ref, wdn_ref, fg_ref,
                o_ref, act_ref, *, final):
    i = pl.program_id(1)
    rows = TM + 2 * HALO
    gain, shift, scale = g_ref[...], mod_ref[3], mod_ref[4]
    keep_prev = (i > 0).astype(f32)
    keep_next = (i < pl.num_programs(1) - 1).astype(f32)
    h = jnp.concatenate([
        _modulated_norm(xp_ref[...], gain, shift, scale) * keep_prev,
        _modulated_norm(x_ref[...], gain, shift, scale),
        _modulated_norm(xn_ref[...], gain, shift, scale) * keep_next,
    ], axis=0).astype(bf16)

    def conv(col):
        u = jnp.dot(h, wup_ref[:, col:col + FF_TILE], preferred_element_type=f32)
        w = cw_ref[:, col:col + FF_TILE]
        full = (pltpu.roll(u, 1, 0) * w[0:1] + u * w[1:2] + pltpu.roll(u, rows - 1, 0) * w[2:3]
                + cb_ref[:, col:col + FF_TILE])
        return full[HALO:HALO + TM]

    for t in range(D_FF // FF_TILE):
        a = conv(t * FF_TILE)
        b = conv(D_FF + t * FF_TILE)
        gelu = 0.5 * a * (1.0 + lax.erf(a * (2.0 ** -0.5)))
        act_ref[:, t * FF_TILE:(t + 1) * FF_TILE] = (gelu * b).astype(bf16)

    y = jnp.dot(act_ref[...], wdn_ref[...], preferred_element_type=f32)
    out = x_ref[...] + mod_ref[5] * y
    if final:
        ms = jnp.mean(out * out, axis=-1, keepdims=True)
        out = out * lax.rsqrt(ms + EPS) * fg_ref[...]
    o_ref[...] = out


def _ffn_call(layer, x, mod, gain, wup, cw, cb, wdn, final_g, final):
    tiles = TM // HALO
    row = pl.BlockSpec((None, TM, D), lambda b, i: (b, i, 0))
    prev = pl.BlockSpec((None, HALO, D), lambda b, i: (b, jnp.maximum(i * tiles - 1, 0), 0))
    nxt = pl.BlockSpec((None, HALO, D),
                       lambda b, i: (b, jnp.minimum((i + 1) * tiles, SEQ // HALO - 1), 0))
    return pl.pallas_call(
        functools.partial(_ffn_kernel, final=final),
        out_shape=jax.ShapeDtypeStruct((BATCH, SEQ, D), f32),
        grid=(BATCH, NT),
        in_specs=[row, prev, nxt,
                  pl.BlockSpec((N_MOD, None, 1, D), lambda b, i: (0, b, 0, 0)),
                  pl.BlockSpec((1, D), lambda b, i: (0, 0)),
                  _layer_resident((D, 2 * D_FF), layer), _layer_resident((3, 2 * D_FF), layer),
                  _layer_resident((1, 2 * D_FF), layer), _layer_resident((D_FF, D), layer),
                  pl.BlockSpec((1, D), lambda b, i: (0, 0))],
        out_specs=row,
        scratch_shapes=[pltpu.VMEM((TM, D_FF), bf16)],
        compiler_params=_params("arbitrary", "arbitrary"),
        name="conv_ffn_final" if final else "conv_ffn",
    )(x, x, x, mod, gain, wup, cw, cb, wdn, final_g)


def kernel(x, c, norm1_g, norm2_g, ada_w, ada_b, w_in, w_fourier, w_ret, w_out,
           ffn_up, conv_w, conv_b, ffn_down, final_g):
    assert x.shape == (BATCH, SEQ, D) and c.shape == (BATCH, D)
    mod_all = _ada_call(c, ada_w, ada_b).reshape(DEPTH, N_MOD, BATCH, 1, D)
    wfr_all, wfi_all, w_rest = _prep_call(w_in)
    cos, sin = _rotary_tables()
    final_gain = final_g.reshape(1, D)
    wf, wr, wo = w_fourier.astype(bf16), w_ret.astype(bf16), w_out.astype(bf16)
    wup, wdn = ffn_up.astype(bf16), ffn_down.astype(bf16)

    for l in range(DEPTH):
        mod = mod_all[l]
        gain1 = norm1_g[l].reshape(1, D)
        q, k, v, gs, af, ar = _inproj_call(l, x, mod, gain1, cos, sin, w_rest)
        tr, ti = _fourier_call(l, x, mod, gain1, wfr_all, wfi_all)
        yf = _dft2_call(tr, ti).reshape(BATCH, SEQ, D)
        yr = _ret_call(q, k, v)
        x = _merge_call(l, x, mod, yf, yr, gs, af, ar, wf, wr, wo)
        x = _ffn_call(l, x, mod, norm2_g[l].reshape(1, D), wup, conv_w,
                      conv_b.reshape(DEPTH, 1, 2 * D_FF), wdn, final_gain,
                      final=(l == DEPTH - 1))
    return x
```

```python
import functools

import numpy as np
import jax
import jax.numpy as jnp
from jax import lax
from jax.experimental import pallas as pl
from jax.experimental.pallas import tpu as pltpu

f32 = jnp.float32
bf16 = jnp.bfloat16

D = 1024
BATCH = 8
SEQ = 4096
DEPTH = 4
GROUP = 128
HEADS = 4
DK = D // HEADS
ROPE_BASE = 10000.0
D_FF = 2816
N_MOD = 6
EPS = 1e-6
D_REST = 6 * D

R = 64
TM = 1024
NT = SEQ // TM
RET_CHUNK = 256
RET_ROWS = 2048
FFT_SLABS = 16
DFT2_SLABS = 32
FF_TILE = 256
HALO = 8
VMEM_LIMIT = 56 * 1024 * 1024

_NT_DIMS = (((1,), (1,)), ((), ()))
_TN_DIMS = (((0,), (0,)), ((), ()))


def _resident(shape):
    nd = len(shape)
    return pl.BlockSpec(shape, lambda *_: (0,) * nd, pipeline_mode=pl.Buffered(1))


def _layer_resident(shape, layer):
    nd = len(shape)
    return pl.BlockSpec((None,) + tuple(shape), lambda *_: (layer,) + (0,) * nd,
                        pipeline_mode=pl.Buffered(1))


def _params(*sem):
    return pltpu.CompilerParams(dimension_semantics=sem, vmem_limit_bytes=VMEM_LIMIT)


def _group_dft_tables():
    n = np.arange(GROUP)
    ang = 2.0 * np.pi * np.outer(n, n) / GROUP
    s = 1.0 / np.sqrt(GROUP)
    return (np.cos(ang) * s).astype(np.float32), (-np.sin(ang) * s).astype(np.float32)


def _stage_tables():
    k = np.arange(R)
    ang = 2.0 * np.pi * np.outer(k, k) / R
    cr = np.cos(ang) / 8.0
    ci = -np.sin(ang) / 8.0
    stage2 = np.concatenate([cr, -ci], axis=1)
    n2 = np.arange(R)[:, None, None]
    k1 = np.arange(R)[None, :, None]
    n1 = np.arange(R)[None, None, :]
    a = np.exp(-2j * np.pi * (n2 * k1 / SEQ + n1 * k1 / R)) / 8.0
    stage1 = np.concatenate([np.concatenate([a.real, -a.imag], axis=2),
                             np.concatenate([a.imag, a.real], axis=2)], axis=1)
    return stage1.astype(np.float32), stage2.astype(np.float32)


def _retention_tables():
    c = RET_CHUNK
    j = np.arange(c, dtype=np.float64)
    diff = j[:, None] - j[None, :]
    ones = np.ones((1, DK))
    mask = np.zeros((HEADS, c, c))
    qdf, qdb, kdf, kdb = (np.zeros((HEADS, c, DK)) for _ in range(4))
    cdf, cdb = np.zeros(HEADS), np.zeros(HEADS)
    for h in range(HEADS):
        lf = np.log1p(-np.exp2(-5.0 - h))
        lb = np.log1p(-np.exp2(-5.5 - h))
        mask[h] = np.where(diff >= 0, np.exp(lf * np.maximum(diff, 0.0)),
                           np.exp(lb * np.maximum(-diff, 0.0)))
        qdf[h] = np.exp(lf * (j + 1.0))[:, None] * ones
        kdf[h] = np.exp(lf * (c - 1.0 - j))[:, None] * ones
        qdb[h] = np.exp(lb * (c - j))[:, None] * ones
        kdb[h] = np.exp(lb * j)[:, None] * ones
        cdf[h] = np.exp(lf * c)
        cdb[h] = np.exp(lb * c)
    dec = np.stack([qdf, qdb, kdf, kdb])
    return mask.astype(np.float32), dec.astype(np.float32), cdf, cdb


_GROUP_TABLE = np.concatenate(_group_dft_tables(), axis=1)
_STAGE1, _STAGE2 = _stage_tables()
_RMASK, _RDEC, _CDF, _CDB = _retention_tables()
_QDF, _QDB, _KDF, _KDB = range(4)


def _rotary_tables():
    half = DK // 2
    inv_freq = ROPE_BASE ** (-jnp.arange(half, dtype=f32) / half)
    ang = jnp.arange(SEQ, dtype=f32)[:, None] * inv_freq[None, :]
    return jnp.cos(ang), jnp.sin(ang)


def _ada_kernel(c_ref, w_ref, b_ref, o_ref):
    c = c_ref[...]
    act = (c * jax.nn.sigmoid(c)).astype(bf16)
    o_ref[...] = jnp.dot(act, w_ref[...].astype(bf16), preferred_element_type=f32) + b_ref[...]


def _ada_call(c, ada_w, ada_b):
    return pl.pallas_call(
        _ada_kernel,
        out_shape=jax.ShapeDtypeStruct((DEPTH, N_MOD, BATCH, D), f32),
        grid=(DEPTH, N_MOD),
        in_specs=[
            pl.BlockSpec((BATCH, D), lambda l, j: (0, 0)),
            pl.BlockSpec((None, D, D), lambda l, j: (l, 0, j)),
            pl.BlockSpec((None, None, 1, D), lambda l, j: (l, j, 0, 0)),
        ],
        out_specs=pl.BlockSpec((None, None, BATCH, D), lambda l, j: (l, j, 0, 0)),
        compiler_params=_params("arbitrary", "arbitrary"),
        name="ada_mod",
    )(c, ada_w, ada_b.reshape(DEPTH, N_MOD, 1, D))


def _prep_kernel(w_ref, t_ref, wr_ref, wi_ref, rest_ref):
    j = pl.program_id(1)

    @pl.when(j == 0)
    def _():
        table = t_ref[...]
        t_hi = table.astype(bf16)
        t_lo = (table - t_hi.astype(f32)).astype(bf16)
        for g in range(D // GROUP):
            cols = slice(g * GROUP, (g + 1) * GROUP)
            w = w_ref[:, cols]
            hi = w.astype(bf16)
            lo = (w - hi.astype(f32)).astype(bf16)
            folded = (jnp.dot(hi, t_hi, preferred_element_type=f32)
                      + jnp.dot(lo, t_hi, preferred_element_type=f32)
                      + jnp.dot(hi, t_lo, preferred_element_type=f32))
            wr_ref[:, cols] = folded[:, :GROUP].astype(bf16)
            wi_ref[:, cols] = folded[:, GROUP:].astype(bf16)

    @pl.when(j > 0)
    def _():
        rest_ref[...] = w_ref[...].astype(bf16)


def _prep_call(w_in):
    sq = jax.ShapeDtypeStruct((DEPTH, D, D), bf16)
    tab = pl.BlockSpec((GROUP, 2 * GROUP), lambda l, j: (0, 0))
    fold_out = pl.BlockSpec((None, D, D), lambda l, j: (l, 0, 0))
    return pl.pallas_call(
        _prep_kernel,
        out_shape=(sq, sq, jax.ShapeDtypeStruct((DEPTH, D, D_REST), bf16)),
        grid=(DEPTH, 1 + D_REST // D),
        in_specs=[pl.BlockSpec((None, D, D), lambda l, j: (l, 0, j)), tab],
        out_specs=(fold_out, fold_out,
                   pl.BlockSpec((None, D, D), lambda l, j: (l, 0, jnp.maximum(j - 1, 0)))),
        compiler_params=_params("arbitrary", "arbitrary"),
        name="weight_prep",
    )(w_in, jnp.asarray(_GROUP_TABLE))


def _modulated_norm(x, gain, shift, scale):
    ms = jnp.mean(x * x, axis=-1, keepdims=True)
    return (x * lax.rsqrt(ms + EPS) * gain) * (1.0 + scale) + shift


def _inproj_kernel(x_ref, mod_ref, g_ref, cos_ref, sin_ref, w_ref,
                   q_ref, k_ref, v_ref, gs_ref, af_ref, ar_ref):
    hb = _modulated_norm(x_ref[...], g_ref[...], mod_ref[0], mod_ref[1]).astype(bf16)
    cos = cos_ref[...]
    sin = sin_ref[...]
    half = DK // 2

    def rotary(sec, o_ref, scale):
        p = jnp.dot(hb, w_ref[:, sec * D:(sec + 1) * D], preferred_element_type=f32)
        for h in range(HEADS):
            t1 = p[:, h * DK:h * DK + half]
            t2 = p[:, h * DK + half:(h + 1) * DK]
            o_ref[:, h * DK:h * DK + half] = ((t1 * cos - t2 * sin) * scale).astype(bf16)
            o_ref[:, h * DK + half:(h + 1) * DK] = ((t1 * sin + t2 * cos) * scale).astype(bf16)

    rotary(0, q_ref, DK ** -0.5)
    rotary(1, k_ref, 1.0)
    g = jnp.dot(hb, w_ref[:, 3 * D:4 * D], preferred_element_type=f32)
    gs_ref[...] = (g * jax.nn.sigmoid(g)).astype(bf16)
    af = jnp.dot(hb, w_ref[:, 4 * D:5 * D], preferred_element_type=f32)
    af_ref[...] = jax.nn.sigmoid(af).astype(bf16)
    ar = jnp.dot(hb, w_ref[:, 5 * D:6 * D], preferred_element_type=f32)
    ar_ref[...] = jax.nn.sigmoid(ar).astype(bf16)
    v_ref[...] = jnp.dot(hb, w_ref[:, 2 * D:3 * D], preferred_element_type=f32).astype(bf16)


def _inproj_call(layer, x, mod, gain, cos, sin, w_rest):
    row = pl.BlockSpec((None, TM, D), lambda b, i: (b, i, 0))
    rot = pl.BlockSpec((TM, DK // 2), lambda b, i: (i, 0))
    act = jax.ShapeDtypeStruct((BATCH, SEQ, D), bf16)
    return pl.pallas_call(
        _inproj_kernel,
        out_shape=(act, act, act, act, act, act),
        grid=(BATCH, NT),
        in_specs=[
            row,
            pl.BlockSpec((N_MOD, None, 1, D), lambda b, i: (0, b, 0, 0)),
            pl.BlockSpec((1, D), lambda b, i: (0, 0)),
            rot, rot,
            _layer_resident((D, D_REST), layer),
        ],
        out_specs=(row, row, row, row, row, row),
        compiler_params=_params("arbitrary", "arbitrary"),
        name="in_projection",
    )(x, mod, gain, cos, sin, w_rest)


def _fourier_kernel(x_ref, mod_ref, g_ref, wr_ref, wi_ref, m_ref,
                    tr_ref, ti_ref, sr_ref, si_ref):
    x = jnp.swapaxes(x_ref[...], 0, 1).reshape(FFT_SLABS * R, D)
    hb = _modulated_norm(x, g_ref[...], mod_ref[0], mod_ref[1]).astype(bf16)
    zr = jnp.dot(hb, wr_ref[...], preferred_element_type=f32).astype(bf16)
    zi = jnp.dot(hb, wi_ref[...], preferred_element_type=f32).astype(bf16)
    for s in range(FFT_SLABS):
        z = jnp.concatenate([zr[s * R:(s + 1) * R], zi[s * R:(s + 1) * R]], axis=0)
        t = jnp.dot(m_ref[s].astype(bf16), z, preferred_element_type=f32)
        sr_ref[s] = t[:R]
        si_ref[s] = t[R:]
    tr_ref[...] = jnp.swapaxes(sr_ref[...], 0, 1).astype(bf16)
    ti_ref[...] = jnp.swapaxes(si_ref[...], 0, 1).astype(bf16)


def _fourier_call(layer, x, mod, gain, wr, wi):
    nsteps = R // FFT_SLABS
    slab = pl.BlockSpec((None, R, FFT_SLABS, D), lambda b, j: (b, 0, j, 0))
    t2 = jax.ShapeDtypeStruct((BATCH, R, R, D), bf16)
    return pl.pallas_call(
        _fourier_kernel,
        out_shape=(t2, t2),
        grid=(BATCH, nsteps),
        in_specs=[slab,
                  pl.BlockSpec((N_MOD, None, 1, D), lambda b, j: (0, b, 0, 0)),
                  pl.BlockSpec((1, D), lambda b, j: (0, 0)),
                  _layer_resident((D, D), layer), _layer_resident((D, D), layer),
                  pl.BlockSpec((FFT_SLABS, 2 * R, 2 * R), lambda b, j: (j, 0, 0))],
        out_specs=(slab, slab),
        scratch_shapes=[pltpu.VMEM((FFT_SLABS, R, D), f32), pltpu.VMEM((FFT_SLABS, R, D), f32)],
        compiler_params=_params("arbitrary", "arbitrary"),
        name="fourier_projection_dft1",
    )(x.reshape(BATCH, R, R, D), mod, gain, wr, wi, jnp.asarray(_STAGE1))


def _dft2_kernel(tr_ref, ti_ref, m_ref, y_ref, s_ref):
    m = m_ref[...].astype(bf16)
    for s in range(DFT2_SLABS):
        t = jnp.concatenate([tr_ref[s], ti_ref[s]], axis=0)
        s_ref[s] = jnp.dot(m, t, preferred_element_type=f32)
    y_ref[...] = jnp.swapaxes(s_ref[...], 0, 1).astype(bf16)


def _dft2_call(tr, ti):
    nsteps = R // DFT2_SLABS
    tin = pl.BlockSpec((None, DFT2_SLABS, R, D), lambda b, j: (b, j, 0, 0))
    yout = pl.BlockSpec((None, R, DFT2_SLABS, D), lambda b, j: (b, 0, j, 0))
    return pl.pallas_call(
        _dft2_kernel,
        out_shape=jax.ShapeDtypeStruct((BATCH, R, R, D), bf16),
        grid=(BATCH, nsteps),
        in_specs=[tin, tin, _resident((R, 2 * R))],
        out_specs=yout,
        scratch_shapes=[pltpu.VMEM((DFT2_SLABS, R, D), f32)],
        compiler_params=_params("arbitrary", "arbitrary"),
        name="dft_stage2",
    )(tr, ti, jnp.asarray(_STAGE2))


def _ret_kernel(q_ref, k_ref, v_ref, m_ref, dec_ref,
                o_ref, sf_ref, sb_ref, sb_all_ref, decb_ref):
    p = pl.program_id(1)
    j = pl.program_id(2)
    n_chunks = RET_ROWS // RET_CHUNK

    @pl.when(j == 0)
    def _():
        sf_ref[...] = jnp.zeros_like(sf_ref)
        sb_ref[...] = jnp.zeros_like(sb_ref)
        decb_ref[...] = dec_ref[...].astype(bf16)

    @pl.when(p == 0)
    def _():
        block = pl.num_programs(2) - 1 - j
        for c in reversed(range(n_chunks)):
            rows = slice(c * RET_CHUNK, (c + 1) * RET_CHUNK)
            for h in range(HEADS):
                cols = slice(h * DK, (h + 1) * DK)
                state = sb_ref[h]
                sb_all_ref[block * n_chunks + c, h] = state.astype(bf16)
                k_dec = k_ref[rows, cols] * decb_ref[_KDB, h]
                sb_ref[h] = state * float(_CDB[h]) + lax.dot_general(
                    k_dec, v_ref[rows, cols], _TN_DIMS, preferred_element_type=f32)

    @pl.when(p == 1)
    def _():
        for c in range(n_chunks):
            rows = slice(c * RET_CHUNK, (c + 1) * RET_CHUNK)
            for h in range(HEADS):
                cols = slice(h * DK, (h + 1) * DK)
                q = q_ref[rows, cols]
                k = k_ref[rows, cols]
                v = v_ref[rows, cols]
                scores = lax.dot_general(q, k, _NT_DIMS, preferred_element_type=f32) * m_ref[h]
                state = sf_ref[h]
                lhs = jnp.concatenate([scores.astype(bf16), q * decb_ref[_QDF, h],
                                       q * decb_ref[_QDB, h]], axis=1)
                rhs = jnp.concatenate([v, state.astype(bf16), sb_all_ref[j * n_chunks + c, h]],
                                      axis=0)
                y = jnp.dot(lhs, rhs, preferred_element_type=f32)
                k_dec = k * decb_ref[_KDF, h]
                sf_ref[h] = state * float(_CDF[h]) + lax.dot_general(
                    k_dec, v, _TN_DIMS, preferred_element_type=f32)
                o_ref[rows, cols] = y.astype(bf16)


def _ret_call(q, k, v):
    nb = SEQ // RET_ROWS

    def scan_idx(b, p, j):
        return (b, j + (1 - p) * (nb - 1 - 2 * j), 0)

    def finish_idx(b, p, j):
        return (b, p * j, 0)

    scan = pl.BlockSpec((None, RET_ROWS, D), scan_idx)
    fin = pl.BlockSpec((None, RET_ROWS, D), finish_idx)
    return pl.pallas_call(
        _ret_kernel,
        out_shape=jax.ShapeDtypeStruct((BATCH, SEQ, D), bf16),
        grid=(BATCH, 2, nb),
        in_specs=[fin, scan, scan, _resident((HEADS, RET_CHUNK, RET_CHUNK)),
                  _resident((4, HEADS, RET_CHUNK, DK))],
        out_specs=fin,
        scratch_shapes=[pltpu.VMEM((HEADS, DK, DK), f32), pltpu.VMEM((HEADS, DK, DK), f32),
                        pltpu.VMEM((SEQ // RET_CHUNK, HEADS, DK, DK), bf16),
                        pltpu.VMEM((4, HEADS, RET_CHUNK, DK), bf16)],
        compiler_params=_params("arbitrary", "arbitrary", "arbitrary"),
        name="retention",
    )(q, k, v, jnp.asarray(_RMASK), jnp.asarray(_RDEC))


def _merge_kernel(x_ref, mod_ref, yf_ref, yr_ref, gs_ref, af_ref, ar_ref, wf_ref, wr_ref, wo_ref,
                  o_ref):
    a = jnp.dot(yf_ref[...], wf_ref[...], preferred_element_type=f32)
    gated = []
    for h in range(HEADS):
        cols = slice(h * DK, (h + 1) * DK)
        y = yr_ref[:, cols].astype(f32)
        mu = jnp.mean(y, axis=-1, keepdims=True)
        yc = y - mu
        var = jnp.mean(yc * yc, axis=-1, keepdims=True)
        yn = yc * lax.rsqrt(var + EPS)
        gated.append((yn * gs_ref[:, cols].astype(f32)).astype(bf16))
    b = jnp.dot(jnp.concatenate(gated, axis=1), wr_ref[...], preferred_element_type=f32)
    merged = af_ref[...].astype(f32) * a + ar_ref[...].astype(f32) * b
    o = jnp.dot(merged.astype(bf16), wo_ref[...], preferred_element_type=f32)
    o_ref[...] = x_ref[...] + mod_ref[2] * o


def _merge_call(layer, x, mod, yf, yr, gs, af, ar, wf, wr, wo):
    row = pl.BlockSpec((None, TM, D), lambda b, i: (b, i, 0))
    return pl.pallas_call(
        _merge_kernel,
        out_shape=jax.ShapeDtypeStruct((BATCH, SEQ, D), f32),
        grid=(BATCH, NT),
        in_specs=[row, pl.BlockSpec((N_MOD, None, 1, D), lambda b, i: (0, b, 0, 0)),
                  row, row, row, row, row,
                  _layer_resident((D, D), layer), _layer_resident((D, D), layer),
                  _layer_resident((D, D), layer)],
        out_specs=row,
        compiler_params=_params("arbitrary", "arbitrary"),
        name="merge_out_projection",
    )(x, mod, yf, yr, gs, af, ar, wf, wr, wo)


def _ffn_kernel(x_ref, xp_ref, xn_ref, mod_ref, g_ref, wup_ref, cw_ref, cb_ref, wdn_ref, fg_ref,
                o_ref, act_ref, *, final):
    i = pl.program_id(1)
    rows = TM + 2 * HALO
    gain, shift, scale = g_ref[...], mod_ref[3], mod_ref[4]
    keep_prev = (i > 0).astype(f32)
    keep_next = (i < pl.num_programs(1) - 1).astype(f32)
    h = jnp.concatenate([
        _modulated_norm(xp_ref[...], gain, shift, scale) * keep_prev,
        _modulated_norm(x_ref[...], gain, shift, scale),
        _modulated_norm(xn_ref[...], gain, shift, scale) * keep_next,
    ], axis=0).astype(bf16)

    def conv(col):
        u = jnp.dot(h, wup_ref[:, col:col + FF_TILE], preferred_element_type=f32)
        w = cw_ref[:, col:col + FF_TILE]
        full = (pltpu.roll(u, 1, 0) * w[0:1] + u * w[1:2] + pltpu.roll(u, rows - 1, 0) * w[2:3]
                + cb_ref[:, col:col + FF_TILE])
        return full[HALO:HALO + TM]

    for t in range(D_FF // FF_TILE):
        a = conv(t * FF_TILE)
        b = conv(D_FF + t * FF_TILE)
        gelu = 0.5 * a * (1.0 + lax.erf(a * (2.0 ** -0.5)))
        act_ref[:, t * FF_TILE:(t + 1) * FF_TILE] = (gelu * b).astype(bf16)

    y = jnp.dot(act_ref[...], wdn_ref[...], preferred_element_type=f32)
    out = x_ref[...] + mod_ref[5] * y
    if final:
        ms = jnp.mean(out * out, axis=-1, keepdims=True)
        out = out * lax.rsqrt(ms + EPS) * fg_ref[...]
    o_ref[...] = out


def _ffn_call(layer, x, mod, gain, wup, cw, cb, wdn, final_g, final):
    tiles = TM // HALO
    row = pl.BlockSpec((None, TM, D), lambda b, i: (b, i, 0))
    prev = pl.BlockSpec((None, HALO, D), lambda b, i: (b, jnp.maximum(i * tiles - 1, 0), 0))
    nxt = pl.BlockSpec((None, HALO, D),
                       lambda b, i: (b, jnp.minimum((i + 1) * tiles, SEQ // HALO - 1), 0))
    return pl.pallas_call(
        functools.partial(_ffn_kernel, final=final),
        out_shape=jax.ShapeDtypeStruct((BATCH, SEQ, D), f32),
        grid=(BATCH, NT),
        in_specs=[row, prev, nxt,
                  pl.BlockSpec((N_MOD, None, 1, D), lambda b, i: (0, b, 0, 0)),
                  pl.BlockSpec((1, D), lambda b, i: (0, 0)),
                  _layer_resident((D, 2 * D_FF), layer), _layer_resident((3, 2 * D_FF), layer),
                  _layer_resident((1, 2 * D_FF), layer), _layer_resident((D_FF, D), layer),
                  pl.BlockSpec((1, D), lambda b, i: (0, 0))],
        out_specs=row,
        scratch_shapes=[pltpu.VMEM((TM, D_FF), bf16)],
        compiler_params=_params("arbitrary", "arbitrary"),
        name="conv_ffn_final" if final else "conv_ffn",
    )(x, x, x, mod, gain, wup, cw, cb, wdn, final_g)


def kernel(x, c, norm1_g, norm2_g, ada_w, ada_b, w_in, w_fourier, w_ret, w_out,
           ffn_up, conv_w, conv_b, ffn_down, final_g):
    assert x.shape == (BATCH, SEQ, D) and c.shape == (BATCH, D)
    mod_all = _ada_call(c, ada_w, ada_b).reshape(DEPTH, N_MOD, BATCH, 1, D)
    wfr_all, wfi_all, w_rest = _prep_call(w_in)
    cos, sin = _rotary_tables()
    final_gain = final_g.reshape(1, D)
    wf, wr, wo = w_fourier.astype(bf16), w_ret.astype(bf16), w_out.astype(bf16)
    wup, wdn = ffn_up.astype(bf16), ffn_down.astype(bf16)

    for l in range(DEPTH):
        mod = mod_all[l]
        gain1 = norm1_g[l].reshape(1, D)
        q, k, v, gs, af, ar = _inproj_call(l, x, mod, gain1, cos, sin, w_rest)
        tr, ti = _fourier_call(l, x, mod, gain1, wfr_all, wfi_all)
        yf = _dft2_call(tr, ti).reshape(BATCH, SEQ, D)
        yr = _ret_call(q, k, v)
        x = _merge_call(l, x, mod, yf, yr, gs, af, ar, wf, wr, wo)
        x = _ffn_call(l, x, mod, norm2_g[l].reshape(1, D), wup, conv_w,
                      conv_b.reshape(DEPTH, 1, 2 * D_FF), wdn, final_gain,
                      final=(l == DEPTH - 1))
    return x
```

```python
import functools

import numpy as np
import jax
import jax.numpy as jnp
from jax import lax
from jax.experimental import pallas as pl
from jax.experimental.pallas import tpu as pltpu

f32 = jnp.float32
bf16 = jnp.bfloat16

D = 1024
BATCH = 8
SEQ = 4096
DEPTH = 4
GROUP = 128
HEADS = 4
DK = D // HEADS
ROPE_BASE = 10000.0
D_FF = 2816
N_MOD = 6
EPS = 1e-6
D_REST = 6 * D

R = 64
TM = 1024
NT = SEQ // TM
RET_CHUNK = 256
RET_ROWS = 2048
FFT_SLABS = 16
DFT2_SLABS = 32
FF_TILE = 256
HALO = 8
VMEM_LIMIT = 56 * 1024 * 1024

_NT_DIMS = (((1,), (1,)), ((), ()))
_TN_DIMS = (((0,), (0,)), ((), ()))


def _resident(shape):
    nd = len(shape)
    return pl.BlockSpec(shape, lambda *_: (0,) * nd, pipeline_mode=pl.Buffered(1))


def _layer_resident(shape, layer):
    nd = len(shape)
    return pl.BlockSpec((None,) + tuple(shape), lambda *_: (layer,) + (0,) * nd,
                        pipeline_mode=pl.Buffered(1))


def _params(*sem):
    return pltpu.CompilerParams(dimension_semantics=sem, vmem_limit_bytes=VMEM_LIMIT)


def _group_dft_tables():
    n = np.arange(GROUP)
    ang = 2.0 * np.pi * np.outer(n, n) / GROUP
    s = 1.0 / np.sqrt(GROUP)
    return (np.cos(ang) * s).astype(np.float32), (-np.sin(ang) * s).astype(np.float32)


def _stage_tables():
    k = np.arange(R)
    ang = 2.0 * np.pi * np.outer(k, k) / R
    cr = np.cos(ang) / 8.0
    ci = -np.sin(ang) / 8.0
    stage2 = np.concatenate([cr, -ci], axis=1)
    n2 = np.arange(R)[:, None, None]
    k1 = np.arange(R)[None, :, None]
    n1 = np.arange(R)[None, None, :]
    a = np.exp(-2j * np.pi * (n2 * k1 / SEQ + n1 * k1 / R)) / 8.0
    stage1 = np.concatenate([np.concatenate([a.real, -a.imag], axis=2),
                             np.concatenate([a.imag, a.real], axis=2)], axis=1)
    return stage1.astype(np.float32), stage2.astype(np.float32)


def _retention_tables():
    c = RET_CHUNK
    j = np.arange(c, dtype=np.float64)
    diff = j[:, None] - j[None, :]
    ones = np.ones((1, DK))
    mask = np.zeros((HEADS, c, c))
    qdf, qdb, kdf, kdb = (np.zeros((HEADS, c, DK)) for _ in range(4))
    cdf, cdb = np.zeros(HEADS), np.zeros(HEADS)
    for h in range(HEADS):
        lf = np.log1p(-np.exp2(-5.0 - h))
        lb = np.log1p(-np.exp2(-5.5 - h))
        mask[h] = np.where(diff >= 0, np.exp(lf * np.maximum(diff, 0.0)),
                           np.exp(lb * np.maximum(-diff, 0.0)))
        qdf[h] = np.exp(lf * (j + 1.0))[:, None] * ones
        kdf[h] = np.exp(lf * (c - 1.0 - j))[:, None] * ones
        qdb[h] = np.exp(lb * (c - j))[:, None] * ones
        kdb[h] = np.exp(lb * j)[:, None] * ones
        cdf[h] = np.exp(lf * c)
        cdb[h] = np.exp(lb * c)
    dec = np.stack([qdf, qdb, kdf, kdb])
    return mask.astype(np.float32), dec.astype(np.float32), cdf, cdb


_GROUP_TABLE = np.concatenate(_group_dft_tables(), axis=1)
_STAGE1, _STAGE2 = _stage_tables()
_RMASK, _RDEC, _CDF, _CDB = _retention_tables()
_QDF, _QDB, _KDF, _KDB = range(4)


def _rotary_tables():
    half = DK // 2
    inv_freq = ROPE_BASE ** (-jnp.arange(half, dtype=f32) / half)
    ang = jnp.arange(SEQ, dtype=f32)[:, None] * inv_freq[None, :]
    return jnp.cos(ang), jnp.sin(ang)


def _ada_kernel(c_ref, w_ref, b_ref, o_ref):
    c = c_ref[...]
    act = (c * jax.nn.sigmoid(c)).astype(bf16)
    o_ref[...] = jnp.dot(act, w_ref[...].astype(bf16), preferred_element_type=f32) + b_ref[...]


def _ada_call(c, ada_w, ada_b):
    return pl.pallas_call(
        _ada_kernel,
        out_shape=jax.ShapeDtypeStruct((DEPTH, N_MOD, BATCH, D), f32),
        grid=(DEPTH, N_MOD),
        in_specs=[
            pl.BlockSpec((BATCH, D), lambda l, j: (0, 0)),
            pl.BlockSpec((None, D, D), lambda l, j: (l, 0, j)),
            pl.BlockSpec((None, None, 1, D), lambda l, j: (l, j, 0, 0)),
        ],
        out_specs=pl.BlockSpec((None, None, BATCH, D), lambda l, j: (l, j, 0, 0)),
        compiler_params=_params("arbitrary", "arbitrary"),
        name="ada_mod",
    )(c, ada_w, ada_b.reshape(DEPTH, N_MOD, 1, D))


def _prep_kernel(w_ref, t_ref, wr_ref, wi_ref, rest_ref):
    j = pl.program_id(1)

    @pl.when(j == 0)
    def _():
        table = t_ref[...]
        t_hi = table.astype(bf16)
        t_lo = (table - t_hi.astype(f32)).astype(bf16)
        for g in range(D // GROUP):
            cols = slice(g * GROUP, (g + 1) * GROUP)
            w = w_ref[:, cols]
            hi = w.astype(bf16)
            lo = (w - hi.astype(f32)).astype(bf16)
            folded = (jnp.dot(hi, t_hi, preferred_element_type=f32)
                      + jnp.dot(lo, t_hi, preferred_element_type=f32)
                      + jnp.dot(hi, t_lo, preferred_element_type=f32))
            wr_ref[:, cols] = folded[:, :GROUP].astype(bf16)
            wi_ref[:, cols] = folded[:, GROUP:].astype(bf16)

    @pl.when(j > 0)
    def _():
        rest_ref[...] = w_ref[...].astype(bf16)


def _prep_call(w_in):
    sq = jax.ShapeDtypeStruct((DEPTH, D, D), bf16)
    tab = pl.BlockSpec((GROUP, 2 * GROUP), lambda l, j: (0, 0))
    fold_out = pl.BlockSpec((None, D, D), lambda l, j: (l, 0, 0))
    return pl.pallas_call(
        _prep_kernel,
        out_shape=(sq, sq, jax.ShapeDtypeStruct((DEPTH, D, D_REST), bf16)),
        grid=(DEPTH, 1 + D_REST // D),
        in_specs=[pl.BlockSpec((None, D, D), lambda l, j: (l, 0, j)), tab],
        out_specs=(fold_out, fold_out,
                   pl.BlockSpec((None, D, D), lambda l, j: (l, 0, jnp.maximum(j - 1, 0)))),
        compiler_params=_params("arbitrary", "arbitrary"),
        name="weight_prep",
    )(w_in, jnp.asarray(_GROUP_TABLE))


def _modulated_norm(x, gain, shift, scale):
    ms = jnp.mean(x * x, axis=-1, keepdims=True)
    return (x * lax.rsqrt(ms + EPS) * gain) * (1.0 + scale) + shift


def _inproj_kernel(x_ref, mod_ref, g_ref, cos_ref, sin_ref, w_ref,
                   hb_ref, q_ref, k_ref, v_ref, gs_ref, af_ref, ar_ref):
    hb = _modulated_norm(x_ref[...], g_ref[...], mod_ref[0], mod_ref[1]).astype(bf16)
    hb_ref[...] = hb
    cos = cos_ref[...]
    sin = sin_ref[...]
    half = DK // 2

    def rotary(sec, o_ref, scale):
        p = jnp.dot(hb, w_ref[:, sec * D:(sec + 1) * D], preferred_element_type=f32)
        for h in range(HEADS):
            t1 = p[:, h * DK:h * DK + half]
            t2 = p[:, h * DK + half:(h + 1) * DK]
            o_ref[:, h * DK:h * DK + half] = ((t1 * cos - t2 * sin) * scale).astype(bf16)
            o_ref[:, h * DK + half:(h + 1) * DK] = ((t1 * sin + t2 * cos) * scale).astype(bf16)

    rotary(0, q_ref, DK ** -0.5)
    rotary(1, k_ref, 1.0)
    g = jnp.dot(hb, w_ref[:, 3 * D:4 * D], preferred_element_type=f32)
    gs_ref[...] = (g * jax.nn.sigmoid(g)).astype(bf16)
    af = jnp.dot(hb, w_ref[:, 4 * D:5 * D], preferred_element_type=f32)
    af_ref[...] = jax.nn.sigmoid(af).astype(bf16)
    ar = jnp.dot(hb, w_ref[:, 5 * D:6 * D], preferred_element_type=f32)
    ar_ref[...] = jax.nn.sigmoid(ar).astype(bf16)
    v_ref[...] = jnp.dot(hb, w_ref[:, 2 * D:3 * D], preferred_element_type=f32).astype(bf16)


def _inproj_call(layer, x, mod, gain, cos, sin, w_rest):
    row = pl.BlockSpec((None, TM, D), lambda b, i: (b, i, 0))
    rot = pl.BlockSpec((TM, DK // 2), lambda b, i: (i, 0))
    act = jax.ShapeDtypeStruct((BATCH, SEQ, D), bf16)
    return pl.pallas_call(
        _inproj_kernel,
        out_shape=(act, act, act, act, act, act, act),
        grid=(BATCH, NT),
        in_specs=[
            row,
            pl.BlockSpec((N_MOD, None, 1, D), lambda b, i: (0, b, 0, 0)),
            pl.BlockSpec((1, D), lambda b, i: (0, 0)),
            rot, rot,
            _layer_resident((D, D_REST), layer),
        ],
        out_specs=(row, row, row, row, row, row, row),
        compiler_params=_params("arbitrary", "arbitrary"),
        name="in_projection",
    )(x, mod, gain, cos, sin, w_rest)


def _fourier_kernel(h_ref, wr_ref, wi_ref, m_ref, tr_ref, ti_ref, sr_ref, si_ref):
    hb = jnp.swapaxes(h_ref[...], 0, 1).reshape(FFT_SLABS * R, D)
    zr = jnp.dot(hb, wr_ref[...], preferred_element_type=f32).astype(bf16)
    zi = jnp.dot(hb, wi_ref[...], preferred_element_type=f32).astype(bf16)
    for s in range(FFT_SLABS):
        z = jnp.concatenate([zr[s * R:(s + 1) * R], zi[s * R:(s + 1) * R]], axis=0)
        t = jnp.dot(m_ref[s].astype(bf16), z, preferred_element_type=f32)
        sr_ref[s] = t[:R]
        si_ref[s] = t[R:]
    tr_ref[...] = jnp.swapaxes(sr_ref[...], 0, 1).astype(bf16)
    ti_ref[...] = jnp.swapaxes(si_ref[...], 0, 1).astype(bf16)


def _fourier_call(layer, hb, wr, wi):
    nsteps = R // FFT_SLABS
    slab = pl.BlockSpec((None, R, FFT_SLABS, D), lambda b, j: (b, 0, j, 0))
    t2 = jax.ShapeDtypeStruct((BATCH, R, R, D), bf16)
    return pl.pallas_call(
        _fourier_kernel,
        out_shape=(t2, t2),
        grid=(BATCH, nsteps),
        in_specs=[slab,
                  _layer_resident((D, D), layer), _layer_resident((D, D), layer),
                  pl.BlockSpec((FFT_SLABS, 2 * R, 2 * R), lambda b, j: (j, 0, 0))],
        out_specs=(slab, slab),
        scratch_shapes=[pltpu.VMEM((FFT_SLABS, R, D), f32), pltpu.VMEM((FFT_SLABS, R, D), f32)],
        compiler_params=_params("arbitrary", "arbitrary"),
        name="fourier_projection_dft1",
    )(hb.reshape(BATCH, R, R, D), wr, wi, jnp.asarray(_STAGE1))


def _dft2_kernel(tr_ref, ti_ref, m_ref, y_ref, s_ref):
    m = m_ref[...].astype(bf16)
    for s in range(DFT2_SLABS):
        t = jnp.concatenate([tr_ref[s], ti_ref[s]], axis=0)
        s_ref[s] = jnp.dot(m, t, preferred_element_type=f32)
    y_ref[...] = jnp.swapaxes(s_ref[...], 0, 1).astype(bf16)


def _dft2_call(tr, ti):
    nsteps = R // DFT2_SLABS
    tin = pl.BlockSpec((None, DFT2_SLABS, R, D), lambda b, j: (b, j, 0, 0))
    yout = pl.BlockSpec((None, R, DFT2_SLABS, D), lambda b, j: (b, 0, j, 0))
    return pl.pallas_call(
        _dft2_kernel,
        out_shape=jax.ShapeDtypeStruct((BATCH, R, R, D), bf16),
        grid=(BATCH, nsteps),
        in_specs=[tin, tin, _resident((R, 2 * R))],
        out_specs=yout,
        scratch_shapes=[pltpu.VMEM((DFT2_SLABS, R, D), f32)],
        compiler_params=_params("arbitrary", "arbitrary"),
        name="dft_stage2",
    )(tr, ti, jnp.asarray(_STAGE2))


def _ret_kernel(q_ref, k_ref, v_ref, m_ref, dec_ref,
                o_ref, sf_ref, sb_ref, sb_all_ref, decb_ref):
    p = pl.program_id(1)
    j = pl.program_id(2)
    n_chunks = RET_ROWS // RET_CHUNK

    @pl.when(j == 0)
    def _():
        sf_ref[...] = jnp.zeros_like(sf_ref)
        sb_ref[...] = jnp.zeros_like(sb_ref)
        decb_ref[...] = dec_ref[...].astype(bf16)

    @pl.when(p == 0)
    def _():
        block = pl.num_programs(2) - 1 - j
        for c in reversed(range(n_chunks)):
            rows = slice(c * RET_CHUNK, (c + 1) * RET_CHUNK)
            for h in range(HEADS):
                cols = slice(h * DK, (h + 1) * DK)
                state = sb_ref[h]
                sb_all_ref[block * n_chunks + c, h] = state.astype(bf16)
                k_dec = k_ref[rows, cols] * decb_ref[_KDB, h]
                sb_ref[h] = state * float(_CDB[h]) + lax.dot_general(
                    k_dec, v_ref[rows, cols], _TN_DIMS, preferred_element_type=f32)

    @pl.when(p == 1)
    def _():
        for c in range(n_chunks):
            rows = slice(c * RET_CHUNK, (c + 1) * RET_CHUNK)
            for h in range(HEADS):
                cols = slice(h * DK, (h + 1) * DK)
                q = q_ref[rows, cols]
                k = k_ref[rows, cols]
                v = v_ref[rows, cols]
                scores = lax.dot_general(q, k, _NT_DIMS, preferred_element_type=f32) * m_ref[h]
                state = sf_ref[h]
                lhs = jnp.concatenate([scores.astype(bf16), q * decb_ref[_QDF, h],
                                       q * decb_ref[_QDB, h]], axis=1)
                rhs = jnp.concatenate([v, state.astype(bf16), sb_all_ref[j * n_chunks + c, h]],
                                      axis=0)
                y = jnp.dot(lhs, rhs, preferred_element_type=f32)
                k_dec = k * decb_ref[_KDF, h]
                sf_ref[h] = state * float(_CDF[h]) + lax.dot_general(
                    k_dec, v, _TN_DIMS, preferred_element_type=f32)
                o_ref[rows, cols] = y.astype(bf16)


def _ret_call(q, k, v):
    nb = SEQ // RET_ROWS

    def scan_idx(b, p, j):
        return (b, j + (1 - p) * (nb - 1 - 2 * j), 0)

    def finish_idx(b, p, j):
        return (b, p * j, 0)

    scan = pl.BlockSpec((None, RET_ROWS, D), scan_idx)
    fin = pl.BlockSpec((None, RET_ROWS, D), finish_idx)
    return pl.pallas_call(
        _ret_kernel,
        out_shape=jax.ShapeDtypeStruct((BATCH, SEQ, D), bf16),
        grid=(BATCH, 2, nb),
        in_specs=[fin, scan, scan, _resident((HEADS, RET_CHUNK, RET_CHUNK)),
                  _resident((4, HEADS, RET_CHUNK, DK))],
        out_specs=fin,
        scratch_shapes=[pltpu.VMEM((HEADS, DK, DK), f32), pltpu.VMEM((HEADS, DK, DK), f32),
                        pltpu.VMEM((SEQ // RET_CHUNK, HEADS, DK, DK), bf16),
                        pltpu.VMEM((4, HEADS, RET_CHUNK, DK), bf16)],
        compiler_params=_params("arbitrary", "arbitrary", "arbitrary"),
        name="retention",
    )(q, k, v, jnp.asarray(_RMASK), jnp.asarray(_RDEC))


def _merge_kernel(x_ref, mod_ref, yf_ref, yr_ref, gs_ref, af_ref, ar_ref, wf_ref, wr_ref, wo_ref,
                  o_ref):
    a = jnp.dot(yf_ref[...], wf_ref[...], preferred_element_type=f32)
    gated = []
    for h in range(HEADS):
        cols = slice(h * DK, (h + 1) * DK)
        y = yr_ref[:, cols].astype(f32)
        mu = jnp.mean(y, axis=-1, keepdims=True)
        yc = y - mu
        var = jnp.mean(yc * yc, axis=-1, keepdims=True)
        yn = yc * lax.rsqrt(var + EPS)
        gated.append((yn * gs_ref[:, cols].astype(f32)).astype(bf16))
    b = jnp.dot(jnp.concatenate(gated, axis=1), wr_ref[...], preferred_element_type=f32)
    merged = af_ref[...].astype(f32) * a + ar_ref[...].astype(f32) * b
    o = jnp.dot(merged.astype(bf16), wo_ref[...], preferred_element_type=f32)
    o_ref[...] = x_ref[...] + mod_ref[2] * o


def _merge_call(layer, x, mod, yf, yr, gs, af, ar, wf, wr, wo):
    row = pl.BlockSpec((None, TM, D), lambda b, i: (b, i, 0))
    return pl.pallas_call(
        _merge_kernel,
        out_shape=jax.ShapeDtypeStruct((BATCH, SEQ, D), f32),
        grid=(BATCH, NT),
        in_specs=[row, pl.BlockSpec((N_MOD, None, 1, D), lambda b, i: (0, b, 0, 0)),
                  row, row, row, row, row,
                  _layer_resident((D, D), layer), _layer_resident((D, D), layer),
                  _layer_resident((D, D), layer)],
        out_specs=row,
        compiler_params=_params("arbitrary", "arbitrary"),
        name="merge_out_projection",
    )(x, mod, yf, yr, gs, af, ar, wf, wr, wo)


def _ffn_kernel(x_ref, xp_ref, xn_ref, mod_ref, g_ref, wup_ref, cw_ref, cb_ref, wdn_ref, fg_ref,
                o_ref, act_ref, *, final):
    i = pl.program_id(1)
    rows = TM + 2 * HALO
    gain, shift, scale = g_ref[...], mod_ref[3], mod_ref[4]
    keep_prev = (i > 0).astype(f32)
    keep_next = (i < pl.num_programs(1) - 1).astype(f32)
    h = jnp.concatenate([
        _modulated_norm(xp_ref[...], gain, shift, scale) * keep_prev,
        _modulated_norm(x_ref[...], gain, shift, scale),
        _modulated_norm(xn_ref[...], gain, shift, scale) * keep_next,
    ], axis=0).astype(bf16)

    def conv(col):
        u = jnp.dot(h, wup_ref[:, col:col + FF_TILE], preferred_element_type=f32)
        w = cw_ref[:, col:col + FF_TILE]
        full = (pltpu.roll(u, 1, 0) * w[0:1] + u * w[1:2] + pltpu.roll(u, rows - 1, 0) * w[2:3]
                + cb_ref[:, col:col + FF_TILE])
        return full[HALO:HALO + TM]

    for t in range(D_FF // FF_TILE):
        a = conv(t * FF_TILE)
        b = conv(D_FF + t * FF_TILE)
        gelu = 0.5 * a * (1.0 + lax.erf(a * (2.0 ** -0.5)))
        act_ref[:, t * FF_TILE:(t + 1) * FF_TILE] = (gelu * b).astype(bf16)

    y = jnp.dot(act_ref[...], wdn_ref[...], preferred_element_type=f32)
    out = x_ref[...] + mod_ref[5] * y
    if final:
        ms = jnp.mean(out * out, axis=-1, keepdims=True)
        out = out * lax.rsqrt(ms + EPS) * fg_ref[...]
    o_ref[...] = out


def _ffn_call(layer, x, mod, gain, wup, cw, cb, wdn, final_g, final):
    tiles = TM // HALO
    row = pl.BlockSpec((None, TM, D), lambda b, i: (b, i, 0))
    prev = pl.BlockSpec((None, HALO, D), lambda b, i: (b, jnp.maximum(i * tiles - 1, 0), 0))
    nxt = pl.BlockSpec((None, HALO, D),
                       lambda b, i: (b, jnp.minimum((i + 1) * tiles, SEQ // HALO - 1), 0))
    return pl.pallas_call(
        functools.partial(_ffn_kernel, final=final),
        out_shape=jax.ShapeDtypeStruct((BATCH, SEQ, D), f32),
        grid=(BATCH, NT),
        in_specs=[row, prev, nxt,
                  pl.BlockSpec((N_MOD, None, 1, D), lambda b, i: (0, b, 0, 0)),
                  pl.BlockSpec((1, D), lambda b, i: (0, 0)),
                  _layer_resident((D, 2 * D_FF), layer), _layer_resident((3, 2 * D_FF), layer),
                  _layer_resident((1, 2 * D_FF), layer), _layer_resident((D_FF, D), layer),
                  pl.BlockSpec((1, D), lambda b, i: (0, 0))],
        out_specs=row,
        scratch_shapes=[pltpu.VMEM((TM, D_FF), bf16)],
        compiler_params=_params("arbitrary", "arbitrary"),
        name="conv_ffn_final" if final else "conv_ffn",
    )(x, x, x, mod, gain, wup, cw, cb, wdn, final_g)


def kernel(x, c, norm1_g, norm2_g, ada_w, ada_b, w_in, w_fourier, w_ret, w_out,
           ffn_up, conv_w, conv_b, ffn_down, final_g):
    assert x.shape == (BATCH, SEQ, D) and c.shape == (BATCH, D)
    mod_all = _ada_call(c, ada_w, ada_b).reshape(DEPTH, N_MOD, BATCH, 1, D)
    wfr_all, wfi_all, w_rest = _prep_call(w_in)
    cos, sin = _rotary_tables()
    final_gain = final_g.reshape(1, D)
    wf, wr, wo = w_fourier.astype(bf16), w_ret.astype(bf16), w_out.astype(bf16)
    wup, wdn = ffn_up.astype(bf16), ffn_down.astype(bf16)

    for l in range(DEPTH):
        mod = mod_all[l]
        gain1 = norm1_g[l].reshape(1, D)
        hb, q, k, v, gs, af, ar = _inproj_call(l, x, mod, gain1, cos, sin, w_rest)
        tr, ti = _fourier_call(l, hb, wfr_all, wfi_all)
        yf = _dft2_call(tr, ti).reshape(BATCH, SEQ, D)
        yr = _ret_call(q, k, v)
        x = _merge_call(l, x, mod, yf, yr, gs, af, ar, wf, wr, wo)
        x = _ffn_call(l, x, mod, norm2_g[l].reshape(1, D), wup, conv_w,
                      conv_b.reshape(DEPTH, 1, 2 * D_FF), wdn, final_gain,
                      final=(l == DEPTH - 1))
    return x
```

```python
import functools

import numpy as np
import jax
import jax.numpy as jnp
from jax import lax
from jax.experimental import pallas as pl
from jax.experimental.pallas import tpu as pltpu

f32 = jnp.float32
bf16 = jnp.bfloat16

D = 1024
BATCH = 8
SEQ = 4096
DEPTH = 4
GROUP = 128
HEADS = 4
DK = D // HEADS
ROPE_BASE = 10000.0
D_FF = 2816
N_MOD = 6
EPS = 1e-6
D_REST = 6 * D

R = 64
TM = 1024
NT = SEQ // TM
RET_CHUNK = 256
RET_ROWS = 2048
FFT_SLABS = 32
DFT2_SLABS = 32
FF_TILE = 256
HALO = 8
VMEM_LIMIT = 56 * 1024 * 1024

_NT_DIMS = (((1,), (1,)), ((), ()))
_TN_DIMS = (((0,), (0,)), ((), ()))


def _resident(shape):
    nd = len(shape)
    return pl.BlockSpec(shape, lambda *_: (0,) * nd, pipeline_mode=pl.Buffered(1))


def _layer_resident(shape, layer):
    nd = len(shape)
    return pl.BlockSpec((None,) + tuple(shape), lambda *_: (layer,) + (0,) * nd,
                        pipeline_mode=pl.Buffered(1))


def _params(*sem):
    return pltpu.CompilerParams(dimension_semantics=sem, vmem_limit_bytes=VMEM_LIMIT)


def _group_dft_tables():
    n = np.arange(GROUP)
    ang = 2.0 * np.pi * np.outer(n, n) / GROUP
    s = 1.0 / np.sqrt(GROUP)
    return (np.cos(ang) * s).astype(np.float32), (-np.sin(ang) * s).astype(np.float32)


def _stage_tables():
    k = np.arange(R)
    ang = 2.0 * np.pi * np.outer(k, k) / R
    cr = np.cos(ang) / 8.0
    ci = -np.sin(ang) / 8.0
    stage2 = np.concatenate([cr, -ci], axis=1)
    n2 = np.arange(R)[:, None, None]
    k1 = np.arange(R)[None, :, None]
    n1 = np.arange(R)[None, None, :]
    a = np.exp(-2j * np.pi * (n2 * k1 / SEQ + n1 * k1 / R)) / 8.0
    stage1 = np.concatenate([np.concatenate([a.real, -a.imag], axis=2),
                             np.concatenate([a.imag, a.real], axis=2)], axis=1)
    return stage1.astype(np.float32), stage2.astype(np.float32)


def _retention_tables():
    c = RET_CHUNK
    j = np.arange(c, dtype=np.float64)
    diff = j[:, None] - j[None, :]
    ones = np.ones((1, DK))
    mask = np.zeros((HEADS, c, c))
    qdf, qdb, kdf, kdb = (np.zeros((HEADS, c, DK)) for _ in range(4))
    cdf, cdb = np.zeros(HEADS), np.zeros(HEADS)
    for h in range(HEADS):
        lf = np.log1p(-np.exp2(-5.0 - h))
        lb = np.log1p(-np.exp2(-5.5 - h))
        mask[h] = np.where(diff >= 0, np.exp(lf * np.maximum(diff, 0.0)),
                           np.exp(lb * np.maximum(-diff, 0.0)))
        qdf[h] = np.exp(lf * (j + 1.0))[:, None] * ones
        kdf[h] = np.exp(lf * (c - 1.0 - j))[:, None] * ones
        qdb[h] = np.exp(lb * (c - j))[:, None] * ones
        kdb[h] = np.exp(lb * j)[:, None] * ones
        cdf[h] = np.exp(lf * c)
        cdb[h] = np.exp(lb * c)
    dec = np.stack([qdf, qdb, kdf, kdb])
    return mask.astype(np.float32), dec.astype(np.float32), cdf, cdb


_GROUP_TABLE = np.concatenate(_group_dft_tables(), axis=1)
_STAGE1, _STAGE2 = _stage_tables()
_RMASK, _RDEC, _CDF, _CDB = _retention_tables()
_QDF, _QDB, _KDF, _KDB = range(4)


def _rotary_tables():
    half = DK // 2
    inv_freq = ROPE_BASE ** (-jnp.arange(half, dtype=f32) / half)
    ang = jnp.arange(SEQ, dtype=f32)[:, None] * inv_freq[None, :]
    return jnp.cos(ang), jnp.sin(ang)


def _ada_kernel(c_ref, w_ref, b_ref, o_ref):
    c = c_ref[...]
    act = (c * jax.nn.sigmoid(c)).astype(bf16)
    o_ref[...] = jnp.dot(act, w_ref[...].astype(bf16), preferred_element_type=f32) + b_ref[...]


def _ada_call(c, ada_w, ada_b):
    return pl.pallas_call(
        _ada_kernel,
        out_shape=jax.ShapeDtypeStruct((DEPTH, N_MOD, BATCH, D), f32),
        grid=(DEPTH, N_MOD),
        in_specs=[
            pl.BlockSpec((BATCH, D), lambda l, j: (0, 0)),
            pl.BlockSpec((None, D, D), lambda l, j: (l, 0, j)),
            pl.BlockSpec((None, None, 1, D), lambda l, j: (l, j, 0, 0)),
        ],
        out_specs=pl.BlockSpec((None, None, BATCH, D), lambda l, j: (l, j, 0, 0)),
        compiler_params=_params("arbitrary", "arbitrary"),
        name="ada_mod",
    )(c, ada_w, ada_b.reshape(DEPTH, N_MOD, 1, D))


def _prep_kernel(w_ref, t_ref, wr_ref, wi_ref, rest_ref):
    j = pl.program_id(1)

    @pl.when(j == 0)
    def _():
        table = t_ref[...]
        t_hi = table.astype(bf16)
        t_lo = (table - t_hi.astype(f32)).astype(bf16)
        for g in range(D // GROUP):
            cols = slice(g * GROUP, (g + 1) * GROUP)
            w = w_ref[:, cols]
            hi = w.astype(bf16)
            lo = (w - hi.astype(f32)).astype(bf16)
            folded = (jnp.dot(hi, t_hi, preferred_element_type=f32)
                      + jnp.dot(lo, t_hi, preferred_element_type=f32)
                      + jnp.dot(hi, t_lo, preferred_element_type=f32))
            wr_ref[:, cols] = folded[:, :GROUP].astype(bf16)
            wi_ref[:, cols] = folded[:, GROUP:].astype(bf16)

    @pl.when(j > 0)
    def _():
        rest_ref[...] = w_ref[...].astype(bf16)


def _prep_call(w_in):
    sq = jax.ShapeDtypeStruct((DEPTH, D, D), bf16)
    tab = pl.BlockSpec((GROUP, 2 * GROUP), lambda l, j: (0, 0))
    fold_out = pl.BlockSpec((None, D, D), lambda l, j: (l, 0, 0))
    return pl.pallas_call(
        _prep_kernel,
        out_shape=(sq, sq, jax.ShapeDtypeStruct((DEPTH, D, D_REST), bf16)),
        grid=(DEPTH, 1 + D_REST // D),
        in_specs=[pl.BlockSpec((None, D, D), lambda l, j: (l, 0, j)), tab],
        out_specs=(fold_out, fold_out,
                   pl.BlockSpec((None, D, D), lambda l, j: (l, 0, jnp.maximum(j - 1, 0)))),
        compiler_params=_params("arbitrary", "arbitrary"),
        name="weight_prep",
    )(w_in, jnp.asarray(_GROUP_TABLE))


def _modulated_norm(x, gain, shift, scale):
    ms = jnp.mean(x * x, axis=-1, keepdims=True)
    return (x * lax.rsqrt(ms + EPS) * gain) * (1.0 + scale) + shift


def _inproj_kernel(x_ref, mod_ref, g_ref, cos_ref, sin_ref, w_ref,
                   hb_ref, q_ref, k_ref, v_ref, gs_ref, af_ref, ar_ref):
    hb = _modulated_norm(x_ref[...], g_ref[...], mod_ref[0], mod_ref[1]).astype(bf16)
    hb_ref[...] = hb
    cos = cos_ref[...]
    sin = sin_ref[...]
    half = DK // 2

    def rotary(sec, o_ref, scale):
        p = jnp.dot(hb, w_ref[:, sec * D:(sec + 1) * D], preferred_element_type=f32)
        for h in range(HEADS):
            t1 = p[:, h * DK:h * DK + half]
            t2 = p[:, h * DK + half:(h + 1) * DK]
            o_ref[:, h * DK:h * DK + half] = ((t1 * cos - t2 * sin) * scale).astype(bf16)
            o_ref[:, h * DK + half:(h + 1) * DK] = ((t1 * sin + t2 * cos) * scale).astype(bf16)

    rotary(0, q_ref, DK ** -0.5)
    rotary(1, k_ref, 1.0)
    g = jnp.dot(hb, w_ref[:, 3 * D:4 * D], preferred_element_type=f32)
    gs_ref[...] = (g * jax.nn.sigmoid(g)).astype(bf16)
    af = jnp.dot(hb, w_ref[:, 4 * D:5 * D], preferred_element_type=f32)
    af_ref[...] = jax.nn.sigmoid(af).astype(bf16)
    ar = jnp.dot(hb, w_ref[:, 5 * D:6 * D], preferred_element_type=f32)
    ar_ref[...] = jax.nn.sigmoid(ar).astype(bf16)
    v_ref[...] = jnp.dot(hb, w_ref[:, 2 * D:3 * D], preferred_element_type=f32).astype(bf16)


def _inproj_call(layer, x, mod, gain, cos, sin, w_rest):
    row = pl.BlockSpec((None, TM, D), lambda b, i: (b, i, 0))
    rot = pl.BlockSpec((TM, DK // 2), lambda b, i: (i, 0))
    act = jax.ShapeDtypeStruct((BATCH, SEQ, D), bf16)
    return pl.pallas_call(
        _inproj_kernel,
        out_shape=(act, act, act, act, act, act, act),
        grid=(BATCH, NT),
        in_specs=[
            row,
            pl.BlockSpec((N_MOD, None, 1, D), lambda b, i: (0, b, 0, 0)),
            pl.BlockSpec((1, D), lambda b, i: (0, 0)),
            rot, rot,
            _layer_resident((D, D_REST), layer),
        ],
        out_specs=(row, row, row, row, row, row, row),
        compiler_params=_params("arbitrary", "arbitrary"),
        name="in_projection",
    )(x, mod, gain, cos, sin, w_rest)


def _fourier_kernel(h_ref, wr_ref, wi_ref, m_ref, tr_ref, ti_ref, sr_ref, si_ref):
    hb = jnp.swapaxes(h_ref[...], 0, 1).reshape(FFT_SLABS * R, D)
    zr = jnp.dot(hb, wr_ref[...], preferred_element_type=f32).astype(bf16)
    zi = jnp.dot(hb, wi_ref[...], preferred_element_type=f32).astype(bf16)
    for s in range(FFT_SLABS):
        z = jnp.concatenate([zr[s * R:(s + 1) * R], zi[s * R:(s + 1) * R]], axis=0)
        t = jnp.dot(m_ref[s].astype(bf16), z, preferred_element_type=f32)
        sr_ref[s] = t[:R].astype(bf16)
        si_ref[s] = t[R:].astype(bf16)
    tr_ref[...] = jnp.swapaxes(sr_ref[...], 0, 1)
    ti_ref[...] = jnp.swapaxes(si_ref[...], 0, 1)


def _fourier_call(layer, hb, wr, wi):
    nsteps = R // FFT_SLABS
    slab = pl.BlockSpec((None, R, FFT_SLABS, D), lambda b, j: (b, 0, j, 0))
    t2 = jax.ShapeDtypeStruct((BATCH, R, R, D), bf16)
    return pl.pallas_call(
        _fourier_kernel,
        out_shape=(t2, t2),
        grid=(BATCH, nsteps),
        in_specs=[slab,
                  _layer_resident((D, D), layer), _layer_resident((D, D), layer),
                  pl.BlockSpec((FFT_SLABS, 2 * R, 2 * R), lambda b, j: (j, 0, 0))],
        out_specs=(slab, slab),
        scratch_shapes=[pltpu.VMEM((FFT_SLABS, R, D), bf16), pltpu.VMEM((FFT_SLABS, R, D), bf16)],
        compiler_params=_params("arbitrary", "arbitrary"),
        name="fourier_projection_dft1",
    )(hb.reshape(BATCH, R, R, D), wr, wi, jnp.asarray(_STAGE1))


def _dft2_kernel(tr_ref, ti_ref, m_ref, y_ref, s_ref):
    m = m_ref[...].astype(bf16)
    for s in range(DFT2_SLABS):
        t = jnp.concatenate([tr_ref[s], ti_ref[s]], axis=0)
        s_ref[s] = jnp.dot(m, t, preferred_element_type=f32)
    y_ref[...] = jnp.swapaxes(s_ref[...], 0, 1).astype(bf16)


def _dft2_call(tr, ti):
    nsteps = R // DFT2_SLABS
    tin = pl.BlockSpec((None, DFT2_SLABS, R, D), lambda b, j: (b, j, 0, 0))
    yout = pl.BlockSpec((None, R, DFT2_SLABS, D), lambda b, j: (b, 0, j, 0))
    return pl.pallas_call(
        _dft2_kernel,
        out_shape=jax.ShapeDtypeStruct((BATCH, R, R, D), bf16),
        grid=(BATCH, nsteps),
        in_specs=[tin, tin, _resident((R, 2 * R))],
        out_specs=yout,
        scratch_shapes=[pltpu.VMEM((DFT2_SLABS, R, D), f32)],
        compiler_params=_params("arbitrary", "arbitrary"),
        name="dft_stage2",
    )(tr, ti, jnp.asarray(_STAGE2))


def _ret_kernel(q_ref, k_ref, v_ref, m_ref, dec_ref,
                o_ref, sf_ref, sb_ref, sb_all_ref, decb_ref):
    p = pl.program_id(1)
    j = pl.program_id(2)
    n_chunks = RET_ROWS // RET_CHUNK

    @pl.when(j == 0)
    def _():
        sf_ref[...] = jnp.zeros_like(sf_ref)
        sb_ref[...] = jnp.zeros_like(sb_ref)
        decb_ref[...] = dec_ref[...].astype(bf16)

    @pl.when(p == 0)
    def _():
        block = pl.num_programs(2) - 1 - j
        for c in reversed(range(n_chunks)):
            rows = slice(c * RET_CHUNK, (c + 1) * RET_CHUNK)
            for h in range(HEADS):
                cols = slice(h * DK, (h + 1) * DK)
                state = sb_ref[h]
                sb_all_ref[block * n_chunks + c, h] = state.astype(bf16)
                k_dec = k_ref[rows, cols] * decb_ref[_KDB, h]
                sb_ref[h] = state * float(_CDB[h]) + lax.dot_general(
                    k_dec, v_ref[rows, cols], _TN_DIMS, preferred_element_type=f32)

    @pl.when(p == 1)
    def _():
        for c in range(n_chunks):
            rows = slice(c * RET_CHUNK, (c + 1) * RET_CHUNK)
            for h in range(HEADS):
                cols = slice(h * DK, (h + 1) * DK)
                q = q_ref[rows, cols]
                k = k_ref[rows, cols]
                v = v_ref[rows, cols]
                scores = lax.dot_general(q, k, _NT_DIMS, preferred_element_type=f32) * m_ref[h]
                state = sf_ref[h]
                lhs = jnp.concatenate([scores.astype(bf16), q * decb_ref[_QDF, h],
                                       q * decb_ref[_QDB, h]], axis=1)
                rhs = jnp.concatenate([v, state.astype(bf16), sb_all_ref[j * n_chunks + c, h]],
                                      axis=0)
                y = jnp.dot(lhs, rhs, preferred_element_type=f32)
                k_dec = k * decb_ref[_KDF, h]
                sf_ref[h] = state * float(_CDF[h]) + lax.dot_general(
                    k_dec, v, _TN_DIMS, preferred_element_type=f32)
                o_ref[rows, cols] = y.astype(bf16)


def _ret_call(q, k, v):
    nb = SEQ // RET_ROWS

    def scan_idx(b, p, j):
        return (b, j + (1 - p) * (nb - 1 - 2 * j), 0)

    def finish_idx(b, p, j):
        return (b, p * j, 0)

    scan = pl.BlockSpec((None, RET_ROWS, D), scan_idx)
    fin = pl.BlockSpec((None, RET_ROWS, D), finish_idx)
    return pl.pallas_call(
        _ret_kernel,
        out_shape=jax.ShapeDtypeStruct((BATCH, SEQ, D), bf16),
        grid=(BATCH, 2, nb),
        in_specs=[fin, scan, scan, _resident((HEADS, RET_CHUNK, RET_CHUNK)),
                  _resident((4, HEADS, RET_CHUNK, DK))],
        out_specs=fin,
        scratch_shapes=[pltpu.VMEM((HEADS, DK, DK), f32), pltpu.VMEM((HEADS, DK, DK), f32),
                        pltpu.VMEM((SEQ // RET_CHUNK, HEADS, DK, DK), bf16),
                        pltpu.VMEM((4, HEADS, RET_CHUNK, DK), bf16)],
        compiler_params=_params("arbitrary", "arbitrary", "arbitrary"),
        name="retention",
    )(q, k, v, jnp.asarray(_RMASK), jnp.asarray(_RDEC))


def _merge_kernel(x_ref, mod_ref, yf_ref, yr_ref, gs_ref, af_ref, ar_ref, wf_ref, wr_ref, wo_ref,
                  o_ref):
    a = jnp.dot(yf_ref[...], wf_ref[...], preferred_element_type=f32)
    gated = []
    for h in range(HEADS):
        cols = slice(h * DK, (h + 1) * DK)
        y = yr_ref[:, cols].astype(f32)
        mu = jnp.mean(y, axis=-1, keepdims=True)
        yc = y - mu
        var = jnp.mean(yc * yc, axis=-1, keepdims=True)
        yn = yc * lax.rsqrt(var + EPS)
        gated.append((yn * gs_ref[:, cols].astype(f32)).astype(bf16))
    b = jnp.dot(jnp.concatenate(gated, axis=1), wr_ref[...], preferred_element_type=f32)
    merged = af_ref[...].astype(f32) * a + ar_ref[...].astype(f32) * b
    o = jnp.dot(merged.astype(bf16), wo_ref[...], preferred_element_type=f32)
    o_ref[...] = x_ref[...] + mod_ref[2] * o


def _merge_call(layer, x, mod, yf, yr, gs, af, ar, wf, wr, wo):
    row = pl.BlockSpec((None, TM, D), lambda b, i: (b, i, 0))
    return pl.pallas_call(
        _merge_kernel,
        out_shape=jax.ShapeDtypeStruct((BATCH, SEQ, D), f32),
        grid=(BATCH, NT),
        in_specs=[row, pl.BlockSpec((N_MOD, None, 1, D), lambda b, i: (0, b, 0, 0)),
                  row, row, row, row, row,
                  _layer_resident((D, D), layer), _layer_resident((D, D), layer),
                  _layer_resident((D, D), layer)],
        out_specs=row,
        compiler_params=_params("arbitrary", "arbitrary"),
        name="merge_out_projection",
    )(x, mod, yf, yr, gs, af, ar, wf, wr, wo)


def _ffn_kernel(x_ref, xp_ref, xn_ref, mod_ref, g_ref, wup_ref, cw_ref, cb_ref, wdn_ref, fg_ref,
                o_ref, act_ref, *, final):
    i = pl.program_id(1)
    rows = TM + 2 * HALO
    gain, shift, scale = g_ref[...], mod_ref[3], mod_ref[4]
    keep_prev = (i > 0).astype(f32)
    keep_next = (i < pl.num_programs(1) - 1).astype(f32)
    h = jnp.concatenate([
        _modulated_norm(xp_ref[...], gain, shift, scale) * keep_prev,
        _modulated_norm(x_ref[...], gain, shift, scale),
        _modulated_norm(xn_ref[...], gain, shift, scale) * keep_next,
    ], axis=0).astype(bf16)

    def conv(col):
        u = jnp.dot(h, wup_ref[:, col:col + FF_TILE], preferred_element_type=f32)
        w = cw_ref[:, col:col + FF_TILE]
        full = (pltpu.roll(u, 1, 0) * w[0:1] + u * w[1:2] + pltpu.roll(u, rows - 1, 0) * w[2:3]
                + cb_ref[:, col:col + FF_TILE])
        return full[HALO:HALO + TM]

    for t in range(D_FF // FF_TILE):
        a = conv(t * FF_TILE)
        b = conv(D_FF + t * FF_TILE)
        gelu = 0.5 * a * (1.0 + lax.erf(a * (2.0 ** -0.5)))
        act_ref[:, t * FF_TILE:(t + 1) * FF_TILE] = (gelu * b).astype(bf16)

    y = jnp.dot(act_ref[...], wdn_ref[...], preferred_element_type=f32)
    out = x_ref[...] + mod_ref[5] * y
    if final:
        ms = jnp.mean(out * out, axis=-1, keepdims=True)
        out = out * lax.rsqrt(ms + EPS) * fg_ref[...]
    o_ref[...] = out


def _ffn_call(layer, x, mod, gain, wup, cw, cb, wdn, final_g, final):
    tiles = TM // HALO
    row = pl.BlockSpec((None, TM, D), lambda b, i: (b, i, 0))
    prev = pl.BlockSpec((None, HALO, D), lambda b, i: (b, jnp.maximum(i * tiles - 1, 0), 0))
    nxt = pl.BlockSpec((None, HALO, D),
                       lambda b, i: (b, jnp.minimum((i + 1) * tiles, SEQ // HALO - 1), 0))
    return pl.pallas_call(
        functools.partial(_ffn_kernel, final=final),
        out_shape=jax.ShapeDtypeStruct((BATCH, SEQ, D), f32),
        grid=(BATCH, NT),
        in_specs=[row, prev, nxt,
                  pl.BlockSpec((N_MOD, None, 1, D), lambda b, i: (0, b, 0, 0)),
                  pl.BlockSpec((1, D), lambda b, i: (0, 0)),
                  _layer_resident((D, 2 * D_FF), layer), _layer_resident((3, 2 * D_FF), layer),
                  _layer_resident((1, 2 * D_FF), layer), _layer_resident((D_FF, D), layer),
                  pl.BlockSpec((1, D), lambda b, i: (0, 0))],
        out_specs=row,
        scratch_shapes=[pltpu.VMEM((TM, D_FF), bf16)],
        compiler_params=_params("arbitrary", "arbitrary"),
        name="conv_ffn_final" if final else "conv_ffn",
    )(x, x, x, mod, gain, wup, cw, cb, wdn, final_g)


def kernel(x, c, norm1_g, norm2_g, ada_w, ada_b, w_in, w_fourier, w_ret, w_out,
           ffn_up, conv_w, conv_b, ffn_down, final_g):
    assert x.shape == (BATCH, SEQ, D) and c.shape == (BATCH, D)
    mod_all = _ada_call(c, ada_w, ada_b).reshape(DEPTH, N_MOD, BATCH, 1, D)
    wfr_all, wfi_all, w_rest = _prep_call(w_in)
    cos, sin = _rotary_tables()
    final_gain = final_g.reshape(1, D)
    wf, wr, wo = w_fourier.astype(bf16), w_ret.astype(bf16), w_out.astype(bf16)
    wup, wdn = ffn_up.astype(bf16), ffn_down.astype(bf16)

    for l in range(DEPTH):
        mod = mod_all[l]
        gain1 = norm1_g[l].reshape(1, D)
        hb, q, k, v, gs, af, ar = _inproj_call(l, x, mod, gain1, cos, sin, w_rest)
        tr, ti = _fourier_call(l, hb, wfr_all, wfi_all)
        yf = _dft2_call(tr, ti).reshape(BATCH, SEQ, D)
        yr = _ret_call(q, k, v)
        x = _merge_call(l, x, mod, yf, yr, gs, af, ar, wf, wr, wo)
        x = _ffn_call(l, x, mod, norm2_g[l].reshape(1, D), wup, conv_w,
                      conv_b.reshape(DEPTH, 1, 2 * D_FF), wdn, final_gain,
                      final=(l == DEPTH - 1))
    return x
```

```python
import functools

import numpy as np
import jax
import jax.numpy as jnp
from jax import lax
from jax.experimental import pallas as pl
from jax.experimental.pallas import tpu as pltpu

f32 = jnp.float32
bf16 = jnp.bfloat16

D = 1024
BATCH = 8
SEQ = 4096
DEPTH = 4
GROUP = 128
HEADS = 4
DK = D // HEADS
ROPE_BASE = 10000.0
D_FF = 2816
N_MOD = 6
EPS = 1e-6
D_REST = 6 * D

R = 64
TM = 1024
NT = SEQ // TM
RET_CHUNK = 256
RET_ROWS = 2048
FFT_SLABS = 32
DFT2_SLABS = 32
FF_TILE = 256
HALO = 8
VMEM_LIMIT = 56 * 1024 * 1024

_NT_DIMS = (((1,), (1,)), ((), ()))
_TN_DIMS = (((0,), (0,)), ((), ()))


def _resident(shape):
    nd = len(shape)
    return pl.BlockSpec(shape, lambda *_: (0,) * nd, pipeline_mode=pl.Buffered(1))


def _layer_resident(shape, layer):
    nd = len(shape)
    return pl.BlockSpec((None,) + tuple(shape), lambda *_: (layer,) + (0,) * nd,
                        pipeline_mode=pl.Buffered(1))


def _params(*sem):
    return pltpu.CompilerParams(dimension_semantics=sem, vmem_limit_bytes=VMEM_LIMIT)


def _group_dft_tables():
    n = np.arange(GROUP)
    ang = 2.0 * np.pi * np.outer(n, n) / GROUP
    s = 1.0 / np.sqrt(GROUP)
    return (np.cos(ang) * s).astype(np.float32), (-np.sin(ang) * s).astype(np.float32)


def _stage_tables():
    k = np.arange(R)
    ang = 2.0 * np.pi * np.outer(k, k) / R
    cr = np.cos(ang) / 8.0
    ci = -np.sin(ang) / 8.0
    stage2 = np.concatenate([cr, -ci], axis=1)
    n2 = np.arange(R)[:, None, None]
    k1 = np.arange(R)[None, :, None]
    n1 = np.arange(R)[None, None, :]
    a = np.exp(-2j * np.pi * (n2 * k1 / SEQ + n1 * k1 / R)) / 8.0
    stage1 = np.concatenate([np.concatenate([a.real, -a.imag], axis=2),
                             np.concatenate([a.imag, a.real], axis=2)], axis=1)
    return stage1.astype(np.float32), stage2.astype(np.float32)


def _retention_tables():
    c = RET_CHUNK
    j = np.arange(c, dtype=np.float64)
    diff = j[:, None] - j[None, :]
    ones = np.ones((1, DK))
    mask = np.zeros((HEADS, c, c))
    qdf, qdb, kdf, kdb = (np.zeros((HEADS, c, DK)) for _ in range(4))
    cdf, cdb = np.zeros(HEADS), np.zeros(HEADS)
    for h in range(HEADS):
        lf = np.log1p(-np.exp2(-5.0 - h))
        lb = np.log1p(-np.exp2(-5.5 - h))
        mask[h] = np.where(diff >= 0, np.exp(lf * np.maximum(diff, 0.0)),
                           np.exp(lb * np.maximum(-diff, 0.0)))
        qdf[h] = np.exp(lf * (j + 1.0))[:, None] * ones
        kdf[h] = np.exp(lf * (c - 1.0 - j))[:, None] * ones
        qdb[h] = np.exp(lb * (c - j))[:, None] * ones
        kdb[h] = np.exp(lb * j)[:, None] * ones
        cdf[h] = np.exp(lf * c)
        cdb[h] = np.exp(lb * c)
    dec = np.stack([qdf, qdb, kdf, kdb])
    return mask.astype(np.float32), dec.astype(np.float32), cdf, cdb


_GROUP_TABLE = np.concatenate(_group_dft_tables(), axis=1)
_STAGE1, _STAGE2 = _stage_tables()
_RMASK, _RDEC, _CDF, _CDB = _retention_tables()
_QDF, _QDB, _KDF, _KDB = range(4)


def _rotary_tables():
    half = DK // 2
    inv_freq = ROPE_BASE ** (-jnp.arange(half, dtype=f32) / half)
    ang = jnp.arange(SEQ, dtype=f32)[:, None] * inv_freq[None, :]
    return jnp.cos(ang), jnp.sin(ang)


def _ada_kernel(c_ref, w_ref, b_ref, o_ref):
    c = c_ref[...]
    act = (c * jax.nn.sigmoid(c)).astype(bf16)
    o_ref[...] = jnp.dot(act, w_ref[...].astype(bf16), preferred_element_type=f32) + b_ref[...]


def _ada_call(c, ada_w, ada_b):
    return pl.pallas_call(
        _ada_kernel,
        out_shape=jax.ShapeDtypeStruct((DEPTH, N_MOD, BATCH, D), f32),
        grid=(DEPTH, N_MOD),
        in_specs=[
            pl.BlockSpec((BATCH, D), lambda l, j: (0, 0)),
            pl.BlockSpec((None, D, D), lambda l, j: (l, 0, j)),
            pl.BlockSpec((None, None, 1, D), lambda l, j: (l, j, 0, 0)),
        ],
        out_specs=pl.BlockSpec((None, None, BATCH, D), lambda l, j: (l, j, 0, 0)),
        compiler_params=_params("arbitrary", "arbitrary"),
        name="ada_mod",
    )(c, ada_w, ada_b.reshape(DEPTH, N_MOD, 1, D))


def _prep_kernel(w_ref, t_ref, wr_ref, wi_ref, rest_ref):
    j = pl.program_id(1)

    @pl.when(j == 0)
    def _():
        table = t_ref[...]
        t_hi = table.astype(bf16)
        t_lo = (table - t_hi.astype(f32)).astype(bf16)
        for g in range(D // GROUP):
            cols = slice(g * GROUP, (g + 1) * GROUP)
            w = w_ref[:, cols]
            hi = w.astype(bf16)
            lo = (w - hi.astype(f32)).astype(bf16)
            folded = (jnp.dot(hi, t_hi, preferred_element_type=f32)
                      + jnp.dot(lo, t_hi, preferred_element_type=f32)
                      + jnp.dot(hi, t_lo, preferred_element_type=f32))
            wr_ref[:, cols] = folded[:, :GROUP].astype(bf16)
            wi_ref[:, cols] = folded[:, GROUP:].astype(bf16)

    @pl.when(j > 0)
    def _():
        rest_ref[...] = w_ref[...].astype(bf16)


def _prep_call(w_in):
    sq = jax.ShapeDtypeStruct((DEPTH, D, D), bf16)
    tab = pl.BlockSpec((GROUP, 2 * GROUP), lambda l, j: (0, 0))
    fold_out = pl.BlockSpec((None, D, D), lambda l, j: (l, 0, 0))
    return pl.pallas_call(
        _prep_kernel,
        out_shape=(sq, sq, jax.ShapeDtypeStruct((DEPTH, D, D_REST), bf16)),
        grid=(DEPTH, 1 + D_REST // D),
        in_specs=[pl.BlockSpec((None, D, D), lambda l, j: (l, 0, j)), tab],
        out_specs=(fold_out, fold_out,
                   pl.BlockSpec((None, D, D), lambda l, j: (l, 0, jnp.maximum(j - 1, 0)))),
        compiler_params=_params("arbitrary", "arbitrary"),
        name="weight_prep",
    )(w_in, jnp.asarray(_GROUP_TABLE))


def _modulated_norm(x, gain, shift, scale):
    ms = jnp.mean(x * x, axis=-1, keepdims=True)
    return (x * lax.rsqrt(ms + EPS) * gain) * (1.0 + scale) + shift


def _inproj_kernel(x_ref, mod_ref, g_ref, cos_ref, sin_ref, w_ref,
                   hb_ref, q_ref, k_ref, v_ref, gs_ref, af_ref, ar_ref):
    hb = _modulated_norm(x_ref[...], g_ref[...], mod_ref[0], mod_ref[1]).astype(bf16)
    hb_ref[...] = hb
    cos = cos_ref[...]
    sin = sin_ref[...]
    half = DK // 2

    def rotary(sec, o_ref, scale):
        p = jnp.dot(hb, w_ref[:, sec * D:(sec + 1) * D], preferred_element_type=f32)
        for h in range(HEADS):
            t1 = p[:, h * DK:h * DK + half]
            t2 = p[:, h * DK + half:(h + 1) * DK]
            o_ref[:, h * DK:h * DK + half] = ((t1 * cos - t2 * sin) * scale).astype(bf16)
            o_ref[:, h * DK + half:(h + 1) * DK] = ((t1 * sin + t2 * cos) * scale).astype(bf16)

    rotary(0, q_ref, DK ** -0.5)
    rotary(1, k_ref, 1.0)
    g = jnp.dot(hb, w_ref[:, 3 * D:4 * D], preferred_element_type=f32)
    gs_ref[...] = (g * jax.nn.sigmoid(g)).astype(bf16)
    af = jnp.dot(hb, w_ref[:, 4 * D:5 * D], preferred_element_type=f32)
    af_ref[...] = jax.nn.sigmoid(af).astype(bf16)
    ar = jnp.dot(hb, w_ref[:, 5 * D:6 * D], preferred_element_type=f32)
    ar_ref[...] = jax.nn.sigmoid(ar).astype(bf16)
    v_ref[...] = jnp.dot(hb, w_ref[:, 2 * D:3 * D], preferred_element_type=f32).astype(bf16)


def _inproj_call(layer, x, mod, gain, cos, sin, w_rest):
    row = pl.BlockSpec((None, TM, D), lambda b, i: (b, i, 0))
    rot = pl.BlockSpec((TM, DK // 2), lambda b, i: (i, 0))
    act = jax.ShapeDtypeStruct((BATCH, SEQ, D), bf16)
    return pl.pallas_call(
        _inproj_kernel,
        out_shape=(act, act, act, act, act, act, act),
        grid=(BATCH, NT),
        in_specs=[
            row,
            pl.BlockSpec((N_MOD, None, 1, D), lambda b, i: (0, b, 0, 0)),
            pl.BlockSpec((1, D), lambda b, i: (0, 0)),
            rot, rot,
            _layer_resident((D, D_REST), layer),
        ],
        out_specs=(row, row, row, row, row, row, row),
        compiler_params=_params("arbitrary", "arbitrary"),
        name="in_projection",
    )(x, mod, gain, cos, sin, w_rest)


def _fourier_kernel(h_ref, wr_ref, wi_ref, m_ref, tr_ref, ti_ref, sr_ref, si_ref):
    hb = jnp.swapaxes(h_ref[...], 0, 1).reshape(FFT_SLABS * R, D)
    zr = jnp.dot(hb, wr_ref[...], preferred_element_type=f32).astype(bf16)
    zi = jnp.dot(hb, wi_ref[...], preferred_element_type=f32).astype(bf16)
    for s in range(FFT_SLABS):
        z = jnp.concatenate([zr[s * R:(s + 1) * R], zi[s * R:(s + 1) * R]], axis=0)
        t = jnp.dot(m_ref[s].astype(bf16), z, preferred_element_type=f32)
        sr_ref[s] = t[:R].astype(bf16)
        si_ref[s] = t[R:].astype(bf16)
    tr_ref[...] = jnp.swapaxes(sr_ref[...], 0, 1)
    ti_ref[...] = jnp.swapaxes(si_ref[...], 0, 1)


def _fourier_call(layer, hb, wr, wi):
    nsteps = R // FFT_SLABS
    slab = pl.BlockSpec((None, R, FFT_SLABS, D), lambda b, j: (b, 0, j, 0))
    t2 = jax.ShapeDtypeStruct((BATCH, R, R, D), bf16)
    return pl.pallas_call(
        _fourier_kernel,
        out_shape=(t2, t2),
        grid=(BATCH, nsteps),
        in_specs=[slab,
                  _layer_resident((D, D), layer), _layer_resident((D, D), layer),
                  pl.BlockSpec((FFT_SLABS, 2 * R, 2 * R), lambda b, j: (j, 0, 0))],
        out_specs=(slab, slab),
        scratch_shapes=[pltpu.VMEM((FFT_SLABS, R, D), bf16), pltpu.VMEM((FFT_SLABS, R, D), bf16)],
        compiler_params=_params("arbitrary", "arbitrary"),
        name="fourier_projection_dft1",
    )(hb.reshape(BATCH, R, R, D), wr, wi, jnp.asarray(_STAGE1))


def _dft2_kernel(tr_ref, ti_ref, m_ref, y_ref, s_ref):
    m = m_ref[...].astype(bf16)
    for s in range(DFT2_SLABS):
        t = jnp.concatenate([tr_ref[s], ti_ref[s]], axis=0)
        s_ref[s] = jnp.dot(m, t, preferred_element_type=f32).astype(bf16)
    y_ref[...] = jnp.swapaxes(s_ref[...], 0, 1)


def _dft2_call(tr, ti):
    nsteps = R // DFT2_SLABS
    tin = pl.BlockSpec((None, DFT2_SLABS, R, D), lambda b, j: (b, j, 0, 0))
    yout = pl.BlockSpec((None, R, DFT2_SLABS, D), lambda b, j: (b, 0, j, 0))
    return pl.pallas_call(
        _dft2_kernel,
        out_shape=jax.ShapeDtypeStruct((BATCH, R, R, D), bf16),
        grid=(BATCH, nsteps),
        in_specs=[tin, tin, _resident((R, 2 * R))],
        out_specs=yout,
        scratch_shapes=[pltpu.VMEM((DFT2_SLABS, R, D), bf16)],
        compiler_params=_params("arbitrary", "arbitrary"),
        name="dft_stage2",
    )(tr, ti, jnp.asarray(_STAGE2))


def _ret_kernel(q_ref, k_ref, v_ref, m_ref, dec_ref,
                o_ref, sf_ref, sb_ref, sb_all_ref, decb_ref):
    p = pl.program_id(1)
    j = pl.program_id(2)
    n_chunks = RET_ROWS // RET_CHUNK

    @pl.when(j == 0)
    def _():
        sf_ref[...] = jnp.zeros_like(sf_ref)
        sb_ref[...] = jnp.zeros_like(sb_ref)
        decb_ref[...] = dec_ref[...].astype(bf16)

    @pl.when(p == 0)
    def _():
        block = pl.num_programs(2) - 1 - j
        for c in reversed(range(n_chunks)):
            rows = slice(c * RET_CHUNK, (c + 1) * RET_CHUNK)
            for h in range(HEADS):
                cols = slice(h * DK, (h + 1) * DK)
                state = sb_ref[h]
                sb_all_ref[block * n_chunks + c, h] = state.astype(bf16)
                k_dec = k_ref[rows, cols] * decb_ref[_KDB, h]
                sb_ref[h] = state * float(_CDB[h]) + lax.dot_general(
                    k_dec, v_ref[rows, cols], _TN_DIMS, preferred_element_type=f32)

    @pl.when(p == 1)
    def _():
        for c in range(n_chunks):
            rows = slice(c * RET_CHUNK, (c + 1) * RET_CHUNK)
            for h in range(HEADS):
                cols = slice(h * DK, (h + 1) * DK)
                q = q_ref[rows, cols]
                k = k_ref[rows, cols]
                v = v_ref[rows, cols]
                scores = lax.dot_general(q, k, _NT_DIMS, preferred_element_type=f32) * m_ref[h]
                state = sf_ref[h]
                lhs = jnp.concatenate([scores.astype(bf16), q * decb_ref[_QDF, h],
                                       q * decb_ref[_QDB, h]], axis=1)
                rhs = jnp.concatenate([v, state.astype(bf16), sb_all_ref[j * n_chunks + c, h]],
                                      axis=0)
                y = jnp.dot(lhs, rhs, preferred_element_type=f32)
                k_dec = k * decb_ref[_KDF, h]
                sf_ref[h] = state * float(_CDF[h]) + lax.dot_general(
                    k_dec, v, _TN_DIMS, preferred_element_type=f32)
                o_ref[rows, cols] = y.astype(bf16)


def _ret_call(q, k, v):
    nb = SEQ // RET_ROWS

    def scan_idx(b, p, j):
        return (b, j + (1 - p) * (nb - 1 - 2 * j), 0)

    def finish_idx(b, p, j):
        return (b, p * j, 0)

    scan = pl.BlockSpec((None, RET_ROWS, D), scan_idx)
    fin = pl.BlockSpec((None, RET_ROWS, D), finish_idx)
    return pl.pallas_call(
        _ret_kernel,
        out_shape=jax.ShapeDtypeStruct((BATCH, SEQ, D), bf16),
        grid=(BATCH, 2, nb),
        in_specs=[fin, scan, scan, _resident((HEADS, RET_CHUNK, RET_CHUNK)),
                  _resident((4, HEADS, RET_CHUNK, DK))],
        out_specs=fin,
        scratch_shapes=[pltpu.VMEM((HEADS, DK, DK), f32), pltpu.VMEM((HEADS, DK, DK), f32),
                        pltpu.VMEM((SEQ // RET_CHUNK, HEADS, DK, DK), bf16),
                        pltpu.VMEM((4, HEADS, RET_CHUNK, DK), bf16)],
        compiler_params=_params("arbitrary", "arbitrary", "arbitrary"),
        name="retention",
    )(q, k, v, jnp.asarray(_RMASK), jnp.asarray(_RDEC))


def _merge_kernel(x_ref, mod_ref, yf_ref, yr_ref, gs_ref, af_ref, ar_ref, wf_ref, wr_ref, wo_ref,
                  o_ref):
    a = jnp.dot(yf_ref[...], wf_ref[...], preferred_element_type=f32)
    gated = []
    for h in range(HEADS):
        cols = slice(h * DK, (h + 1) * DK)
        y = yr_ref[:, cols].astype(f32)
        mu = jnp.mean(y, axis=-1, keepdims=True)
        yc = y - mu
        var = jnp.mean(yc * yc, axis=-1, keepdims=True)
        yn = yc * lax.rsqrt(var + EPS)
        gated.append((yn * gs_ref[:, cols].astype(f32)).astype(bf16))
    b = jnp.dot(jnp.concatenate(gated, axis=1), wr_ref[...], preferred_element_type=f32)
    merged = af_ref[...].astype(f32) * a + ar_ref[...].astype(f32) * b
    o = jnp.dot(merged.astype(bf16), wo_ref[...], preferred_element_type=f32)
    o_ref[...] = x_ref[...] + mod_ref[2] * o


def _merge_call(layer, x, mod, yf, yr, gs, af, ar, wf, wr, wo):
    row = pl.BlockSpec((None, TM, D), lambda b, i: (b, i, 0))
    return pl.pallas_call(
        _merge_kernel,
        out_shape=jax.ShapeDtypeStruct((BATCH, SEQ, D), f32),
        grid=(BATCH, NT),
        in_specs=[row, pl.BlockSpec((N_MOD, None, 1, D), lambda b, i: (0, b, 0, 0)),
                  row, row, row, row, row,
                  _layer_resident((D, D), layer), _layer_resident((D, D), layer),
                  _layer_resident((D, D), layer)],
        out_specs=row,
        compiler_params=_params("arbitrary", "arbitrary"),
        name="merge_out_projection",
    )(x, mod, yf, yr, gs, af, ar, wf, wr, wo)


def _ffn_kernel(x_ref, xp_ref, xn_ref, mod_ref, g_ref, wup_ref, cw_ref, cb_ref, wdn_ref, fg_ref,
                o_ref, act_ref, *, final):
    i = pl.program_id(1)
    rows = TM + 2 * HALO
    gain, shift, scale = g_ref[...], mod_ref[3], mod_ref[4]
    keep_prev = (i > 0).astype(f32)
    keep_next = (i < pl.num_programs(1) - 1).astype(f32)
    h = jnp.concatenate([
        _modulated_norm(xp_ref[...], gain, shift, scale) * keep_prev,
        _modulated_norm(x_ref[...], gain, shift, scale),
        _modulated_norm(xn_ref[...], gain, shift, scale) * keep_next,
    ], axis=0).astype(bf16)

    def conv(col):
        u = jnp.dot(h, wup_ref[:, col:col + FF_TILE], preferred_element_type=f32)
        w = cw_ref[:, col:col + FF_TILE]
        full = (pltpu.roll(u, 1, 0) * w[0:1] + u * w[1:2] + pltpu.roll(u, rows - 1, 0) * w[2:3]
                + cb_ref[:, col:col + FF_TILE])
        return full[HALO:HALO + TM]

    for t in range(D_FF // FF_TILE):
        a = conv(t * FF_TILE)
        b = conv(D_FF + t * FF_TILE)
        gelu = 0.5 * a * (1.0 + lax.erf(a * (2.0 ** -0.5)))
        act_ref[:, t * FF_TILE:(t + 1) * FF_TILE] = (gelu * b).astype(bf16)

    y = jnp.dot(act_ref[...], wdn_ref[...], preferred_element_type=f32)
    out = x_ref[...] + mod_ref[5] * y
    if final:
        ms = jnp.mean(out * out, axis=-1, keepdims=True)
        out = out * lax.rsqrt(ms + EPS) * fg_ref[...]
    o_ref[...] = out


def _ffn_call(layer, x, mod, gain, wup, cw, cb, wdn, final_g, final):
    tiles = TM // HALO
    row = pl.BlockSpec((None, TM, D), lambda b, i: (b, i, 0))
    prev = pl.BlockSpec((None, HALO, D), lambda b, i: (b, jnp.maximum(i * tiles - 1, 0), 0))
    nxt = pl.BlockSpec((None, HALO, D),
                       lambda b, i: (b, jnp.minimum((i + 1) * tiles, SEQ // HALO - 1), 0))
    return pl.pallas_call(
        functools.partial(_ffn_kernel, final=final),
        out_shape=jax.ShapeDtypeStruct((BATCH, SEQ, D), f32),
        grid=(BATCH, NT),
        in_specs=[row, prev, nxt,
                  pl.BlockSpec((N_MOD, None, 1, D), lambda b, i: (0, b, 0, 0)),
                  pl.BlockSpec((1, D), lambda b, i: (0, 0)),
                  _layer_resident((D, 2 * D_FF), layer), _layer_resident((3, 2 * D_FF), layer),
                  _layer_resident((1, 2 * D_FF), layer), _layer_resident((D_FF, D), layer),
                  pl.BlockSpec((1, D), lambda b, i: (0, 0))],
        out_specs=row,
        scratch_shapes=[pltpu.VMEM((TM, D_FF), bf16)],
        compiler_params=_params("arbitrary", "arbitrary"),
        name="conv_ffn_final" if final else "conv_ffn",
    )(x, x, x, mod, gain, wup, cw, cb, wdn, final_g)


def kernel(x, c, norm1_g, norm2_g, ada_w, ada_b, w_in, w_fourier, w_ret, w_out,
           ffn_up, conv_w, conv_b, ffn_down, final_g):
    assert x.shape == (BATCH, SEQ, D) and c.shape == (BATCH, D)
    mod_all = _ada_call(c, ada_w, ada_b).reshape(DEPTH, N_MOD, BATCH, 1, D)
    wfr_all, wfi_all, w_rest = _prep_call(w_in)
    cos, sin = _rotary_tables()
    final_gain = final_g.reshape(1, D)
    wf, wr, wo = w_fourier.astype(bf16), w_ret.astype(bf16), w_out.astype(bf16)
    wup, wdn = ffn_up.astype(bf16), ffn_down.astype(bf16)

    for l in range(DEPTH):
        mod = mod_all[l]
        gain1 = norm1_g[l].reshape(1, D)
        hb, q, k, v, gs, af, ar = _inproj_call(l, x, mod, gain1, cos, sin, w_rest)
        tr, ti = _fourier_call(l, hb, wfr_all, wfi_all)
        yf = _dft2_call(tr, ti).reshape(BATCH, SEQ, D)
        yr = _ret_call(q, k, v)
        x = _merge_call(l, x, mod, yf, yr, gs, af, ar, wf, wr, wo)
        x = _ffn_call(l, x, mod, norm2_g[l].reshape(1, D), wup, conv_w,
                      conv_b.reshape(DEPTH, 1, 2 * D_FF), wdn, final_gain,
                      final=(l == DEPTH - 1))
    return x
```

```python
import functools

import numpy as np
import jax
import jax.numpy as jnp
from jax import lax
from jax.experimental import pallas as pl
from jax.experimental.pallas import tpu as pltpu

f32 = jnp.float32
bf16 = jnp.bfloat16

D = 1024
BATCH = 8
SEQ = 4096
DEPTH = 4
GROUP = 128
HEADS = 4
DK = D // HEADS
ROPE_BASE = 10000.0
D_FF = 2816
N_MOD = 6
EPS = 1e-6
D_REST = 6 * D

R = 64
TM = 1024
NT = SEQ // TM
RET_CHUNK = 256
RET_ROWS = 1024
FFT_SLABS = 32
DFT2_SLABS = RET_ROWS // R
FF_TILE = 256
HALO = 8
VMEM_LIMIT = 56 * 1024 * 1024

_NT_DIMS = (((1,), (1,)), ((), ()))
_TN_DIMS = (((0,), (0,)), ((), ()))


def _resident(shape):
    nd = len(shape)
    return pl.BlockSpec(shape, lambda *_: (0,) * nd, pipeline_mode=pl.Buffered(1))


def _layer_resident(shape, layer):
    nd = len(shape)
    return pl.BlockSpec((None,) + tuple(shape), lambda *_: (layer,) + (0,) * nd,
                        pipeline_mode=pl.Buffered(1))


def _params(*sem):
    return pltpu.CompilerParams(dimension_semantics=sem, vmem_limit_bytes=VMEM_LIMIT)


def _group_dft_tables():
    n = np.arange(GROUP)
    ang = 2.0 * np.pi * np.outer(n, n) / GROUP
    s = 1.0 / np.sqrt(GROUP)
    return (np.cos(ang) * s).astype(np.float32), (-np.sin(ang) * s).astype(np.float32)


def _stage_tables():
    k = np.arange(R)
    ang = 2.0 * np.pi * np.outer(k, k) / R
    cr = np.cos(ang) / 8.0
    ci = -np.sin(ang) / 8.0
    stage2 = np.concatenate([cr, -ci], axis=1)
    n2 = np.arange(R)[:, None, None]
    k1 = np.arange(R)[None, :, None]
    n1 = np.arange(R)[None, None, :]
    a = np.exp(-2j * np.pi * (n2 * k1 / SEQ + n1 * k1 / R)) / 8.0
    stage1 = np.concatenate([np.concatenate([a.real, -a.imag], axis=2),
                             np.concatenate([a.imag, a.real], axis=2)], axis=1)
    return stage1.astype(np.float32), stage2.astype(np.float32)


def _retention_tables():
    c = RET_CHUNK
    j = np.arange(c, dtype=np.float64)
    diff = j[:, None] - j[None, :]
    ones = np.ones((1, DK))
    mask = np.zeros((HEADS, c, c))
    qdf, qdb, kdf, kdb = (np.zeros((HEADS, c, DK)) for _ in range(4))
    cdf, cdb = np.zeros(HEADS), np.zeros(HEADS)
    for h in range(HEADS):
        lf = np.log1p(-np.exp2(-5.0 - h))
        lb = np.log1p(-np.exp2(-5.5 - h))
        mask[h] = np.where(diff >= 0, np.exp(lf * np.maximum(diff, 0.0)),
                           np.exp(lb * np.maximum(-diff, 0.0)))
        qdf[h] = np.exp(lf * (j + 1.0))[:, None] * ones
        kdf[h] = np.exp(lf * (c - 1.0 - j))[:, None] * ones
        qdb[h] = np.exp(lb * (c - j))[:, None] * ones
        kdb[h] = np.exp(lb * j)[:, None] * ones
        cdf[h] = np.exp(lf * c)
        cdb[h] = np.exp(lb * c)
    dec = np.stack([qdf, qdb, kdf, kdb])
    return mask.astype(np.float32), dec.astype(np.float32), cdf, cdb


_GROUP_TABLE = np.concatenate(_group_dft_tables(), axis=1)
_STAGE1, _STAGE2 = _stage_tables()
_RMASK, _RDEC, _CDF, _CDB = _retention_tables()
_QDF, _QDB, _KDF, _KDB = range(4)


def _rotary_tables():
    half = DK // 2
    inv_freq = ROPE_BASE ** (-jnp.arange(half, dtype=f32) / half)
    ang = jnp.arange(SEQ, dtype=f32)[:, None] * inv_freq[None, :]
    return jnp.cos(ang), jnp.sin(ang)


def _ada_kernel(c_ref, w_ref, b_ref, o_ref):
    c = c_ref[...]
    act = (c * jax.nn.sigmoid(c)).astype(bf16)
    o_ref[...] = jnp.dot(act, w_ref[...].astype(bf16), preferred_element_type=f32) + b_ref[...]


def _ada_call(c, ada_w, ada_b):
    return pl.pallas_call(
        _ada_kernel,
        out_shape=jax.ShapeDtypeStruct((DEPTH, N_MOD, BATCH, D), f32),
        grid=(DEPTH, N_MOD),
        in_specs=[
            pl.BlockSpec((BATCH, D), lambda l, j: (0, 0)),
            pl.BlockSpec((None, D, D), lambda l, j: (l, 0, j)),
            pl.BlockSpec((None, None, 1, D), lambda l, j: (l, j, 0, 0)),
        ],
        out_specs=pl.BlockSpec((None, None, BATCH, D), lambda l, j: (l, j, 0, 0)),
        compiler_params=_params("arbitrary", "arbitrary"),
        name="ada_mod",
    )(c, ada_w, ada_b.reshape(DEPTH, N_MOD, 1, D))


def _prep_kernel(w_ref, t_ref, wr_ref, wi_ref, rest_ref):
    j = pl.program_id(1)

    @pl.when(j == 0)
    def _():
        table = t_ref[...]
        t_hi = table.astype(bf16)
        t_lo = (table - t_hi.astype(f32)).astype(bf16)
        for g in range(D // GROUP):
            cols = slice(g * GROUP, (g + 1) * GROUP)
            w = w_ref[:, cols]
            hi = w.astype(bf16)
            lo = (w - hi.astype(f32)).astype(bf16)
            folded = (jnp.dot(hi, t_hi, preferred_element_type=f32)
                      + jnp.dot(lo, t_hi, preferred_element_type=f32)
                      + jnp.dot(hi, t_lo, preferred_element_type=f32))
            wr_ref[:, cols] = folded[:, :GROUP].astype(bf16)
            wi_ref[:, cols] = folded[:, GROUP:].astype(bf16)

    @pl.when(j > 0)
    def _():
        rest_ref[...] = w_ref[...].astype(bf16)


def _prep_call(w_in):
    sq = jax.ShapeDtypeStruct((DEPTH, D, D), bf16)
    tab = pl.BlockSpec((GROUP, 2 * GROUP), lambda l, j: (0, 0))
    fold_out = pl.BlockSpec((None, D, D), lambda l, j: (l, 0, 0))
    return pl.pallas_call(
        _prep_kernel,
        out_shape=(sq, sq, jax.ShapeDtypeStruct((DEPTH, D, D_REST), bf16)),
        grid=(DEPTH, 1 + D_REST // D),
        in_specs=[pl.BlockSpec((None, D, D), lambda l, j: (l, 0, j)), tab],
        out_specs=(fold_out, fold_out,
                   pl.BlockSpec((None, D, D), lambda l, j: (l, 0, jnp.maximum(j - 1, 0)))),
        compiler_params=_params("arbitrary", "arbitrary"),
        name="weight_prep",
    )(w_in, jnp.asarray(_GROUP_TABLE))


def _modulated_norm(x, gain, shift, scale):
    ms = jnp.mean(x * x, axis=-1, keepdims=True)
    return (x * lax.rsqrt(ms + EPS) * gain) * (1.0 + scale) + shift


def _inproj_kernel(x_ref, mod_ref, g_ref, cos_ref, sin_ref, w_ref,
                   hb_ref, q_ref, k_ref, v_ref, gs_ref, af_ref, ar_ref):
    hb = _modulated_norm(x_ref[...], g_ref[...], mod_ref[0], mod_ref[1]).astype(bf16)
    hb_ref[...] = hb
    cos = cos_ref[...]
    sin = sin_ref[...]
    half = DK // 2

    def rotary(sec, o_ref, scale):
        p = jnp.dot(hb, w_ref[:, sec * D:(sec + 1) * D], preferred_element_type=f32)
        for h in range(HEADS):
            t1 = p[:, h * DK:h * DK + half]
            t2 = p[:, h * DK + half:(h + 1) * DK]
            o_ref[:, h * DK:h * DK + half] = ((t1 * cos - t2 * sin) * scale).astype(bf16)
            o_ref[:, h * DK + half:(h + 1) * DK] = ((t1 * sin + t2 * cos) * scale).astype(bf16)

    rotary(0, q_ref, DK ** -0.5)
    rotary(1, k_ref, 1.0)
    g = jnp.dot(hb, w_ref[:, 3 * D:4 * D], preferred_element_type=f32)
    gs_ref[...] = (g * jax.nn.sigmoid(g)).astype(bf16)
    af = jnp.dot(hb, w_ref[:, 4 * D:5 * D], preferred_element_type=f32)
    af_ref[...] = jax.nn.sigmoid(af).astype(bf16)
    ar = jnp.dot(hb, w_ref[:, 5 * D:6 * D], preferred_element_type=f32)
    ar_ref[...] = jax.nn.sigmoid(ar).astype(bf16)
    v_ref[...] = jnp.dot(hb, w_ref[:, 2 * D:3 * D], preferred_element_type=f32).astype(bf16)


def _inproj_call(layer, x, mod, gain, cos, sin, w_rest):
    row = pl.BlockSpec((None, TM, D), lambda b, i: (b, i, 0))
    rot = pl.BlockSpec((TM, DK // 2), lambda b, i: (i, 0))
    act = jax.ShapeDtypeStruct((BATCH, SEQ, D), bf16)
    return pl.pallas_call(
        _inproj_kernel,
        out_shape=(act, act, act, act, act, act, act),
        grid=(BATCH, NT),
        in_specs=[
            row,
            pl.BlockSpec((N_MOD, None, 1, D), lambda b, i: (0, b, 0, 0)),
            pl.BlockSpec((1, D), lambda b, i: (0, 0)),
            rot, rot,
            _layer_resident((D, D_REST), layer),
        ],
        out_specs=(row, row, row, row, row, row, row),
        compiler_params=_params("arbitrary", "arbitrary"),
        name="in_projection",
    )(x, mod, gain, cos, sin, w_rest)


def _fourier_kernel(h_ref, wr_ref, wi_ref, m_ref, tr_ref, ti_ref, sr_ref, si_ref):
    hb = jnp.swapaxes(h_ref[...], 0, 1).reshape(FFT_SLABS * R, D)
    zr = jnp.dot(hb, wr_ref[...], preferred_element_type=f32).astype(bf16)
    zi = jnp.dot(hb, wi_ref[...], preferred_element_type=f32).astype(bf16)
    for s in range(FFT_SLABS):
        z = jnp.concatenate([zr[s * R:(s + 1) * R], zi[s * R:(s + 1) * R]], axis=0)
        t = jnp.dot(m_ref[s].astype(bf16), z, preferred_element_type=f32)
        sr_ref[s] = t[:R].astype(bf16)
        si_ref[s] = t[R:].astype(bf16)
    tr_ref[...] = jnp.swapaxes(sr_ref[...], 0, 1)
    ti_ref[...] = jnp.swapaxes(si_ref[...], 0, 1)


def _fourier_call(layer, hb, wr, wi):
    nsteps = R // FFT_SLABS
    slab = pl.BlockSpec((None, R, FFT_SLABS, D), lambda b, j: (b, 0, j, 0))
    t2 = jax.ShapeDtypeStruct((BATCH, R, R, D), bf16)
    return pl.pallas_call(
        _fourier_kernel,
        out_shape=(t2, t2),
        grid=(BATCH, nsteps),
        in_specs=[slab,
                  _layer_resident((D, D), layer), _layer_resident((D, D), layer),
                  pl.BlockSpec((FFT_SLABS, 2 * R, 2 * R), lambda b, j: (j, 0, 0))],
        out_specs=(slab, slab),
        scratch_shapes=[pltpu.VMEM((FFT_SLABS, R, D), bf16), pltpu.VMEM((FFT_SLABS, R, D), bf16)],
        compiler_params=_params("arbitrary", "arbitrary"),
        name="fourier_projection_dft1",
    )(hb.reshape(BATCH, R, R, D), wr, wi, jnp.asarray(_STAGE1))


def _ret_kernel(q_ref, k_ref, v_ref, m_ref, dec_ref, tr_ref, ti_ref, g_ref,
                o_ref, y_ref, sf_ref, sb_ref, sb_all_ref, decb_ref, s_ref):
    p = pl.program_id(1)
    j = pl.program_id(2)
    n_chunks = RET_ROWS // RET_CHUNK

    @pl.when(j == 0)
    def _():
        sf_ref[...] = jnp.zeros_like(sf_ref)
        sb_ref[...] = jnp.zeros_like(sb_ref)
        decb_ref[...] = dec_ref[...].astype(bf16)

    @pl.when(p == 0)
    def _():
        block = pl.num_programs(2) - 1 - j
        for c in reversed(range(n_chunks)):
            rows = slice(c * RET_CHUNK, (c + 1) * RET_CHUNK)
            for h in range(HEADS):
                cols = slice(h * DK, (h + 1) * DK)
                state = sb_ref[h]
                sb_all_ref[block * n_chunks + c, h] = state.astype(bf16)
                k_dec = k_ref[rows, cols] * decb_ref[_KDB, h]
                sb_ref[h] = state * float(_CDB[h]) + lax.dot_general(
                    k_dec, v_ref[rows, cols], _TN_DIMS, preferred_element_type=f32)

    @pl.when(p == 1)
    def _():
        for c in range(n_chunks):
            rows = slice(c * RET_CHUNK, (c + 1) * RET_CHUNK)
            for h in range(HEADS):
                cols = slice(h * DK, (h + 1) * DK)
                q = q_ref[rows, cols]
                k = k_ref[rows, cols]
                v = v_ref[rows, cols]
                scores = lax.dot_general(q, k, _NT_DIMS, preferred_element_type=f32) * m_ref[h]
                state = sf_ref[h]
                lhs = jnp.concatenate([scores.astype(bf16), q * decb_ref[_QDF, h],
                                       q * decb_ref[_QDB, h]], axis=1)
                rhs = jnp.concatenate([v, state.astype(bf16), sb_all_ref[j * n_chunks + c, h]],
                                      axis=0)
                y = jnp.dot(lhs, rhs, preferred_element_type=f32)
                k_dec = k * decb_ref[_KDF, h]
                sf_ref[h] = state * float(_CDF[h]) + lax.dot_general(
                    k_dec, v, _TN_DIMS, preferred_element_type=f32)
                o_ref[rows, cols] = y.astype(bf16)
        g = g_ref[...].astype(bf16)
        for s in range(DFT2_SLABS):
            t = jnp.concatenate([tr_ref[s], ti_ref[s]], axis=0)
            s_ref[s] = jnp.dot(g, t, preferred_element_type=f32).astype(bf16)
        y_ref[...] = jnp.swapaxes(s_ref[...], 0, 1)


def _ret_call(q, k, v, tr, ti):
    nb = SEQ // RET_ROWS

    def scan_idx(b, p, j):
        return (b, j + (1 - p) * (nb - 1 - 2 * j), 0)

    def finish_idx(b, p, j):
        return (b, p * j, 0)

    scan = pl.BlockSpec((None, RET_ROWS, D), scan_idx)
    fin = pl.BlockSpec((None, RET_ROWS, D), finish_idx)
    tin = pl.BlockSpec((None, DFT2_SLABS, R, D), lambda b, p, j: (b, p * j, 0, 0))
    yout = pl.BlockSpec((None, R, DFT2_SLABS, D), lambda b, p, j: (b, 0, p * j, 0))
    return pl.pallas_call(
        _ret_kernel,
        out_shape=(jax.ShapeDtypeStruct((BATCH, SEQ, D), bf16),
                   jax.ShapeDtypeStruct((BATCH, R, R, D), bf16)),
        grid=(BATCH, 2, nb),
        in_specs=[fin, scan, scan, _resident((HEADS, RET_CHUNK, RET_CHUNK)),
                  _resident((4, HEADS, RET_CHUNK, DK)), tin, tin, _resident((R, 2 * R))],
        out_specs=(fin, yout),
        scratch_shapes=[pltpu.VMEM((HEADS, DK, DK), f32), pltpu.VMEM((HEADS, DK, DK), f32),
                        pltpu.VMEM((SEQ // RET_CHUNK, HEADS, DK, DK), bf16),
                        pltpu.VMEM((4, HEADS, RET_CHUNK, DK), bf16),
                        pltpu.VMEM((DFT2_SLABS, R, D), bf16)],
        compiler_params=_params("arbitrary", "arbitrary", "arbitrary"),
        name="retention_dft2",
    )(q, k, v, jnp.asarray(_RMASK), jnp.asarray(_RDEC), tr, ti, jnp.asarray(_STAGE2))


def _merge_kernel(x_ref, mod_ref, yf_ref, yr_ref, gs_ref, af_ref, ar_ref, wf_ref, wr_ref, wo_ref,
                  o_ref):
    a = jnp.dot(yf_ref[...], wf_ref[...], preferred_element_type=f32)
    gated = []
    for h in range(HEADS):
        cols = slice(h * DK, (h + 1) * DK)
        y = yr_ref[:, cols].astype(f32)
        mu = jnp.mean(y, axis=-1, keepdims=True)
        yc = y - mu
        var = jnp.mean(yc * yc, axis=-1, keepdims=True)
        yn = yc * lax.rsqrt(var + EPS)
        gated.append((yn * gs_ref[:, cols].astype(f32)).astype(bf16))
    b = jnp.dot(jnp.concatenate(gated, axis=1), wr_ref[...], preferred_element_type=f32)
    merged = af_ref[...].astype(f32) * a + ar_ref[...].astype(f32) * b
    o = jnp.dot(merged.astype(bf16), wo_ref[...], preferred_element_type=f32)
    o_ref[...] = x_ref[...] + mod_ref[2] * o


def _merge_call(layer, x, mod, yf, yr, gs, af, ar, wf, wr, wo):
    row = pl.BlockSpec((None, TM, D), lambda b, i: (b, i, 0))
    return pl.pallas_call(
        _merge_kernel,
        out_shape=jax.ShapeDtypeStruct((BATCH, SEQ, D), f32),
        grid=(BATCH, NT),
        in_specs=[row, pl.BlockSpec((N_MOD, None, 1, D), lambda b, i: (0, b, 0, 0)),
                  row, row, row, row, row,
                  _layer_resident((D, D), layer), _layer_resident((D, D), layer),
                  _layer_resident((D, D), layer)],
        out_specs=row,
        compiler_params=_params("arbitrary", "arbitrary"),
        name="merge_out_projection",
    )(x, mod, yf, yr, gs, af, ar, wf, wr, wo)


def _ffn_kernel(x_ref, xp_ref, xn_ref, mod_ref, g_ref, wup_ref, cw_ref, cb_ref, wdn_ref, fg_ref,
                o_ref, act_ref, *, final):
    i = pl.program_id(1)
    rows = TM + 2 * HALO
    gain, shift, scale = g_ref[...], mod_ref[3], mod_ref[4]
    keep_prev = (i > 0).astype(f32)
    keep_next = (i < pl.num_programs(1) - 1).astype(f32)
    h = jnp.concatenate([
        _modulated_norm(xp_ref[...], gain, shift, scale) * keep_prev,
        _modulated_norm(x_ref[...], gain, shift, scale),
        _modulated_norm(xn_ref[...], gain, shift, scale) * keep_next,
    ], axis=0).astype(bf16)

    def conv(col):
        u = jnp.dot(h, wup_ref[:, col:col + FF_TILE], preferred_element_type=f32)
        w = cw_ref[:, col:col + FF_TILE]
        full = (pltpu.roll(u, 1, 0) * w[0:1] + u * w[1:2] + pltpu.roll(u, rows - 1, 0) * w[2:3]
                + cb_ref[:, col:col + FF_TILE])
        return full[HALO:HALO + TM]

    for t in range(D_FF // FF_TILE):
        a = conv(t * FF_TILE)
        b = conv(D_FF + t * FF_TILE)
        gelu = 0.5 * a * (1.0 + lax.erf(a * (2.0 ** -0.5)))
        act_ref[:, t * FF_TILE:(t + 1) * FF_TILE] = (gelu * b).astype(bf16)

    y = jnp.dot(act_ref[...], wdn_ref[...], preferred_element_type=f32)
    out = x_ref[...] + mod_ref[5] * y
    if final:
        ms = jnp.mean(out * out, axis=-1, keepdims=True)
        out = out * lax.rsqrt(ms + EPS) * fg_ref[...]
    o_ref[...] = out


def _ffn_call(layer, x, mod, gain, wup, cw, cb, wdn, final_g, final):
    tiles = TM // HALO
    row = pl.BlockSpec((None, TM, D), lambda b, i: (b, i, 0))
    prev = pl.BlockSpec((None, HALO, D), lambda b, i: (b, jnp.maximum(i * tiles - 1, 0), 0))
    nxt = pl.BlockSpec((None, HALO, D),
                       lambda b, i: (b, jnp.minimum((i + 1) * tiles, SEQ // HALO - 1), 0))
    return pl.pallas_call(
        functools.partial(_ffn_kernel, final=final),
        out_shape=jax.ShapeDtypeStruct((BATCH, SEQ, D), f32),
        grid=(BATCH, NT),
        in_specs=[row, prev, nxt,
                  pl.BlockSpec((N_MOD, None, 1, D), lambda b, i: (0, b, 0, 0)),
                  pl.BlockSpec((1, D), lambda b, i: (0, 0)),
                  _layer_resident((D, 2 * D_FF), layer), _layer_resident((3, 2 * D_FF), layer),
                  _layer_resident((1, 2 * D_FF), layer), _layer_resident((D_FF, D), layer),
                  pl.BlockSpec((1, D), lambda b, i: (0, 0))],
        out_specs=row,
        scratch_shapes=[pltpu.VMEM((TM, D_FF), bf16)],
        compiler_params=_params("arbitrary", "arbitrary"),
        name="conv_ffn_final" if final else "conv_ffn",
    )(x, x, x, mod, gain, wup, cw, cb, wdn, final_g)


def kernel(x, c, norm1_g, norm2_g, ada_w, ada_b, w_in, w_fourier, w_ret, w_out,
           ffn_up, conv_w, conv_b, ffn_down, final_g):
    assert x.shape == (BATCH, SEQ, D) and c.shape == (BATCH, D)
    mod_all = _ada_call(c, ada_w, ada_b).reshape(DEPTH, N_MOD, BATCH, 1, D)
    wfr_all, wfi_all, w_rest = _prep_call(w_in)
    cos, sin = _rotary_tables()
    final_gain = final_g.reshape(1, D)
    wf, wr, wo = w_fourier.astype(bf16), w_ret.astype(bf16), w_out.astype(bf16)
    wup, wdn = ffn_up.astype(bf16), ffn_down.astype(bf16)

    for l in range(DEPTH):
        mod = mod_all[l]
        gain1 = norm1_g[l].reshape(1, D)
        hb, q, k, v, gs, af, ar = _inproj_call(l, x, mod, gain1, cos, sin, w_rest)
        tr, ti = _fourier_call(l, hb, wfr_all, wfi_all)
        yr, yf = _ret_call(q, k, v, tr, ti)
        x = _merge_call(l, x, mod, yf.reshape(BATCH, SEQ, D), yr, gs, af, ar, wf, wr, wo)
        x = _ffn_call(l, x, mod, norm2_g[l].reshape(1, D), wup, conv_w,
                      conv_b.reshape(DEPTH, 1, 2 * D_FF), wdn, final_gain,
                      final=(l == DEPTH - 1))
    return x
```

```python
import functools

import numpy as np
import jax
import jax.numpy as jnp
from jax import lax
from jax.experimental import pallas as pl
from jax.experimental.pallas import tpu as pltpu

f32 = jnp.float32
bf16 = jnp.bfloat16

D = 1024
BATCH = 8
SEQ = 4096
DEPTH = 4
GROUP = 128
HEADS = 4
DK = D // HEADS
ROPE_BASE = 10000.0
D_FF = 2816
N_MOD = 6
EPS = 1e-6
D_REST = 6 * D

R = 64
TM = 1024
NT = SEQ // TM
RET_CHUNK = 256
RET_ROWS = 1024
FFT_SLABS = 32
DFT2_SLABS = RET_ROWS // R
FF_TILE = 256
HALO = 8
VMEM_LIMIT = 56 * 1024 * 1024

_NT_DIMS = (((1,), (1,)), ((), ()))
_TN_DIMS = (((0,), (0,)), ((), ()))


def _resident(shape):
    nd = len(shape)
    return pl.BlockSpec(shape, lambda *_: (0,) * nd, pipeline_mode=pl.Buffered(1))


def _layer_resident(shape, layer):
    nd = len(shape)
    return pl.BlockSpec((None,) + tuple(shape), lambda *_: (layer,) + (0,) * nd,
                        pipeline_mode=pl.Buffered(1))


def _params(*sem):
    return pltpu.CompilerParams(dimension_semantics=sem, vmem_limit_bytes=VMEM_LIMIT)


def _group_dft_tables():
    n = np.arange(GROUP)
    ang = 2.0 * np.pi * np.outer(n, n) / GROUP
    s = 1.0 / np.sqrt(GROUP)
    return (np.cos(ang) * s).astype(np.float32), (-np.sin(ang) * s).astype(np.float32)


def _stage_tables():
    k = np.arange(R)
    ang = 2.0 * np.pi * np.outer(k, k) / R
    cr = np.cos(ang) / 8.0
    ci = -np.sin(ang) / 8.0
    stage2 = np.concatenate([cr, -ci], axis=1)
    n2 = np.arange(R)[:, None, None]
    k1 = np.arange(R)[None, :, None]
    n1 = np.arange(R)[None, None, :]
    a = np.exp(-2j * np.pi * (n2 * k1 / SEQ + n1 * k1 / R)) / 8.0
    stage1 = np.concatenate([np.concatenate([a.real, -a.imag], axis=2),
                             np.concatenate([a.imag, a.real], axis=2)], axis=1)
    return stage1.astype(np.float32), stage2.astype(np.float32)


def _retention_tables():
    c = RET_CHUNK
    j = np.arange(c, dtype=np.float64)
    diff = j[:, None] - j[None, :]
    ones = np.ones((1, DK))
    mask = np.zeros((HEADS, c, c))
    qdf, qdb, kdf, kdb = (np.zeros((HEADS, c, DK)) for _ in range(4))
    cdf, cdb = np.zeros(HEADS), np.zeros(HEADS)
    for h in range(HEADS):
        lf = np.log1p(-np.exp2(-5.0 - h))
        lb = np.log1p(-np.exp2(-5.5 - h))
        mask[h] = np.where(diff >= 0, np.exp(lf * np.maximum(diff, 0.0)),
                           np.exp(lb * np.maximum(-diff, 0.0)))
        qdf[h] = np.exp(lf * (j + 1.0))[:, None] * ones
        kdf[h] = np.exp(lf * (c - 1.0 - j))[:, None] * ones
        qdb[h] = np.exp(lb * (c - j))[:, None] * ones
        kdb[h] = np.exp(lb * j)[:, None] * ones
        cdf[h] = np.exp(lf * c)
        cdb[h] = np.exp(lb * c)
    dec = np.stack([qdf, qdb, kdf, kdb])
    return mask.astype(np.float32), dec.astype(np.float32), cdf, cdb


_GROUP_TABLE = np.concatenate(_group_dft_tables(), axis=1)
_STAGE1, _STAGE2 = _stage_tables()
_RMASK, _RDEC, _CDF, _CDB = _retention_tables()
_QDF, _QDB, _KDF, _KDB = range(4)


def _rotary_tables():
    half = DK // 2
    inv_freq = ROPE_BASE ** (-jnp.arange(half, dtype=f32) / half)
    ang = jnp.arange(SEQ, dtype=f32)[:, None] * inv_freq[None, :]
    return jnp.cos(ang), jnp.sin(ang)


def _ada_kernel(c_ref, w_ref, b_ref, o_ref):
    c = c_ref[...]
    act = (c * jax.nn.sigmoid(c)).astype(bf16)
    o_ref[...] = jnp.dot(act, w_ref[...].astype(bf16), preferred_element_type=f32) + b_ref[...]


def _ada_call(c, ada_w, ada_b):
    return pl.pallas_call(
        _ada_kernel,
        out_shape=jax.ShapeDtypeStruct((DEPTH, N_MOD, BATCH, D), f32),
        grid=(DEPTH, N_MOD),
        in_specs=[
            pl.BlockSpec((BATCH, D), lambda l, j: (0, 0)),
            pl.BlockSpec((None, D, D), lambda l, j: (l, 0, j)),
            pl.BlockSpec((None, None, 1, D), lambda l, j: (l, j, 0, 0)),
        ],
        out_specs=pl.BlockSpec((None, None, BATCH, D), lambda l, j: (l, j, 0, 0)),
        compiler_params=_params("arbitrary", "arbitrary"),
        name="ada_mod",
    )(c, ada_w, ada_b.reshape(DEPTH, N_MOD, 1, D))


def _prep_kernel(w_ref, t_ref, wr_ref, wi_ref, rest_ref):
    j = pl.program_id(1)

    @pl.when(j == 0)
    def _():
        table = t_ref[...]
        t_hi = table.astype(bf16)
        t_lo = (table - t_hi.astype(f32)).astype(bf16)
        for g in range(D // GROUP):
            cols = slice(g * GROUP, (g + 1) * GROUP)
            w = w_ref[:, cols]
            hi = w.astype(bf16)
            lo = (w - hi.astype(f32)).astype(bf16)
            folded = (jnp.dot(hi, t_hi, preferred_element_type=f32)
                      + jnp.dot(lo, t_hi, preferred_element_type=f32)
                      + jnp.dot(hi, t_lo, preferred_element_type=f32))
            wr_ref[:, cols] = folded[:, :GROUP].astype(bf16)
            wi_ref[:, cols] = folded[:, GROUP:].astype(bf16)

    @pl.when(j > 0)
    def _():
        rest_ref[...] = w_ref[...].astype(bf16)


def _prep_call(w_in):
    sq = jax.ShapeDtypeStruct((DEPTH, D, D), bf16)
    tab = pl.BlockSpec((GROUP, 2 * GROUP), lambda l, j: (0, 0))
    fold_out = pl.BlockSpec((None, D, D), lambda l, j: (l, 0, 0))
    return pl.pallas_call(
        _prep_kernel,
        out_shape=(sq, sq, jax.ShapeDtypeStruct((DEPTH, D, D_REST), bf16)),
        grid=(DEPTH, 1 + D_REST // D),
        in_specs=[pl.BlockSpec((None, D, D), lambda l, j: (l, 0, j)), tab],
        out_specs=(fold_out, fold_out,
                   pl.BlockSpec((None, D, D), lambda l, j: (l, 0, jnp.maximum(j - 1, 0)))),
        compiler_params=_params("arbitrary", "arbitrary"),
        name="weight_prep",
    )(w_in, jnp.asarray(_GROUP_TABLE))


def _modulated_norm(x, gain, shift, scale):
    ms = jnp.mean(x * x, axis=-1, keepdims=True)
    return (x * lax.rsqrt(ms + EPS) * gain) * (1.0 + scale) + shift


def _inproj_kernel(x_ref, mod_ref, g_ref, cos_ref, sin_ref, w_ref,
                   hb_ref, q_ref, k_ref, v_ref, gs_ref, af_ref, ar_ref):
    hb = _modulated_norm(x_ref[...], g_ref[...], mod_ref[0], mod_ref[1]).astype(bf16)
    hb_ref[...] = hb
    cos = cos_ref[...]
    sin = sin_ref[...]
    half = DK // 2

    def rotary(sec, o_ref, scale):
        p = jnp.dot(hb, w_ref[:, sec * D:(sec + 1) * D], preferred_element_type=f32)
        for h in range(HEADS):
            t1 = p[:, h * DK:h * DK + half]
            t2 = p[:, h * DK + half:(h + 1) * DK]
            o_ref[:, h * DK:h * DK + half] = ((t1 * cos - t2 * sin) * scale).astype(bf16)
            o_ref[:, h * DK + half:(h + 1) * DK] = ((t1 * sin + t2 * cos) * scale).astype(bf16)

    rotary(0, q_ref, DK ** -0.5)
    rotary(1, k_ref, 1.0)
    g = jnp.dot(hb, w_ref[:, 3 * D:4 * D], preferred_element_type=f32)
    gs_ref[...] = (g * jax.nn.sigmoid(g)).astype(bf16)
    af = jnp.dot(hb, w_ref[:, 4 * D:5 * D], preferred_element_type=f32)
    af_ref[...] = jax.nn.sigmoid(af).astype(bf16)
    ar = jnp.dot(hb, w_ref[:, 5 * D:6 * D], preferred_element_type=f32)
    ar_ref[...] = jax.nn.sigmoid(ar).astype(bf16)
    v_ref[...] = jnp.dot(hb, w_ref[:, 2 * D:3 * D], preferred_element_type=f32).astype(bf16)


def _inproj_call(layer, x, mod, gain, cos, sin, w_rest):
    row = pl.BlockSpec((None, TM, D), lambda b, i: (b, i, 0))
    rot = pl.BlockSpec((TM, DK // 2), lambda b, i: (i, 0))
    act = jax.ShapeDtypeStruct((BATCH, SEQ, D), bf16)
    return pl.pallas_call(
        _inproj_kernel,
        out_shape=(act, act, act, act, act, act, act),
        grid=(BATCH, NT),
        in_specs=[
            row,
            pl.BlockSpec((N_MOD, None, 1, D), lambda b, i: (0, b, 0, 0)),
            pl.BlockSpec((1, D), lambda b, i: (0, 0)),
            rot, rot,
            _layer_resident((D, D_REST), layer),
        ],
        out_specs=(row, row, row, row, row, row, row),
        compiler_params=_params("arbitrary", "arbitrary"),
        name="in_projection",
    )(x, mod, gain, cos, sin, w_rest)


def _fourier_kernel(h_ref, wr_ref, wi_ref, m_ref, tr_ref, ti_ref, sr_ref, si_ref):
    hb = jnp.swapaxes(h_ref[...], 0, 1).reshape(FFT_SLABS * R, D)
    zr = jnp.dot(hb, wr_ref[...], preferred_element_type=f32).astype(bf16)
    zi = jnp.dot(hb, wi_ref[...], preferred_element_type=f32).astype(bf16)
    for s in range(FFT_SLABS):
        z = jnp.concatenate([zr[s * R:(s + 1) * R], zi[s * R:(s + 1) * R]], axis=0)
        t = jnp.dot(m_ref[s].astype(bf16), z, preferred_element_type=f32)
        sr_ref[s] = t[:R].astype(bf16)
        si_ref[s] = t[R:].astype(bf16)
    tr_ref[...] = jnp.swapaxes(sr_ref[...], 0, 1)
    ti_ref[...] = jnp.swapaxes(si_ref[...], 0, 1)


def _fourier_call(layer, hb, wr, wi):
    nsteps = R // FFT_SLABS
    slab = pl.BlockSpec((None, R, FFT_SLABS, D), lambda b, j: (b, 0, j, 0))
    t2 = jax.ShapeDtypeStruct((BATCH, R, R, D), bf16)
    return pl.pallas_call(
        _fourier_kernel,
        out_shape=(t2, t2),
        grid=(BATCH, nsteps),
        in_specs=[slab,
                  _layer_resident((D, D), layer), _layer_resident((D, D), layer),
                  pl.BlockSpec((FFT_SLABS, 2 * R, 2 * R), lambda b, j: (j, 0, 0))],
        out_specs=(slab, slab),
        scratch_shapes=[pltpu.VMEM((FFT_SLABS, R, D), bf16), pltpu.VMEM((FFT_SLABS, R, D), bf16)],
        compiler_params=_params("arbitrary", "arbitrary"),
        name="fourier_projection_dft1",
    )(hb.reshape(BATCH, R, R, D), wr, wi, jnp.asarray(_STAGE1))


def _ret_kernel(q_ref, k_ref, v_ref, m_ref, dec_ref, tr_ref, ti_ref, g_ref,
                o_ref, y_ref, sf_ref, sb_ref, sb_all_ref, decb_ref, s_ref):
    p = pl.program_id(1)
    j = pl.program_id(2)
    n_chunks = RET_ROWS // RET_CHUNK

    @pl.when(j == 0)
    def _():
        sf_ref[...] = jnp.zeros_like(sf_ref)
        sb_ref[...] = jnp.zeros_like(sb_ref)
        decb_ref[...] = dec_ref[...].astype(bf16)

    @pl.when(p == 0)
    def _():
        block = pl.num_programs(2) - 1 - j
        for c in reversed(range(n_chunks)):
            rows = slice(c * RET_CHUNK, (c + 1) * RET_CHUNK)
            for h in range(HEADS):
                cols = slice(h * DK, (h + 1) * DK)
                state = sb_ref[h]
                sb_all_ref[block * n_chunks + c, h] = state.astype(bf16)
                k_dec = k_ref[rows, cols] * decb_ref[_KDB, h]
                sb_ref[h] = state * float(_CDB[h]) + lax.dot_general(
                    k_dec, v_ref[rows, cols], _TN_DIMS, preferred_element_type=f32)

    @pl.when(p == 1)
    def _():
        for c in range(n_chunks):
            rows = slice(c * RET_CHUNK, (c + 1) * RET_CHUNK)
            for h in range(HEADS):
                cols = slice(h * DK, (h + 1) * DK)
                q = q_ref[rows, cols]
                k = k_ref[rows, cols]
                v = v_ref[rows, cols]
                scores = lax.dot_general(q, k, _NT_DIMS, preferred_element_type=f32) * m_ref[h]
                state = sf_ref[h]
                lhs = jnp.concatenate([scores.astype(bf16), q * decb_ref[_QDF, h],
                                       q * decb_ref[_QDB, h]], axis=1)
                rhs = jnp.concatenate([v, state.astype(bf16), sb_all_ref[j * n_chunks + c, h]],
                                      axis=0)
                y = jnp.dot(lhs, rhs, preferred_element_type=f32)
                k_dec = k * decb_ref[_KDF, h]
                sf_ref[h] = state * float(_CDF[h]) + lax.dot_general(
                    k_dec, v, _TN_DIMS, preferred_element_type=f32)
                o_ref[rows, cols] = y.astype(bf16)
        g = g_ref[...].astype(bf16)
        for s in range(DFT2_SLABS):
            t = jnp.concatenate([tr_ref[s], ti_ref[s]], axis=0)
            s_ref[s] = jnp.dot(g, t, preferred_element_type=f32).astype(bf16)
        y_ref[...] = jnp.swapaxes(s_ref[...], 0, 1)


def _ret_call(q, k, v, tr, ti):
    nb = SEQ // RET_ROWS

    def scan_idx(b, p, j):
        return (b, j + (1 - p) * (nb - 1 - 2 * j), 0)

    def finish_idx(b, p, j):
        return (b, p * j, 0)

    scan = pl.BlockSpec((None, RET_ROWS, D), scan_idx)
    fin = pl.BlockSpec((None, RET_ROWS, D), finish_idx)
    tin = pl.BlockSpec((None, DFT2_SLABS, R, D), lambda b, p, j: (b, p * j, 0, 0))
    yout = pl.BlockSpec((None, R, DFT2_SLABS, D), lambda b, p, j: (b, 0, p * j, 0))
    return pl.pallas_call(
        _ret_kernel,
        out_shape=(jax.ShapeDtypeStruct((BATCH, SEQ, D), bf16),
                   jax.ShapeDtypeStruct((BATCH, R, R, D), bf16)),
        grid=(BATCH, 2, nb),
        in_specs=[fin, scan, scan, _resident((HEADS, RET_CHUNK, RET_CHUNK)),
                  _resident((4, HEADS, RET_CHUNK, DK)), tin, tin, _resident((R, 2 * R))],
        out_specs=(fin, yout),
        scratch_shapes=[pltpu.VMEM((HEADS, DK, DK), f32), pltpu.VMEM((HEADS, DK, DK), f32),
                        pltpu.VMEM((SEQ // RET_CHUNK, HEADS, DK, DK), bf16),
                        pltpu.VMEM((4, HEADS, RET_CHUNK, DK), bf16),
                        pltpu.VMEM((DFT2_SLABS, R, D), bf16)],
        compiler_params=_params("arbitrary", "arbitrary", "arbitrary"),
        name="retention_dft2",
    )(q, k, v, jnp.asarray(_RMASK), jnp.asarray(_RDEC), tr, ti, jnp.asarray(_STAGE2))


def _merge_kernel(x_ref, mod_ref, yf_ref, yr_ref, gs_ref, af_ref, ar_ref, wf_ref, wr_ref, wo_ref,
                  up_ref, dn_ref, o_ref, upb_ref, dnb_ref):
    upb_ref[...] = up_ref[...].astype(bf16)
    dnb_ref[...] = dn_ref[...].astype(bf16)
    a = jnp.dot(yf_ref[...], wf_ref[...], preferred_element_type=f32)
    gated = []
    for h in range(HEADS):
        cols = slice(h * DK, (h + 1) * DK)
        y = yr_ref[:, cols].astype(f32)
        mu = jnp.mean(y, axis=-1, keepdims=True)
        yc = y - mu
        var = jnp.mean(yc * yc, axis=-1, keepdims=True)
        yn = yc * lax.rsqrt(var + EPS)
        gated.append((yn * gs_ref[:, cols].astype(f32)).astype(bf16))
    b = jnp.dot(jnp.concatenate(gated, axis=1), wr_ref[...], preferred_element_type=f32)
    merged = af_ref[...].astype(f32) * a + ar_ref[...].astype(f32) * b
    o = jnp.dot(merged.astype(bf16), wo_ref[...], preferred_element_type=f32)
    o_ref[...] = x_ref[...] + mod_ref[2] * o


def _merge_call(layer, x, mod, yf, yr, gs, af, ar, wf, wr, wo, ffn_up, ffn_down):
    steps = BATCH * NT
    up_rows = D // steps
    dn_rows = 2 * D_FF // steps
    row = pl.BlockSpec((None, TM, D), lambda b, i: (b, i, 0))
    return pl.pallas_call(
        _merge_kernel,
        out_shape=(jax.ShapeDtypeStruct((BATCH, SEQ, D), f32),
                   jax.ShapeDtypeStruct((D, 2 * D_FF), bf16), jax.ShapeDtypeStruct((D_FF, D), bf16)),
        grid=(BATCH, NT),
        in_specs=[row, pl.BlockSpec((N_MOD, None, 1, D), lambda b, i: (0, b, 0, 0)),
                  row, row, row, row, row,
                  _layer_resident((D, D), layer), _layer_resident((D, D), layer),
                  _layer_resident((D, D), layer),
                  pl.BlockSpec((None, up_rows, 2 * D_FF), lambda b, i: (layer, b * NT + i, 0)),
                  pl.BlockSpec((None, dn_rows, D), lambda b, i: (layer, (b * NT + i) // 2, 0))],
        out_specs=(row,
                   pl.BlockSpec((up_rows, 2 * D_FF), lambda b, i: (b * NT + i, 0)),
                   pl.BlockSpec((dn_rows, D), lambda b, i: ((b * NT + i) // 2, 0))),
        compiler_params=_params("arbitrary", "arbitrary"),
        name="merge_out_projection",
    )(x, mod, yf, yr, gs, af, ar, wf, wr, wo, ffn_up, ffn_down)


def _ffn_kernel(x_ref, xp_ref, xn_ref, mod_ref, g_ref, wup_ref, cw_ref, cb_ref, wdn_ref, fg_ref,
                o_ref, act_ref, *, final):
    i = pl.program_id(1)
    rows = TM + 2 * HALO
    gain, shift, scale = g_ref[...], mod_ref[3], mod_ref[4]
    keep_prev = (i > 0).astype(f32)
    keep_next = (i < pl.num_programs(1) - 1).astype(f32)
    h = jnp.concatenate([
        _modulated_norm(xp_ref[...], gain, shift, scale) * keep_prev,
        _modulated_norm(x_ref[...], gain, shift, scale),
        _modulated_norm(xn_ref[...], gain, shift, scale) * keep_next,
    ], axis=0).astype(bf16)

    def conv(col):
        u = jnp.dot(h, wup_ref[:, col:col + FF_TILE], preferred_element_type=f32)
        w = cw_ref[:, col:col + FF_TILE]
        full = (pltpu.roll(u, 1, 0) * w[0:1] + u * w[1:2] + pltpu.roll(u, rows - 1, 0) * w[2:3]
                + cb_ref[:, col:col + FF_TILE])
        return full[HALO:HALO + TM]

    for t in range(D_FF // FF_TILE):
        a = conv(t * FF_TILE)
        b = conv(D_FF + t * FF_TILE)
        gelu = 0.5 * a * (1.0 + lax.erf(a * (2.0 ** -0.5)))
        act_ref[:, t * FF_TILE:(t + 1) * FF_TILE] = (gelu * b).astype(bf16)

    y = jnp.dot(act_ref[...], wdn_ref[...], preferred_element_type=f32)
    out = x_ref[...] + mod_ref[5] * y
    if final:
        ms = jnp.mean(out * out, axis=-1, keepdims=True)
        out = out * lax.rsqrt(ms + EPS) * fg_ref[...]
    o_ref[...] = out


def _ffn_call(layer, x, mod, gain, wup, cw, cb, wdn, final_g, final):
    tiles = TM // HALO
    row = pl.BlockSpec((None, TM, D), lambda b, i: (b, i, 0))
    prev = pl.BlockSpec((None, HALO, D), lambda b, i: (b, jnp.maximum(i * tiles - 1, 0), 0))
    nxt = pl.BlockSpec((None, HALO, D),
                       lambda b, i: (b, jnp.minimum((i + 1) * tiles, SEQ // HALO - 1), 0))
    return pl.pallas_call(
        functools.partial(_ffn_kernel, final=final),
        out_shape=jax.ShapeDtypeStruct((BATCH, SEQ, D), f32),
        grid=(BATCH, NT),
        in_specs=[row, prev, nxt,
                  pl.BlockSpec((N_MOD, None, 1, D), lambda b, i: (0, b, 0, 0)),
                  pl.BlockSpec((1, D), lambda b, i: (0, 0)),
                  _resident((D, 2 * D_FF)), _layer_resident((3, 2 * D_FF), layer),
                  _layer_resident((1, 2 * D_FF), layer), _resident((D_FF, D)),
                  pl.BlockSpec((1, D), lambda b, i: (0, 0))],
        out_specs=row,
        scratch_shapes=[pltpu.VMEM((TM, D_FF), bf16)],
        compiler_params=_params("arbitrary", "arbitrary"),
        name="conv_ffn_final" if final else "conv_ffn",
    )(x, x, x, mod, gain, wup, cw, cb, wdn, final_g)


def kernel(x, c, norm1_g, norm2_g, ada_w, ada_b, w_in, w_fourier, w_ret, w_out,
           ffn_up, conv_w, conv_b, ffn_down, final_g):
    assert x.shape == (BATCH, SEQ, D) and c.shape == (BATCH, D)
    mod_all = _ada_call(c, ada_w, ada_b).reshape(DEPTH, N_MOD, BATCH, 1, D)
    wfr_all, wfi_all, w_rest = _prep_call(w_in)
    cos, sin = _rotary_tables()
    final_gain = final_g.reshape(1, D)
    wf, wr, wo = w_fourier.astype(bf16), w_ret.astype(bf16), w_out.astype(bf16)

    for l in range(DEPTH):
        mod = mod_all[l]
        gain1 = norm1_g[l].reshape(1, D)
        hb, q, k, v, gs, af, ar = _inproj_call(l, x, mod, gain1, cos, sin, w_rest)
        tr, ti = _fourier_call(l, hb, wfr_all, wfi_all)
        yr, yf = _ret_call(q, k, v, tr, ti)
        x, wup, wdn = _merge_call(l, x, mod, yf.reshape(BATCH, SEQ, D), yr, gs, af, ar,
                                  wf, wr, wo, ffn_up, ffn_down)
        x = _ffn_call(l, x, mod, norm2_g[l].reshape(1, D), wup, conv_w,
                      conv_b.reshape(DEPTH, 1, 2 * D_FF), wdn, final_gain,
                      final=(l == DEPTH - 1))
    return x
```

```python
import functools

import numpy as np
import jax
import jax.numpy as jnp
from jax import lax
from jax.experimental import pallas as pl
from jax.experimental.pallas import tpu as pltpu

f32 = jnp.float32
bf16 = jnp.bfloat16

D = 1024
BATCH = 8
SEQ = 4096
DEPTH = 4
GROUP = 128
HEADS = 4
DK = D // HEADS
ROPE_BASE = 10000.0
D_FF = 2816
N_MOD = 6
EPS = 1e-6
D_REST = 6 * D

R = 64
TM = 1024
NT = SEQ // TM
RET_CHUNK = 256
RET_ROWS = 1024
FFT_SLABS = 32
DFT2_SLABS = RET_ROWS // R
FF_TILE = 256
HALO = 8
VMEM_LIMIT = 56 * 1024 * 1024

_NT_DIMS = (((1,), (1,)), ((), ()))
_TN_DIMS = (((0,), (0,)), ((), ()))


def _resident(shape):
    nd = len(shape)
    return pl.BlockSpec(shape, lambda *_: (0,) * nd, pipeline_mode=pl.Buffered(1))


def _layer_resident(shape, layer):
    nd = len(shape)
    return pl.BlockSpec((None,) + tuple(shape), lambda *_: (layer,) + (0,) * nd,
                        pipeline_mode=pl.Buffered(1))


def _params(*sem):
    return pltpu.CompilerParams(dimension_semantics=sem, vmem_limit_bytes=VMEM_LIMIT)


def _group_dft_tables():
    n = np.arange(GROUP)
    ang = 2.0 * np.pi * np.outer(n, n) / GROUP
    s = 1.0 / np.sqrt(GROUP)
    return (np.cos(ang) * s).astype(np.float32), (-np.sin(ang) * s).astype(np.float32)


def _stage_tables():
    k = np.arange(R)
    ang = 2.0 * np.pi * np.outer(k, k) / R
    cr = np.cos(ang) / 8.0
    ci = -np.sin(ang) / 8.0
    stage2 = np.concatenate([cr, -ci], axis=1)
    n2 = np.arange(R)[:, None, None]
    k1 = np.arange(R)[None, :, None]
    n1 = np.arange(R)[None, None, :]
    a = np.exp(-2j * np.pi * (n2 * k1 / SEQ + n1 * k1 / R)) / 8.0
    stage1 = np.concatenate([np.concatenate([a.real, -a.imag], axis=2),
                             np.concatenate([a.imag, a.real], axis=2)], axis=1)
    return stage1.astype(np.float32), stage2.astype(np.float32)


def _retention_tables():
    c = RET_CHUNK
    j = np.arange(c, dtype=np.float64)
    diff = j[:, None] - j[None, :]
    ones = np.ones((1, DK))
    mask = np.zeros((HEADS, c, c))
    qdf, qdb, kdf, kdb = (np.zeros((HEADS, c, DK)) for _ in range(4))
    cdf, cdb = np.zeros(HEADS), np.zeros(HEADS)
    for h in range(HEADS):
        lf = np.log1p(-np.exp2(-5.0 - h))
        lb = np.log1p(-np.exp2(-5.5 - h))
        mask[h] = np.where(diff >= 0, np.exp(lf * np.maximum(diff, 0.0)),
                           np.exp(lb * np.maximum(-diff, 0.0)))
        qdf[h] = np.exp(lf * (j + 1.0))[:, None] * ones
        kdf[h] = np.exp(lf * (c - 1.0 - j))[:, None] * ones
        qdb[h] = np.exp(lb * (c - j))[:, None] * ones
        kdb[h] = np.exp(lb * j)[:, None] * ones
        cdf[h] = np.exp(lf * c)
        cdb[h] = np.exp(lb * c)
    dec = np.stack([qdf, qdb, kdf, kdb])
    return mask.astype(np.float32), dec.astype(np.float32), cdf, cdb


_GROUP_TABLE = np.concatenate(_group_dft_tables(), axis=1)
_STAGE1, _STAGE2 = _stage_tables()
_RMASK, _RDEC, _CDF, _CDB = _retention_tables()
_QDF, _QDB, _KDF, _KDB = range(4)


def _rotary_tables():
    half = DK // 2
    inv_freq = ROPE_BASE ** (-jnp.arange(half, dtype=f32) / half)
    ang = jnp.arange(SEQ, dtype=f32)[:, None] * inv_freq[None, :]
    return jnp.cos(ang), jnp.sin(ang)


def _ada_kernel(c_ref, w_ref, b_ref, o_ref):
    c = c_ref[...]
    act = (c * jax.nn.sigmoid(c)).astype(bf16)
    o_ref[...] = jnp.dot(act, w_ref[...].astype(bf16), preferred_element_type=f32) + b_ref[...]


def _ada_call(c, ada_w, ada_b):
    return pl.pallas_call(
        _ada_kernel,
        out_shape=jax.ShapeDtypeStruct((DEPTH, N_MOD, BATCH, D), f32),
        grid=(DEPTH, N_MOD),
        in_specs=[
            pl.BlockSpec((BATCH, D), lambda l, j: (0, 0)),
            pl.BlockSpec((None, D, D), lambda l, j: (l, 0, j)),
            pl.BlockSpec((None, None, 1, D), lambda l, j: (l, j, 0, 0)),
        ],
        out_specs=pl.BlockSpec((None, None, BATCH, D), lambda l, j: (l, j, 0, 0)),
        compiler_params=_params("arbitrary", "arbitrary"),
        name="ada_mod",
    )(c, ada_w, ada_b.reshape(DEPTH, N_MOD, 1, D))


def _prep_kernel(w_ref, t_ref, wr_ref, wi_ref, rest_ref):
    j = pl.program_id(1)

    @pl.when(j == 0)
    def _():
        table = t_ref[...]
        t_hi = table.astype(bf16)
        t_lo = (table - t_hi.astype(f32)).astype(bf16)
        for g in range(D // GROUP):
            cols = slice(g * GROUP, (g + 1) * GROUP)
            w = w_ref[:, cols]
            hi = w.astype(bf16)
            lo = (w - hi.astype(f32)).astype(bf16)
            folded = (jnp.dot(hi, t_hi, preferred_element_type=f32)
                      + jnp.dot(lo, t_hi, preferred_element_type=f32)
                      + jnp.dot(hi, t_lo, preferred_element_type=f32))
            wr_ref[:, cols] = folded[:, :GROUP].astype(bf16)
            wi_ref[:, cols] = folded[:, GROUP:].astype(bf16)

    @pl.when(j > 0)
    def _():
        rest_ref[...] = w_ref[...].astype(bf16)


def _prep_call(w_in):
    sq = jax.ShapeDtypeStruct((DEPTH, D, D), bf16)
    tab = pl.BlockSpec((GROUP, 2 * GROUP), lambda l, j: (0, 0))
    fold_out = pl.BlockSpec((None, D, D), lambda l, j: (l, 0, 0))
    return pl.pallas_call(
        _prep_kernel,
        out_shape=(sq, sq, jax.ShapeDtypeStruct((DEPTH, D, D_REST), bf16)),
        grid=(DEPTH, 1 + D_REST // D),
        in_specs=[pl.BlockSpec((None, D, D), lambda l, j: (l, 0, j)), tab],
        out_specs=(fold_out, fold_out,
                   pl.BlockSpec((None, D, D), lambda l, j: (l, 0, jnp.maximum(j - 1, 0)))),
        compiler_params=_params("arbitrary", "arbitrary"),
        name="weight_prep",
    )(w_in, jnp.asarray(_GROUP_TABLE))


def _modulated_norm(x, gain, shift, scale):
    ms = jnp.mean(x * x, axis=-1, keepdims=True)
    return (x * lax.rsqrt(ms + EPS) * gain) * (1.0 + scale) + shift


def _inproj_kernel(x_ref, mod_ref, g_ref, cos_ref, sin_ref, w_ref,
                   hb_ref, q_ref, k_ref, v_ref, gs_ref, af_ref, ar_ref):
    hb = _modulated_norm(x_ref[...], g_ref[...], mod_ref[0], mod_ref[1]).astype(bf16)
    hb_ref[...] = hb
    cos = cos_ref[...]
    sin = sin_ref[...]
    half = DK // 2

    def rotary(sec, o_ref, scale):
        p = jnp.dot(hb, w_ref[:, sec * D:(sec + 1) * D], preferred_element_type=f32)
        for h in range(HEADS):
            t1 = p[:, h * DK:h * DK + half]
            t2 = p[:, h * DK + half:(h + 1) * DK]
            o_ref[:, h * DK:h * DK + half] = ((t1 * cos - t2 * sin) * scale).astype(bf16)
            o_ref[:, h * DK + half:(h + 1) * DK] = ((t1 * sin + t2 * cos) * scale).astype(bf16)

    rotary(0, q_ref, DK ** -0.5)
    rotary(1, k_ref, 1.0)
    g = jnp.dot(hb, w_ref[:, 3 * D:4 * D], preferred_element_type=f32)
    gs_ref[...] = (g * jax.nn.sigmoid(g)).astype(bf16)
    af = jnp.dot(hb, w_ref[:, 4 * D:5 * D], preferred_element_type=f32)
    af_ref[...] = jax.nn.sigmoid(af).astype(bf16)
    ar = jnp.dot(hb, w_ref[:, 5 * D:6 * D], preferred_element_type=f32)
    ar_ref[...] = jax.nn.sigmoid(ar).astype(bf16)
    v_ref[...] = jnp.dot(hb, w_ref[:, 2 * D:3 * D], preferred_element_type=f32).astype(bf16)


def _inproj_call(layer, x, mod, gain, cos, sin, w_rest):
    row = pl.BlockSpec((None, TM, D), lambda b, i: (b, i, 0))
    rot = pl.BlockSpec((TM, DK // 2), lambda b, i: (i, 0))
    act = jax.ShapeDtypeStruct((BATCH, SEQ, D), bf16)
    return pl.pallas_call(
        _inproj_kernel,
        out_shape=(act, act, act, act, act, act, act),
        grid=(BATCH, NT),
        in_specs=[
            row,
            pl.BlockSpec((N_MOD, None, 1, D), lambda b, i: (0, b, 0, 0)),
            pl.BlockSpec((1, D), lambda b, i: (0, 0)),
            rot, rot,
            _layer_resident((D, D_REST), layer),
        ],
        out_specs=(row, row, row, row, row, row, row),
        compiler_params=_params("arbitrary", "arbitrary"),
        name="in_projection",
    )(x, mod, gain, cos, sin, w_rest)


def _fourier_kernel(h_ref, wr_ref, wi_ref, m_ref, tr_ref, ti_ref, sr_ref, si_ref):
    hb = jnp.swapaxes(h_ref[...], 0, 1).reshape(FFT_SLABS * R, D)
    zr = jnp.dot(hb, wr_ref[...], preferred_element_type=f32).astype(bf16)
    zi = jnp.dot(hb, wi_ref[...], preferred_element_type=f32).astype(bf16)
    for s in range(FFT_SLABS):
        z = jnp.concatenate([zr[s * R:(s + 1) * R], zi[s * R:(s + 1) * R]], axis=0)
        t = jnp.dot(m_ref[s].astype(bf16), z, preferred_element_type=f32)
        sr_ref[s] = t[:R].astype(bf16)
        si_ref[s] = t[R:].astype(bf16)
    tr_ref[...] = jnp.swapaxes(sr_ref[...], 0, 1)
    ti_ref[...] = jnp.swapaxes(si_ref[...], 0, 1)


def _fourier_call(layer, hb, wr, wi):
    nsteps = R // FFT_SLABS
    slab = pl.BlockSpec((None, R, FFT_SLABS, D), lambda b, j: (b, 0, j, 0))
    t2 = jax.ShapeDtypeStruct((BATCH, R, R, D), bf16)
    return pl.pallas_call(
        _fourier_kernel,
        out_shape=(t2, t2),
        grid=(BATCH, nsteps),
        in_specs=[slab,
                  _layer_resident((D, D), layer), _layer_resident((D, D), layer),
                  pl.BlockSpec((FFT_SLABS, 2 * R, 2 * R), lambda b, j: (j, 0, 0))],
        out_specs=(slab, slab),
        scratch_shapes=[pltpu.VMEM((FFT_SLABS, R, D), bf16), pltpu.VMEM((FFT_SLABS, R, D), bf16)],
        compiler_params=_params("arbitrary", "arbitrary"),
        name="fourier_projection_dft1",
    )(hb.reshape(BATCH, R, R, D), wr, wi, jnp.asarray(_STAGE1))


def _ret_kernel(q_ref, k_ref, v_ref, m_ref, dec_ref, tr_ref, ti_ref, g_ref, wf_ref, wr_ref, wo_ref,
                o_ref, y_ref, wfb_ref, wrb_ref, wob_ref, sf_ref, sb_ref, sb_all_ref, decb_ref, s_ref):
    p = pl.program_id(1)
    j = pl.program_id(2)
    n_chunks = RET_ROWS // RET_CHUNK
    wfb_ref[...] = wf_ref[...].astype(bf16)
    wrb_ref[...] = wr_ref[...].astype(bf16)
    wob_ref[...] = wo_ref[...].astype(bf16)

    @pl.when((pl.program_id(0) == 0) & (p == 0) & (j == 0))
    def _():
        decb_ref[...] = dec_ref[...].astype(bf16)

    @pl.when(j == 0)
    def _():
        sf_ref[...] = jnp.zeros_like(sf_ref)
        sb_ref[...] = jnp.zeros_like(sb_ref)

    @pl.when(p == 0)
    def _():
        block = pl.num_programs(2) - 1 - j
        for c in reversed(range(n_chunks)):
            rows = slice(c * RET_CHUNK, (c + 1) * RET_CHUNK)
            for h in range(HEADS):
                cols = slice(h * DK, (h + 1) * DK)
                state = sb_ref[h]
                sb_all_ref[block * n_chunks + c, h] = state.astype(bf16)
                k_dec = k_ref[rows, cols] * decb_ref[_KDB, h]
                sb_ref[h] = state * float(_CDB[h]) + lax.dot_general(
                    k_dec, v_ref[rows, cols], _TN_DIMS, preferred_element_type=f32)

    @pl.when(p == 1)
    def _():
        for c in range(n_chunks):
            rows = slice(c * RET_CHUNK, (c + 1) * RET_CHUNK)
            for h in range(HEADS):
                cols = slice(h * DK, (h + 1) * DK)
                q = q_ref[rows, cols]
                k = k_ref[rows, cols]
                v = v_ref[rows, cols]
                scores = lax.dot_general(q, k, _NT_DIMS, preferred_element_type=f32) * m_ref[h]
                state = sf_ref[h]
                lhs = jnp.concatenate([scores.astype(bf16), q * decb_ref[_QDF, h],
                                       q * decb_ref[_QDB, h]], axis=1)
                rhs = jnp.concatenate([v, state.astype(bf16), sb_all_ref[j * n_chunks + c, h]],
                                      axis=0)
                y = jnp.dot(lhs, rhs, preferred_element_type=f32)
                k_dec = k * decb_ref[_KDF, h]
                sf_ref[h] = state * float(_CDF[h]) + lax.dot_general(
                    k_dec, v, _TN_DIMS, preferred_element_type=f32)
                o_ref[rows, cols] = y.astype(bf16)
        g = g_ref[...].astype(bf16)
        for s in range(DFT2_SLABS):
            t = jnp.concatenate([tr_ref[s], ti_ref[s]], axis=0)
            s_ref[s] = jnp.dot(g, t, preferred_element_type=f32).astype(bf16)
        y_ref[...] = jnp.swapaxes(s_ref[...], 0, 1)


def _ret_call(layer, q, k, v, tr, ti, w_fourier, w_ret, w_out):
    nb = SEQ // RET_ROWS
    w_rows = D // (BATCH * 2 * nb)

    def w_in_idx(b, p, j):
        return (layer, (b * 2 + p) * nb + j, 0)

    def w_out_idx(b, p, j):
        return ((b * 2 + p) * nb + j, 0)

    w_src = pl.BlockSpec((None, w_rows, D), w_in_idx)
    w_dst = pl.BlockSpec((w_rows, D), w_out_idx)
    w_bf16 = jax.ShapeDtypeStruct((D, D), bf16)

    def scan_idx(b, p, j):
        return (b, j + (1 - p) * (nb - 1 - 2 * j), 0)

    def finish_idx(b, p, j):
        return (b, p * j, 0)

    scan = pl.BlockSpec((None, RET_ROWS, D), scan_idx)
    fin = pl.BlockSpec((None, RET_ROWS, D), finish_idx)
    tin = pl.BlockSpec((None, DFT2_SLABS, R, D), lambda b, p, j: (b, p * j, 0, 0))
    yout = pl.BlockSpec((None, R, DFT2_SLABS, D), lambda b, p, j: (b, 0, p * j, 0))
    return pl.pallas_call(
        _ret_kernel,
        out_shape=(jax.ShapeDtypeStruct((BATCH, SEQ, D), bf16),
                   jax.ShapeDtypeStruct((BATCH, R, R, D), bf16), w_bf16, w_bf16, w_bf16),
        grid=(BATCH, 2, nb),
        in_specs=[fin, scan, scan, _resident((HEADS, RET_CHUNK, RET_CHUNK)),
                  _resident((4, HEADS, RET_CHUNK, DK)), tin, tin, _resident((R, 2 * R)),
                  w_src, w_src, w_src],
        out_specs=(fin, yout, w_dst, w_dst, w_dst),
        scratch_shapes=[pltpu.VMEM((HEADS, DK, DK), f32), pltpu.VMEM((HEADS, DK, DK), f32),
                        pltpu.VMEM((SEQ // RET_CHUNK, HEADS, DK, DK), bf16),
                        pltpu.VMEM((4, HEADS, RET_CHUNK, DK), bf16),
                        pltpu.VMEM((DFT2_SLABS, R, D), bf16)],
        compiler_params=_params("arbitrary", "arbitrary", "arbitrary"),
        name="retention_dft2",
    )(q, k, v, jnp.asarray(_RMASK), jnp.asarray(_RDEC), tr, ti, jnp.asarray(_STAGE2),
      w_fourier, w_ret, w_out)


def _merge_kernel(x_ref, mod_ref, yf_ref, yr_ref, gs_ref, af_ref, ar_ref, wf_ref, wr_ref, wo_ref,
                  up_ref, dn_ref, o_ref, upb_ref, dnb_ref):
    upb_ref[...] = up_ref[...].astype(bf16)
    dnb_ref[...] = dn_ref[...].astype(bf16)
    a = jnp.dot(yf_ref[...], wf_ref[...], preferred_element_type=f32)
    gated = []
    for h in range(HEADS):
        cols = slice(h * DK, (h + 1) * DK)
        y = yr_ref[:, cols].astype(f32)
        mu = jnp.mean(y, axis=-1, keepdims=True)
        yc = y - mu
        var = jnp.mean(yc * yc, axis=-1, keepdims=True)
        yn = yc * lax.rsqrt(var + EPS)
        gated.append((yn * gs_ref[:, cols].astype(f32)).astype(bf16))
    b = jnp.dot(jnp.concatenate(gated, axis=1), wr_ref[...], preferred_element_type=f32)
    merged = af_ref[...].astype(f32) * a + ar_ref[...].astype(f32) * b
    o = jnp.dot(merged.astype(bf16), wo_ref[...], preferred_element_type=f32)
    o_ref[...] = x_ref[...] + mod_ref[2] * o


def _merge_call(layer, x, mod, yf, yr, gs, af, ar, wf, wr, wo, ffn_up, ffn_down):
    steps = BATCH * NT
    up_rows = D // steps
    dn_rows = 2 * D_FF // steps
    row = pl.BlockSpec((None, TM, D), lambda b, i: (b, i, 0))
    return pl.pallas_call(
        _merge_kernel,
        out_shape=(jax.ShapeDtypeStruct((BATCH, SEQ, D), f32),
                   jax.ShapeDtypeStruct((D, 2 * D_FF), bf16), jax.ShapeDtypeStruct((D_FF, D), bf16)),
        grid=(BATCH, NT),
        in_specs=[row, pl.BlockSpec((N_MOD, None, 1, D), lambda b, i: (0, b, 0, 0)),
                  row, row, row, row, row,
                  _resident((D, D)), _resident((D, D)), _resident((D, D)),
                  pl.BlockSpec((None, up_rows, 2 * D_FF), lambda b, i: (layer, b * NT + i, 0)),
                  pl.BlockSpec((None, dn_rows, D), lambda b, i: (layer, (b * NT + i) // 2, 0))],
        out_specs=(row,
                   pl.BlockSpec((up_rows, 2 * D_FF), lambda b, i: (b * NT + i, 0)),
                   pl.BlockSpec((dn_rows, D), lambda b, i: ((b * NT + i) // 2, 0))),
        compiler_params=_params("arbitrary", "arbitrary"),
        name="merge_out_projection",
    )(x, mod, yf, yr, gs, af, ar, wf, wr, wo, ffn_up, ffn_down)


def _ffn_kernel(x_ref, xp_ref, xn_ref, mod_ref, g_ref, wup_ref, cw_ref, cb_ref, wdn_ref, fg_ref,
                o_ref, act_ref, *, final):
    i = pl.program_id(1)
    rows = TM + 2 * HALO
    gain, shift, scale = g_ref[...], mod_ref[3], mod_ref[4]
    keep_prev = (i > 0).astype(f32)
    keep_next = (i < pl.num_programs(1) - 1).astype(f32)
    h = jnp.concatenate([
        _modulated_norm(xp_ref[...], gain, shift, scale) * keep_prev,
        _modulated_norm(x_ref[...], gain, shift, scale),
        _modulated_norm(xn_ref[...], gain, shift, scale) * keep_next,
    ], axis=0).astype(bf16)

    def conv(col):
        u = jnp.dot(h, wup_ref[:, col:col + FF_TILE], preferred_element_type=f32)
        w = cw_ref[:, col:col + FF_TILE]
        full = (pltpu.roll(u, 1, 0) * w[0:1] + u * w[1:2] + pltpu.roll(u, rows - 1, 0) * w[2:3]
                + cb_ref[:, col:col + FF_TILE])
        return full[HALO:HALO + TM]

    for t in range(D_FF // FF_TILE):
        a = conv(t * FF_TILE)
        b = conv(D_FF + t * FF_TILE)
        gelu = 0.5 * a * (1.0 + lax.erf(a * (2.0 ** -0.5)))
        act_ref[:, t * FF_TILE:(t + 1) * FF_TILE] = (gelu * b).astype(bf16)

    y = jnp.dot(act_ref[...], wdn_ref[...], preferred_element_type=f32)
    out = x_ref[...] + mod_ref[5] * y
    if final:
        ms = jnp.mean(out * out, axis=-1, keepdims=True)
        out = out * lax.rsqrt(ms + EPS) * fg_ref[...]
    o_ref[...] = out


def _ffn_call(layer, x, mod, gain, wup, cw, cb, wdn, final_g, final):
    tiles = TM // HALO
    row = pl.BlockSpec((None, TM, D), lambda b, i: (b, i, 0))
    prev = pl.BlockSpec((None, HALO, D), lambda b, i: (b, jnp.maximum(i * tiles - 1, 0), 0))
    nxt = pl.BlockSpec((None, HALO, D),
                       lambda b, i: (b, jnp.minimum((i + 1) * tiles, SEQ // HALO - 1), 0))
    return pl.pallas_call(
        functools.partial(_ffn_kernel, final=final),
        out_shape=jax.ShapeDtypeStruct((BATCH, SEQ, D), f32),
        grid=(BATCH, NT),
        in_specs=[row, prev, nxt,
                  pl.BlockSpec((N_MOD, None, 1, D), lambda b, i: (0, b, 0, 0)),
                  pl.BlockSpec((1, D), lambda b, i: (0, 0)),
                  _resident((D, 2 * D_FF)), _layer_resident((3, 2 * D_FF), layer),
                  _layer_resident((1, 2 * D_FF), layer), _resident((D_FF, D)),
                  pl.BlockSpec((1, D), lambda b, i: (0, 0))],
        out_specs=row,
        scratch_shapes=[pltpu.VMEM((TM, D_FF), bf16)],
        compiler_params=_params("arbitrary", "arbitrary"),
        name="conv_ffn_final" if final else "conv_ffn",
    )(x, x, x, mod, gain, wup, cw, cb, wdn, final_g)


def kernel(x, c, norm1_g, norm2_g, ada_w, ada_b, w_in, w_fourier, w_ret, w_out,
           ffn_up, conv_w, conv_b, ffn_down, final_g):
    assert x.shape == (BATCH, SEQ, D) and c.shape == (BATCH, D)
    mod_all = _ada_call(c, ada_w, ada_b).reshape(DEPTH, N_MOD, BATCH, 1, D)
    wfr_all, wfi_all, w_rest = _prep_call(w_in)
    cos, sin = _rotary_tables()
    final_gain = final_g.reshape(1, D)

    for l in range(DEPTH):
        mod = mod_all[l]
        gain1 = norm1_g[l].reshape(1, D)
        hb, q, k, v, gs, af, ar = _inproj_call(l, x, mod, gain1, cos, sin, w_rest)
        tr, ti = _fourier_call(l, hb, wfr_all, wfi_all)
        yr, yf, wf, wr, wo = _ret_call(l, q, k, v, tr, ti, w_fourier, w_ret, w_out)
        x, wup, wdn = _merge_call(l, x, mod, yf.reshape(BATCH, SEQ, D), yr, gs, af, ar,
                                  wf, wr, wo, ffn_up, ffn_down)
        x = _ffn_call(l, x, mod, norm2_g[l].reshape(1, D), wup, conv_w,
                      conv_b.reshape(DEPTH, 1, 2 * D_FF), wdn, final_gain,
                      final=(l == DEPTH - 1))
    return x
```
